```python
import jax, jax.numpy as jnp
from jax import lax
import numpy as np

D_MODEL = 1024
BATCH = 8
SEQ = 2048
DEPTH = 2
DEC_BATCH = 128
DEC_SEQ = 4
PAST_LEN = 16384
PAGE_SIZE = 128

N_MEM = 256
ATT_HEADS = 4
ATT_HEAD_DIM = D_MODEL // ATT_HEADS
ATT_WIDTH = ATT_HEADS * ATT_HEAD_DIM
W_CONV = D_MODEL
CONV_WIDTH = 3
W_POOL = D_MODEL
POOL_WINDOWS = (2, 4, 8, 16)
N_POOL_GROUPS = 4
POOL_GROUP = W_POOL // N_POOL_GROUPS
POOL_STATE = 15
EPS = 1e-6
IN_SPLITS = (W_CONV, W_CONV, W_CONV, W_CONV, W_POOL, W_POOL, ATT_WIDTH, ATT_WIDTH, D_MODEL, D_MODEL, D_MODEL)
D_IN = sum(IN_SPLITS)

kernel_name = "hybrid_conv_pool_memxattn_decoder_step"


def rmsnorm(x, g):
    xf = x.astype(jnp.float32)
    y = xf * lax.rsqrt(jnp.mean(xf * xf, axis=-1, keepdims=True) + EPS)
    return (y * g.astype(jnp.float32)).astype(x.dtype)


def mem_keys_values(mem, g, w_kv):
    kv = rmsnorm(mem, g) @ w_kv
    k, v = jnp.split(kv, 2, axis=-1)
    b = mem.shape[0]
    return (k.reshape(b, N_MEM, ATT_HEADS, ATT_HEAD_DIM),
            v.reshape(b, N_MEM, ATT_HEADS, ATT_HEAD_DIM))


def causal_multiscale_pool(p_ext, pos):
    b, l, _ = p_ext.shape
    t = l - POOL_STATE
    pf = p_ext.astype(jnp.float32).reshape(b, l, N_POOL_GROUPS, POOL_GROUP)
    cs = jnp.concatenate([jnp.zeros_like(pf[:, :1]), jnp.cumsum(pf, axis=1)], axis=1)
    end = cs[:, POOL_STATE + 1:]
    means = []
    for g, w in enumerate(POOL_WINDOWS):
        start = cs[:, POOL_STATE + 1 - w: POOL_STATE + 1 - w + t, g]
        cnt = jnp.minimum(pos + 1, w).astype(jnp.float32)[None, :, None]
        means.append((end[:, :, g] - start) / cnt)
    mean = jnp.stack(means, axis=2)
    return mean - pf[:, POOL_STATE:]


def layer(x, mem_k, mem_v, conv_prev, pool_prev, pos,
          norm_g, w_in, conv_w, pool_w, pool_scale, w_br_conv, w_br_pool, w_br_att, w_out):
    b, t, _ = x.shape
    h = rmsnorm(x, norm_g)
    z = h @ w_in
    idx = np.cumsum(IN_SPLITS)[:-1].tolist()
    hc, bc, cc, gc, hp, gp, q, ga, mc, mp, ma = jnp.split(z, idx, axis=-1)

    u = cc * hc
    u_ext = jnp.concatenate([conv_prev.astype(u.dtype), u], axis=1)
    y_conv = conv_w[0] * u_ext[:, 0:t]
    for k in range(1, CONV_WIDTH):
        y_conv = y_conv + conv_w[k] * u_ext[:, k:k + t]
    conv_br = (bc * y_conv * jax.nn.silu(gc)) @ w_br_conv

    p_ext = jnp.concatenate([pool_prev.astype(hp.dtype), hp], axis=1)
    mixed = causal_multiscale_pool(p_ext, pos)
    pooled = jnp.einsum('btgc,gcd->btgd', mixed, pool_w.astype(jnp.float32))
    pooled = pooled.reshape(b, t, W_POOL).astype(x.dtype) * pool_scale
    pool_br = (pooled * jax.nn.silu(gp)) @ w_br_pool

    qh = q.reshape(b, t, ATT_HEADS, ATT_HEAD_DIM)
    s = jnp.einsum('bthd,bmhd->bhtm', qh, mem_k).astype(jnp.float32) * (ATT_HEAD_DIM ** -0.5)
    p = jax.nn.softmax(s, axis=-1).astype(x.dtype)
    o = jnp.einsum('bhtm,bmhd->bthd', p, mem_v).reshape(b, t, ATT_WIDTH)
    att_br = (o * jax.nn.silu(ga)) @ w_br_att

    merged = (jax.nn.sigmoid(mc) * conv_br + jax.nn.sigmoid(mp) * pool_br
              + jax.nn.sigmoid(ma) * att_br)
    x_new = x + merged @ w_out
    return x_new, u_ext[:, -(CONV_WIDTH - 1):], p_ext[:, -POOL_STATE:]


def setup_inputs(seed: int = 0) -> dict:
    key = jax.random.key(seed)
    ks = jax.random.split(key, 24)
    f32 = jnp.float32
    nrm = lambda k, shape, scale: jax.random.normal(k, shape, f32) * scale
    return {
        "x_prompt": nrm(ks[0], (BATCH, SEQ, D_MODEL), 1.0),
        "x_sample": nrm(ks[1], (DEC_BATCH, DEC_SEQ, D_MODEL), 1.0),
        "mem_prompt": nrm(ks[2], (BATCH, N_MEM, D_MODEL), 1.0),
        "cache_mem_k": nrm(ks[3], (DEPTH, DEC_BATCH, N_MEM, ATT_HEADS, ATT_HEAD_DIM), 1.0),
        "cache_mem_v": nrm(ks[4], (DEPTH, DEC_BATCH, N_MEM, ATT_HEADS, ATT_HEAD_DIM), 1.0),
        "state_conv": nrm(ks[5], (DEPTH, DEC_BATCH, CONV_WIDTH - 1, W_CONV), 1.0),
        "state_pool": nrm(ks[6], (DEPTH, DEC_BATCH, POOL_STATE, W_POOL), 1.0),
        "norm_g": 1.0 + nrm(ks[7], (DEPTH, D_MODEL), 0.05),
        "w_in": nrm(ks[8], (DEPTH, D_MODEL, D_IN), D_MODEL ** -0.5),
        "conv_w": nrm(ks[9], (DEPTH, CONV_WIDTH, W_CONV), CONV_WIDTH ** -0.5),
        "pool_w": nrm(ks[10], (DEPTH, N_POOL_GROUPS, POOL_GROUP, POOL_GROUP), POOL_GROUP ** -0.5),
        "pool_scale": 1.0 + nrm(ks[11], (DEPTH, W_POOL), 0.1),
        "mem_norm_g": 1.0 + nrm(ks[12], (DEPTH, D_MODEL), 0.05),
        "w_mem_kv": nrm(ks[13], (DEPTH, D_MODEL, 2 * ATT_WIDTH), D_MODEL ** -0.5),
        "w_br_conv": nrm(ks[14], (DEPTH, W_CONV, D_MODEL), W_CONV ** -0.5),
        "w_br_pool": nrm(ks[15], (DEPTH, W_POOL, D_MODEL), W_POOL ** -0.5),
        "w_br_att": nrm(ks[16], (DEPTH, ATT_WIDTH, D_MODEL), ATT_WIDTH ** -0.5),
        "w_out": nrm(ks[17], (DEPTH, D_MODEL, D_MODEL), D_MODEL ** -0.5),
        "final_norm_g": 1.0 + nrm(ks[18], (D_MODEL,), 0.05),
    }


def reference(x_prompt, x_sample, mem_prompt, cache_mem_k, cache_mem_v, state_conv, state_pool,
              norm_g, w_in, conv_w, pool_w, pool_scale, mem_norm_g, w_mem_kv,
              w_br_conv, w_br_pool, w_br_att, w_out, final_norm_g):
    b_p, t_p, _ = x_prompt.shape
    t_s = x_sample.shape[1]
    pos_p = jnp.arange(t_p, dtype=jnp.int32)
    pos_s = PAST_LEN + jnp.arange(t_s, dtype=jnp.int32)
    xp, xs = x_prompt, x_sample
    mk_p, mv_p, cv_p, pl_p, cv_s, pl_s = [], [], [], [], [], []
    for l in range(DEPTH):
        lw = (norm_g[l], w_in[l], conv_w[l], pool_w[l], pool_scale[l],
              w_br_conv[l], w_br_pool[l], w_br_att[l], w_out[l])
        k_p, v_p = mem_keys_values(mem_prompt, mem_norm_g[l], w_mem_kv[l])
        conv0 = jnp.zeros((b_p, CONV_WIDTH - 1, W_CONV), xp.dtype)
        pool0 = jnp.zeros((b_p, POOL_STATE, W_POOL), xp.dtype)
        xp, c_new, p_new = layer(xp, k_p, v_p, conv0, pool0, pos_p, *lw)
        mk_p.append(k_p); mv_p.append(v_p); cv_p.append(c_new); pl_p.append(p_new)
        xs, c_new_s, p_new_s = layer(xs, cache_mem_k[l], cache_mem_v[l], state_conv[l], state_pool[l],
                                     pos_s, *lw)
        cv_s.append(c_new_s); pl_s.append(p_new_s)
    y_prompt = rmsnorm(xp, final_norm_g)
    y_sample = rmsnorm(xs, final_norm_g)
    return (y_prompt, y_sample, jnp.stack(mk_p), jnp.stack(mv_p), jnp.stack(cv_p), jnp.stack(pl_p),
            jnp.stack(cv_s), jnp.stack(pl_s))
```

```python
import functools

import jax
import jax.numpy as jnp
from jax import lax
from jax.experimental import pallas as pl
from jax.experimental.pallas import tpu as pltpu

D_MODEL = 1024
N_MEM = 256
HEADS = 4
HEAD_DIM = 256
CONV_WIDTH = 3
POOL_WINDOWS = (2, 4, 8, 16)
POOL_GROUP = 256
POOL_STATE = 15
PAST_LEN = 16384
EPS = 1e-6
OFF_HC, OFF_BC, OFF_CC, OFF_GC = 0, 1024, 2048, 3072
OFF_HP, OFF_GP = 4096, 5120
OFF_Q = 6144
D_IN_A = 7168
OFF_GA, OFF_MC, OFF_MP, OFF_MA = 0, 1024, 2048, 3072
D_IN_C = 4096

CW = 256
N_CHUNKS = D_MODEL // CW
TM = 512
POOL_PAD = 32
CONV_PAD = 8
SB = 4
VMEM_LIMIT = 62 * 1024 * 1024

F32 = jnp.float32
BF16 = jnp.bfloat16


def _dot(a, b):
    return jnp.dot(a, b, preferred_element_type=F32)


def _dot_nt(a, b):
    return lax.dot_general(a, b, (((1,), (1,)), ((), ())), preferred_element_type=F32)


def _sigmoid(x):
    return 0.5 * jnp.tanh(0.5 * x) + 0.5


def _silu(x):
    return x * _sigmoid(x)


def _rmsnorm(x, g):
    ms = jnp.mean(x * x, axis=-1, keepdims=True)
    return (x * lax.rsqrt(ms + EPS)) * g


def _softmax(s):
    e = jnp.exp(s - jnp.max(s, axis=-1, keepdims=True))
    return e * (1.0 / jnp.sum(e, axis=-1, keepdims=True))


def _cols(off, c, w=CW):
    return slice(off + c * w, off + (c + 1) * w)


def _params(n_axes):
    return pltpu.CompilerParams(
        dimension_semantics=("arbitrary",) * n_axes, vmem_limit_bytes=VMEM_LIMIT)


def _kv_kernel(mem_ref, g_ref, w_ref, k_ref, v_ref, kb_ref, vb_ref):
    h = _rmsnorm(mem_ref[...], g_ref[0]).astype(BF16)
    k = _dot(h, w_ref[0, :, :D_MODEL])
    v = _dot(h, w_ref[0, :, D_MODEL:])
    k_ref[0] = k
    v_ref[0] = v
    kb_ref[0] = k.astype(BF16)
    vb_ref[0] = v.astype(BF16)


def _mem_kv(mem2d, mem_norm_g, w_kv_bf16):
    depth = w_kv_bf16.shape[0]
    rows = mem2d.shape[0]
    rt = 512
    out_f = jax.ShapeDtypeStruct((depth, rows, D_MODEL), F32)
    out_b = jax.ShapeDtypeStruct((depth, rows, D_MODEL), BF16)
    blk = pl.BlockSpec((1, rt, D_MODEL), lambda l, r: (l, r, 0))
    return pl.pallas_call(
        _kv_kernel,
        grid=(depth, rows // rt),
        in_specs=[
            pl.BlockSpec((rt, D_MODEL), lambda l, r: (r, 0)),
            pl.BlockSpec((1, 1, D_MODEL), lambda l, r: (l, 0, 0)),
            pl.BlockSpec((1, D_MODEL, 2 * D_MODEL), lambda l, r: (l, 0, 0)),
        ],
        out_specs=[blk, blk, blk, blk],
        out_shape=[out_f, out_f, out_b, out_b],
        compiler_params=_params(2),
        name="mem_kv",
    )(mem2d, mem_norm_g.reshape(depth, 1, D_MODEL), w_kv_bf16)


def _att_gate(o, h, aa_s, winc_ref):
    for hh in range(HEADS):
        sl = _cols(0, hh, HEAD_DIM)
        ga = _dot(h, winc_ref[:, _cols(OFF_GA, hh, HEAD_DIM)])
        aa_s[:, sl] = (o[:, sl] * _silu(ga)).astype(BF16)


def _merge_and_out(x, h, ac, ap, aa, m_s, winc_ref, wbc_ref, wbp_ref, wba_ref, wout_ref):
    for c in range(N_CHUNKS):
        sl = _cols(0, c)
        conv_br = _dot(ac, wbc_ref[:, sl])
        pool_br = _dot(ap, wbp_ref[:, sl])
        att_br = _dot(aa, wba_ref[:, sl])
        mc = _dot(h, winc_ref[:, _cols(OFF_MC, c)])
        mp = _dot(h, winc_ref[:, _cols(OFF_MP, c)])
        ma = _dot(h, winc_ref[:, _cols(OFF_MA, c)])
        merged = _sigmoid(mc) * conv_br + _sigmoid(mp) * pool_br + _sigmoid(ma) * att_br
        m_s[:, sl] = merged.astype(BF16)
    return x + _dot(m_s[...], wout_ref[...])


def _prompt_layer_kernel(x_ref, k_ref, v_ref, g_ref, wina_ref, winc_ref, convw_ref, poolw_ref, pscale_ref,
                         wbc_ref, wbp_ref, wba_ref, wout_ref, fg_ref,
                         y_ref, cst_ref, pst_ref,
                         u_s, p_s, sa_s, sb_s, ac_s, ap_s, aa_s, m_s, *, final_norm):
    t = pl.program_id(1)

    @pl.when(t == 0)
    def _():
        u_s[0:CONV_PAD, :] = jnp.zeros((CONV_PAD, D_MODEL), F32)
        p_s[0:POOL_PAD, :] = jnp.zeros((POOL_PAD, D_MODEL), F32)
        sa_s[0:16, :] = jnp.zeros((16, POOL_GROUP), F32)
        sb_s[0:16, :] = jnp.zeros((16, POOL_GROUP), F32)

    x = x_ref[0]
    h = _rmsnorm(x, g_ref[...]).astype(BF16)

    for c in range(N_CHUNKS):
        sl = _cols(0, c)
        hc = _dot(h, wina_ref[:, _cols(OFF_HC, c)])
        cc = _dot(h, wina_ref[:, _cols(OFF_CC, c)])
        u = cc * hc
        u_s[CONV_PAD:CONV_PAD + TM, sl] = u
        u1 = u_s[CONV_PAD - 1:CONV_PAD - 1 + TM, sl]
        u2 = u_s[CONV_PAD - 2:CONV_PAD - 2 + TM, sl]
        y = convw_ref[0:1, sl] * u2 + convw_ref[1:2, sl] * u1 + convw_ref[2:3, sl] * u
        bc = _dot(h, wina_ref[:, _cols(OFF_BC, c)])
        gc = _dot(h, wina_ref[:, _cols(OFF_GC, c)])
        ac_s[:, sl] = (bc * y * _silu(gc)).astype(BF16)
    new_conv = u_s[CONV_PAD + TM - 2:CONV_PAD + TM, :]
    cst_ref[0] = new_conv
    u_s[CONV_PAD - 2:CONV_PAD, :] = new_conv

    pos1 = (t * TM + 1 + lax.broadcasted_iota(jnp.int32, (TM, 1), 0)).astype(F32)
    n = TM + 16
    for g, w in enumerate(POOL_WINDOWS):
        sl = _cols(0, g, POOL_GROUP)
        hp = _dot(h, wina_ref[:, _cols(OFF_HP, g, POOL_GROUP)])
        p_s[POOL_PAD:POOL_PAD + TM, sl] = hp
        cur = p_s[16:16 + n, sl] + p_s[15:15 + n, sl]
        shift, src, dst = 2, sa_s, sb_s
        while shift < w:
            src[16:16 + n, :] = cur
            cur = src[16:16 + n, :] + src[16 - shift:16 - shift + n, :]
            shift *= 2
            src, dst = dst, src
        inv_cnt = 1.0 / jnp.minimum(pos1, float(w))
        mixed = cur[16:, :] * inv_cnt - hp
        pooled = _dot(mixed.astype(BF16), poolw_ref[g]) * pscale_ref[:, sl]
        gp = _dot(h, wina_ref[:, _cols(OFF_GP, g, POOL_GROUP)])
        ap_s[:, sl] = (pooled * _silu(gp)).astype(BF16)
    pst_ref[0] = p_s[POOL_PAD + TM - POOL_STATE:POOL_PAD + TM, :]
    p_s[16:POOL_PAD, :] = p_s[TM + 16:TM + POOL_PAD, :]

    scale = HEAD_DIM ** -0.5
    for hh in range(HEADS):
        sl = _cols(0, hh, HEAD_DIM)
        q = _dot(h, wina_ref[:, _cols(OFF_Q, hh, HEAD_DIM)]).astype(BF16)
        p = _softmax(_dot_nt(q, k_ref[0, :, sl]) * scale)
        o = _dot(p.astype(BF16), v_ref[0, :, sl])
        ga = _dot(h, winc_ref[:, _cols(OFF_GA, hh, HEAD_DIM)])
        aa_s[:, sl] = (o * _silu(ga)).astype(BF16)

    xn = _merge_and_out(x, h, ac_s[...], ap_s[...], aa_s[...], m_s,
                        winc_ref, wbc_ref, wbp_ref, wba_ref, wout_ref)
    if final_norm:
        xn = _rmsnorm(xn, fg_ref[...])
    y_ref[0] = xn


def _const_spec(shape):
    nd = len(shape)
    return pl.BlockSpec(shape, lambda *_: (0,) * nd, pipeline_mode=pl.Buffered(1))


def _prompt_layer(x, kb, vb, lw, final_norm):
    b, t, _ = x.shape
    nt = t // TM
    weights = (lw["g"], lw["win_a"], lw["win_c"], lw["conv_w"], lw["pool_w"], lw["pool_scale"],
               lw["wbc"], lw["wbp"], lw["wba"], lw["wout"], lw["fg"])
    return pl.pallas_call(
        functools.partial(_prompt_layer_kernel, final_norm=final_norm),
        grid=(b, nt),
        in_specs=[
            pl.BlockSpec((1, TM, D_MODEL), lambda i, j: (i, j, 0)),
            pl.BlockSpec((1, N_MEM, D_MODEL), lambda i, j: (i, 0, 0)),
            pl.BlockSpec((1, N_MEM, D_MODEL), lambda i, j: (i, 0, 0)),
        ] + [_const_spec(w.shape) for w in weights],
        out_specs=[
            pl.BlockSpec((1, TM, D_MODEL), lambda i, j: (i, j, 0)),
            pl.BlockSpec((1, CONV_WIDTH - 1, D_MODEL), lambda i, j: (i, 0, 0)),
            pl.BlockSpec((1, POOL_STATE, D_MODEL), lambda i, j: (i, 0, 0)),
        ],
        out_shape=[
            jax.ShapeDtypeStruct((b, t, D_MODEL), F32),
            jax.ShapeDtypeStruct((b, CONV_WIDTH - 1, D_MODEL), F32),
            jax.ShapeDtypeStruct((b, POOL_STATE, D_MODEL), F32),
        ],
        scratch_shapes=[
            pltpu.VMEM((CONV_PAD + TM, D_MODEL), F32),
            pltpu.VMEM((POOL_PAD + TM, D_MODEL), F32),
            pltpu.VMEM((POOL_PAD + TM, POOL_GROUP), F32),
            pltpu.VMEM((POOL_PAD + TM, POOL_GROUP), F32),
            pltpu.VMEM((TM, D_MODEL), BF16),
            pltpu.VMEM((TM, D_MODEL), BF16),
            pltpu.VMEM((TM, D_MODEL), BF16),
            pltpu.VMEM((TM, D_MODEL), BF16),
        ],
        compiler_params=_params(2),
        name="prompt_layer",
    )(x, kb, vb, *weights)


def _sample_in_kernel(x_ref, cst_ref, pst_ref, g_ref, wina_ref, convw_ref, poolw_ref, pscale_ref,
                      ac_ref, ap_ref, q_ref, u_ref, hp_ref, *, n_t, n_b):
    rows = n_t * n_b
    x = x_ref[...].reshape(rows, D_MODEL)
    h = _rmsnorm(x, g_ref[...]).astype(BF16)

    for c in range(N_CHUNKS):
        sl = _cols(0, c)
        hc = _dot(h, wina_ref[:, _cols(OFF_HC, c)])
        cc = _dot(h, wina_ref[:, _cols(OFF_CC, c)])
        u = cc * hc
        u_ref[:, :, sl] = u.reshape(n_t, n_b, CW)
        ext = [cst_ref[0, :, sl], cst_ref[1, :, sl]] + [u[i * n_b:(i + 1) * n_b] for i in range(n_t)]
        y = jnp.concatenate(
            [convw_ref[0:1, sl] * ext[i] + convw_ref[1:2, sl] * ext[i + 1] + convw_ref[2:3, sl] * ext[i + 2]
             for i in range(n_t)], axis=0)
        bc = _dot(h, wina_ref[:, _cols(OFF_BC, c)])
        gc = _dot(h, wina_ref[:, _cols(OFF_GC, c)])
        ac_ref[:, sl] = (bc * y * _silu(gc)).astype(BF16)

    for g, w in enumerate(POOL_WINDOWS):
        sl = _cols(0, g, POOL_GROUP)
        hp = _dot(h, wina_ref[:, _cols(OFF_HP, g, POOL_GROUP)])
        hp_ref[:, :, sl] = hp.reshape(n_t, n_b, POOL_GROUP)
        ext = [pst_ref[j, :, sl] for j in range(POOL_STATE)] + [hp[i * n_b:(i + 1) * n_b] for i in range(n_t)]
        mixed = []
        for i in range(n_t):
            cnt = float(min(PAST_LEN + i + 1, w))
            acc = ext[POOL_STATE + i]
            for j in range(1, w):
                acc = acc + ext[POOL_STATE + i - j]
            mixed.append(acc * (1.0 / cnt) - ext[POOL_STATE + i])
        mixed = jnp.concatenate(mixed, axis=0)
        pooled = _dot(mixed.astype(BF16), poolw_ref[g]) * pscale_ref[:, sl]
        gp = _dot(h, wina_ref[:, _cols(OFF_GP, g, POOL_GROUP)])
        ap_ref[:, sl] = (pooled * _silu(gp)).astype(BF16)

    for hh in range(HEADS):
        q_ref[:, _cols(0, hh, HEAD_DIM)] = _dot(h, wina_ref[:, _cols(OFF_Q, hh, HEAD_DIM)])


def _sample_in(x_t, cst_t, pst_t, lw):
    n_t, n_b, _ = x_t.shape
    rows = n_t * n_b
    tb = (n_t, n_b, D_MODEL)
    ins = (x_t, cst_t, pst_t, lw["g"], lw["win_a"], lw["conv_w"], lw["pool_w"], lw["pool_scale"])
    return pl.pallas_call(
        functools.partial(_sample_in_kernel, n_t=n_t, n_b=n_b),
        out_shape=[
            jax.ShapeDtypeStruct((rows, D_MODEL), BF16),
            jax.ShapeDtypeStruct((rows, D_MODEL), BF16),
            jax.ShapeDtypeStruct((rows, D_MODEL), F32),
            jax.ShapeDtypeStruct(tb, F32),
            jax.ShapeDtypeStruct(tb, F32),
        ],
        compiler_params=pltpu.CompilerParams(vmem_limit_bytes=VMEM_LIMIT),
        name="sample_in",
    )(*ins)


def _sample_attn_kernel(q_ref, k_ref, v_ref, o_ref, *, n_t):
    scale = HEAD_DIM ** -0.5
    n_flat = N_MEM * HEADS
    col_head = lax.broadcasted_iota(jnp.int32, (HEADS * n_t, n_flat), 1) % HEADS
    row_head = lax.broadcasted_iota(jnp.int32, (HEADS * n_t, n_flat), 0) // n_t
    own = col_head == row_head
    for i in range(SB):
        qb = q_ref[i]
        qe = jnp.concatenate([qb[:, _cols(0, hh, HEAD_DIM)] for hh in range(HEADS)], axis=0).astype(BF16)
        kf = k_ref[0, i].reshape(n_flat, HEAD_DIM).astype(BF16)
        vf = v_ref[0, i].reshape(n_flat, HEAD_DIM).astype(BF16)
        s = jnp.where(own, _dot_nt(qe, kf) * scale, -jnp.inf)
        o = _dot(_softmax(s).astype(BF16), vf)
        o_ref[i] = jnp.concatenate([o[hh * n_t:(hh + 1) * n_t] for hh in range(HEADS)], axis=1)


def _sample_attn(q_b, cache_k, cache_v, layer):
    n_b, n_t, _ = q_b.shape
    kv_spec = pl.BlockSpec((1, SB, N_MEM, HEADS, HEAD_DIM), lambda s: (layer, s, 0, 0, 0))
    qo_spec = pl.BlockSpec((SB, n_t, D_MODEL), lambda s: (s, 0, 0))
    return pl.pallas_call(
        functools.partial(_sample_attn_kernel, n_t=n_t),
        grid=(n_b // SB,),
        in_specs=[qo_spec, kv_spec, kv_spec],
        out_specs=qo_spec,
        out_shape=jax.ShapeDtypeStruct(q_b.shape, F32),
        compiler_params=_params(1),
        name="sample_attn",
    )(q_b, cache_k, cache_v)


def _sample_out_kernel(x_ref, o_ref, ac_ref, ap_ref, g_ref, winc_ref, wbc_ref, wbp_ref, wba_ref, wout_ref,
                       fg_ref, y_ref, aa_s, m_s, *, final_norm):
    x = x_ref[...]
    h = _rmsnorm(x, g_ref[...]).astype(BF16)
    _att_gate(o_ref[...], h, aa_s, winc_ref)
    xn = _merge_and_out(x, h, ac_ref[...], ap_ref[...], aa_s[...], m_s,
                        winc_ref, wbc_ref, wbp_ref, wba_ref, wout_ref)
    if final_norm:
        xn = _rmsnorm(xn, fg_ref[...])
    y_ref[...] = xn


def _sample_out(x2d, o2d, ac, ap, lw, final_norm):
    rows = x2d.shape[0]
    ins = (x2d, o2d, ac, ap, lw["g"], lw["win_c"], lw["wbc"], lw["wbp"], lw["wba"], lw["wout"], lw["fg"])
    return pl.pallas_call(
        functools.partial(_sample_out_kernel, final_norm=final_norm),
        out_shape=jax.ShapeDtypeStruct((rows, D_MODEL), F32),
        scratch_shapes=[pltpu.VMEM((rows, D_MODEL), BF16), pltpu.VMEM((rows, D_MODEL), BF16)],
        compiler_params=pltpu.CompilerParams(vmem_limit_bytes=VMEM_LIMIT),
        name="sample_out",
    )(*ins)


def _to_batch_major(a2d, n_t, n_b):
    return jnp.transpose(a2d.reshape(n_t, n_b, D_MODEL), (1, 0, 2))


def kernel(x_prompt, x_sample, mem_prompt, cache_mem_k, cache_mem_v, state_conv, state_pool, norm_g, w_in, conv_w, pool_w, pool_scale, mem_norm_g, w_mem_kv, w_br_conv, w_br_pool, w_br_att, w_out, final_norm_g):
    depth = w_in.shape[0]
    b_p = x_prompt.shape[0]
    n_b, n_t, _ = x_sample.shape
    rows = n_b * n_t
    kv_shape = (depth, b_p, N_MEM, HEADS, HEAD_DIM)

    k_f, v_f, k_b, v_b = _mem_kv(mem_prompt.reshape(b_p * N_MEM, D_MODEL), mem_norm_g, w_mem_kv.astype(BF16))

    xp, xs = x_prompt, x_sample
    cv_p, pl_p, cv_s, pl_s = [], [], [], []
    for l in range(depth):
        final = l == depth - 1
        lw = dict(
            g=norm_g[l].reshape(1, D_MODEL),
            win_a=w_in[l, :, :D_IN_A].astype(BF16), win_c=w_in[l, :, D_IN_A:].astype(BF16),
            conv_w=conv_w[l], pool_w=pool_w[l].astype(BF16), pool_scale=pool_scale[l].reshape(1, D_MODEL),
            wbc=w_br_conv[l].astype(BF16), wbp=w_br_pool[l].astype(BF16), wba=w_br_att[l].astype(BF16),
            wout=w_out[l].astype(BF16), fg=final_norm_g.reshape(1, D_MODEL))

        xp, c_new, p_new = _prompt_layer(
            xp, k_b[l].reshape(b_p, N_MEM, D_MODEL), v_b[l].reshape(b_p, N_MEM, D_MODEL), lw, final)
        cv_p.append(c_new)
        pl_p.append(p_new)

        ac, ap, q, u_t, hp_t = _sample_in(
            jnp.transpose(xs, (1, 0, 2)), jnp.transpose(state_conv[l], (1, 0, 2)),
            jnp.transpose(state_pool[l], (1, 0, 2)), lw)
        o = _sample_attn(_to_batch_major(q, n_t, n_b), cache_mem_k, cache_mem_v, l)
        xs = _sample_out(
            xs.reshape(rows, D_MODEL), o.reshape(rows, D_MODEL),
            _to_batch_major(ac, n_t, n_b).reshape(rows, D_MODEL),
            _to_batch_major(ap, n_t, n_b).reshape(rows, D_MODEL), lw, final).reshape(n_b, n_t, D_MODEL)
        cv_s.append(jnp.transpose(u_t[n_t - (CONV_WIDTH - 1):], (1, 0, 2)))
        pl_s.append(jnp.concatenate([state_pool[l][:, n_t:], jnp.transpose(hp_t, (1, 0, 2))], axis=1))

    return (xp, xs, k_f.reshape(kv_shape), v_f.reshape(kv_shape),
            jnp.stack(cv_p), jnp.stack(pl_p), jnp.stack(cv_s), jnp.stack(pl_s))
```

```python
import functools

import jax
import jax.numpy as jnp
from jax import lax
from jax.experimental import pallas as pl
from jax.experimental.pallas import tpu as pltpu

D_MODEL = 1024
N_MEM = 256
HEADS = 4
HEAD_DIM = 256
CONV_WIDTH = 3
POOL_WINDOWS = (2, 4, 8, 16)
POOL_GROUP = 256
POOL_STATE = 15
PAST_LEN = 16384
EPS = 1e-6
OFF_HC, OFF_BC, OFF_CC, OFF_GC = 0, 1024, 2048, 3072
OFF_HP, OFF_GP = 4096, 5120
OFF_Q = 6144
D_IN_A = 7168
C_BLOCKS = (7, 8, 9, 10)

CW = 256
N_CHUNKS = D_MODEL // CW
TM = 512
POOL_PAD = 32
CONV_PAD = 8
SB = 4
VMEM_LIMIT = 62 * 1024 * 1024

F32 = jnp.float32
BF16 = jnp.bfloat16


def _dot(a, b):
    return jnp.dot(a, b, preferred_element_type=F32)


def _dot_nt(a, b):
    return lax.dot_general(a, b, (((1,), (1,)), ((), ())), preferred_element_type=F32)


def _sigmoid(x):
    return 0.5 * jnp.tanh(0.5 * x) + 0.5


def _silu(x):
    return x * _sigmoid(x)


def _rmsnorm(x, g):
    ms = jnp.mean(x * x, axis=-1, keepdims=True)
    return (x * lax.rsqrt(ms + EPS)) * g


def _softmax(s):
    e = jnp.exp(s - jnp.max(s, axis=-1, keepdims=True))
    return e * (1.0 / jnp.sum(e, axis=-1, keepdims=True))


def _cols(off, c, w=CW):
    return slice(off + c * w, off + (c + 1) * w)


def _params(n_axes):
    return pltpu.CompilerParams(
        dimension_semantics=("arbitrary",) * n_axes, vmem_limit_bytes=VMEM_LIMIT)


def _kv_kernel(mem_ref, g_ref, w_ref, k_ref, v_ref, kb_ref, vb_ref):
    h = _rmsnorm(mem_ref[...], g_ref[0]).astype(BF16)
    k = _dot(h, w_ref[0, :, :D_MODEL])
    v = _dot(h, w_ref[0, :, D_MODEL:])
    k_ref[0] = k.reshape(k_ref.shape[1:])
    v_ref[0] = v.reshape(v_ref.shape[1:])
    kb_ref[0] = k.astype(BF16)
    vb_ref[0] = v.astype(BF16)


def _mem_kv(mem2d, mem_norm_g, w_kv_bf16):
    depth = w_kv_bf16.shape[0]
    rows = mem2d.shape[0]
    bt = 2
    rt = bt * N_MEM
    out_f = jax.ShapeDtypeStruct((depth, rows // N_MEM, N_MEM, HEADS, HEAD_DIM), F32)
    out_b = jax.ShapeDtypeStruct((depth, rows, D_MODEL), BF16)
    blk = pl.BlockSpec((1, rt, D_MODEL), lambda l, r: (l, r, 0))
    blk5 = pl.BlockSpec((1, bt, N_MEM, HEADS, HEAD_DIM), lambda l, r: (l, r, 0, 0, 0))
    return pl.pallas_call(
        _kv_kernel,
        grid=(depth, rows // rt),
        in_specs=[
            pl.BlockSpec((rt, D_MODEL), lambda l, r: (r, 0)),
            pl.BlockSpec((1, 1, D_MODEL), lambda l, r: (l, 0, 0)),
            pl.BlockSpec((1, D_MODEL, 2 * D_MODEL), lambda l, r: (l, 0, 0)),
        ],
        out_specs=[blk5, blk5, blk, blk],
        out_shape=[out_f, out_f, out_b, out_b],
        compiler_params=_params(2),
        name="mem_kv",
    )(mem2d, mem_norm_g.reshape(depth, 1, D_MODEL), w_kv_bf16)


def _att_gate(o, h, aa_s, wga_ref):
    for hh in range(HEADS):
        sl = _cols(0, hh, HEAD_DIM)
        ga = _dot(h, wga_ref[:, sl])
        aa_s[:, sl] = (o[:, sl] * _silu(ga)).astype(BF16)


def _merge_and_out(x, h, ac, ap, aa, m_s, wmc_ref, wmp_ref, wma_ref, wbc_ref, wbp_ref, wba_ref, wout_ref):
    for c in range(N_CHUNKS):
        sl = _cols(0, c)
        conv_br = _dot(ac, wbc_ref[:, sl])
        pool_br = _dot(ap, wbp_ref[:, sl])
        att_br = _dot(aa, wba_ref[:, sl])
        mc = _dot(h, wmc_ref[:, sl])
        mp = _dot(h, wmp_ref[:, sl])
        ma = _dot(h, wma_ref[:, sl])
        merged = _sigmoid(mc) * conv_br + _sigmoid(mp) * pool_br + _sigmoid(ma) * att_br
        m_s[:, sl] = merged.astype(BF16)
    return x + _dot(m_s[...], wout_ref[...])


def _prompt_layer_kernel(x_ref, k_ref, v_ref, g_ref, wina_ref, wga_ref, wmc_ref, wmp_ref, wma_ref,
                         convw_ref, poolw_ref, pscale_ref, wbc_ref, wbp_ref, wba_ref, wout_ref, fg_ref,
                         y_ref, cst_ref, pst_ref,
                         u_s, p_s, sa_s, sb_s, ac_s, ap_s, aa_s, m_s, *, final_norm):
    t = pl.program_id(1)

    @pl.when(t == 0)
    def _():
        u_s[0:CONV_PAD, :] = jnp.zeros((CONV_PAD, D_MODEL), F32)
        p_s[0:POOL_PAD, :] = jnp.zeros((POOL_PAD, D_MODEL), F32)
        sa_s[0:16, :] = jnp.zeros((16, POOL_GROUP), F32)
        sb_s[0:16, :] = jnp.zeros((16, POOL_GROUP), F32)

    x = x_ref[0]
    h = _rmsnorm(x, g_ref[...]).astype(BF16)

    for c in range(N_CHUNKS):
        sl = _cols(0, c)
        hc = _dot(h, wina_ref[:, _cols(OFF_HC, c)])
        cc = _dot(h, wina_ref[:, _cols(OFF_CC, c)])
        u = cc * hc
        u_s[CONV_PAD:CONV_PAD + TM, sl] = u
        u1 = u_s[CONV_PAD - 1:CONV_PAD - 1 + TM, sl]
        u2 = u_s[CONV_PAD - 2:CONV_PAD - 2 + TM, sl]
        y = convw_ref[0:1, sl] * u2 + convw_ref[1:2, sl] * u1 + convw_ref[2:3, sl] * u
        bc = _dot(h, wina_ref[:, _cols(OFF_BC, c)])
        gc = _dot(h, wina_ref[:, _cols(OFF_GC, c)])
        ac_s[:, sl] = (bc * y * _silu(gc)).astype(BF16)
    new_conv = u_s[CONV_PAD + TM - 2:CONV_PAD + TM, :]
    cst_ref[0] = new_conv
    u_s[CONV_PAD - 2:CONV_PAD, :] = new_conv

    pos1 = (t * TM + 1 + lax.broadcasted_iota(jnp.int32, (TM, 1), 0)).astype(F32)
    n = TM + 16
    for g, w in enumerate(POOL_WINDOWS):
        sl = _cols(0, g, POOL_GROUP)
        hp = _dot(h, wina_ref[:, _cols(OFF_HP, g, POOL_GROUP)])
        p_s[POOL_PAD:POOL_PAD + TM, sl] = hp
        cur = p_s[16:16 + n, sl] + p_s[15:15 + n, sl]
        shift, src, dst = 2, sa_s, sb_s
        while shift < w:
            src[16:16 + n, :] = cur
            cur = src[16:16 + n, :] + src[16 - shift:16 - shift + n, :]
            shift *= 2
            src, dst = dst, src
        inv_cnt = 1.0 / jnp.minimum(pos1, float(w))
        mixed = cur[16:, :] * inv_cnt - hp
        pooled = _dot(mixed.astype(BF16), poolw_ref[g]) * pscale_ref[:, sl]
        gp = _dot(h, wina_ref[:, _cols(OFF_GP, g, POOL_GROUP)])
        ap_s[:, sl] = (pooled * _silu(gp)).astype(BF16)
    pst_ref[0] = p_s[POOL_PAD + TM - POOL_STATE:POOL_PAD + TM, :]
    p_s[16:POOL_PAD, :] = p_s[TM + 16:TM + POOL_PAD, :]

    scale = HEAD_DIM ** -0.5
    for hh in range(HEADS):
        sl = _cols(0, hh, HEAD_DIM)
        q = _dot(h, wina_ref[:, _cols(OFF_Q, hh, HEAD_DIM)]).astype(BF16)
        p = _softmax(_dot_nt(q, k_ref[:, sl]) * scale)
        o = _dot(p.astype(BF16), v_ref[:, sl])
        ga = _dot(h, wga_ref[:, sl])
        aa_s[:, sl] = (o * _silu(ga)).astype(BF16)

    xn = _merge_and_out(x, h, ac_s[...], ap_s[...], aa_s[...], m_s,
                        wmc_ref, wmp_ref, wma_ref, wbc_ref, wbp_ref, wba_ref, wout_ref)
    if final_norm:
        xn = _rmsnorm(xn, fg_ref[...])
    y_ref[0] = xn


def _const_spec(shape):
    nd = len(shape)
    return pl.BlockSpec(shape, lambda *_: (0,) * nd, pipeline_mode=pl.Buffered(1))


def _layer_spec(arr, layer, cols=None, col_block=0):
    tail = list(arr.shape[1:])
    idx = [0] * len(tail)
    if cols is not None:
        tail[-1] = cols
        idx[-1] = col_block
    return pl.BlockSpec((None, *tail), lambda *_: (layer, *idx), pipeline_mode=pl.Buffered(1))


def _in_side_specs(sw, layer):
    return [_layer_spec(sw["g"], layer), _layer_spec(sw["win"], layer, D_IN_A, 0)]


def _c_side_specs(sw, layer):
    return [_layer_spec(sw["win"], layer, D_MODEL, blk) for blk in C_BLOCKS]


def _prompt_layer(x, kb, vb, sw, layer, final_norm):
    b, t, _ = x.shape
    nt = t // TM
    rest = ("conv_w", "pool_w", "pool_scale", "wbc", "wbp", "wba", "wout")
    weights = (sw["g"], sw["win"]) + (sw["win"],) * 4 + tuple(sw[n] for n in rest) + (sw["fg"],)
    w_specs = (_in_side_specs(sw, layer) + _c_side_specs(sw, layer)
               + [_layer_spec(sw[n], layer) for n in rest] + [_const_spec(sw["fg"].shape)])
    return pl.pallas_call(
        functools.partial(_prompt_layer_kernel, final_norm=final_norm),
        grid=(b, nt),
        in_specs=[
            pl.BlockSpec((1, TM, D_MODEL), lambda i, j: (i, j, 0)),
            pl.BlockSpec((None, N_MEM, D_MODEL), lambda i, j: (layer, i, 0)),
            pl.BlockSpec((None, N_MEM, D_MODEL), lambda i, j: (layer, i, 0)),
        ] + w_specs,
        out_specs=[
            pl.BlockSpec((1, TM, D_MODEL), lambda i, j: (i, j, 0)),
            pl.BlockSpec((1, CONV_WIDTH - 1, D_MODEL), lambda i, j: (i, 0, 0)),
            pl.BlockSpec((1, POOL_STATE, D_MODEL), lambda i, j: (i, 0, 0)),
        ],
        out_shape=[
            jax.ShapeDtypeStruct((b, t, D_MODEL), F32),
            jax.ShapeDtypeStruct((b, CONV_WIDTH - 1, D_MODEL), F32),
            jax.ShapeDtypeStruct((b, POOL_STATE, D_MODEL), F32),
        ],
        scratch_shapes=[
            pltpu.VMEM((CONV_PAD + TM, D_MODEL), F32),
            pltpu.VMEM((POOL_PAD + TM, D_MODEL), F32),
            pltpu.VMEM((POOL_PAD + TM, POOL_GROUP), F32),
            pltpu.VMEM((POOL_PAD + TM, POOL_GROUP), F32),
            pltpu.VMEM((TM, D_MODEL), BF16),
            pltpu.VMEM((TM, D_MODEL), BF16),
            pltpu.VMEM((TM, D_MODEL), BF16),
            pltpu.VMEM((TM, D_MODEL), BF16),
        ],
        compiler_params=_params(2),
        name="prompt_layer",
    )(x, kb, vb, *weights)


def _sample_in_kernel(x_ref, cst_ref, pst_ref, g_ref, wina_ref, convw_ref, poolw_ref, pscale_ref,
                      ac_ref, ap_ref, q_ref, u_ref, hp_ref, *, n_t, n_b):
    rows = n_t * n_b
    x = x_ref[...].reshape(rows, D_MODEL)
    h = _rmsnorm(x, g_ref[...]).astype(BF16)

    for c in range(N_CHUNKS):
        sl = _cols(0, c)
        hc = _dot(h, wina_ref[:, _cols(OFF_HC, c)])
        cc = _dot(h, wina_ref[:, _cols(OFF_CC, c)])
        u = cc * hc
        u_ref[:, :, sl] = u.reshape(n_t, n_b, CW)
        ext = [cst_ref[0, :, sl], cst_ref[1, :, sl]] + [u[i * n_b:(i + 1) * n_b] for i in range(n_t)]
        y = jnp.concatenate(
            [convw_ref[0:1, sl] * ext[i] + convw_ref[1:2, sl] * ext[i + 1] + convw_ref[2:3, sl] * ext[i + 2]
             for i in range(n_t)], axis=0)
        bc = _dot(h, wina_ref[:, _cols(OFF_BC, c)])
        gc = _dot(h, wina_ref[:, _cols(OFF_GC, c)])
        ac_ref[:, sl] = (bc * y * _silu(gc)).astype(BF16)

    for g, w in enumerate(POOL_WINDOWS):
        sl = _cols(0, g, POOL_GROUP)
        hp = _dot(h, wina_ref[:, _cols(OFF_HP, g, POOL_GROUP)])
        hp_ref[:, :, sl] = hp.reshape(n_t, n_b, POOL_GROUP)
        ext = [pst_ref[j, :, sl] for j in range(POOL_STATE)] + [hp[i * n_b:(i + 1) * n_b] for i in range(n_t)]
        mixed = []
        for i in range(n_t):
            cnt = float(min(PAST_LEN + i + 1, w))
            acc = ext[POOL_STATE + i]
            for j in range(1, w):
                acc = acc + ext[POOL_STATE + i - j]
            mixed.append(acc * (1.0 / cnt) - ext[POOL_STATE + i])
        mixed = jnp.concatenate(mixed, axis=0)
        pooled = _dot(mixed.astype(BF16), poolw_ref[g]) * pscale_ref[:, sl]
        gp = _dot(h, wina_ref[:, _cols(OFF_GP, g, POOL_GROUP)])
        ap_ref[:, sl] = (pooled * _silu(gp)).astype(BF16)

    for hh in range(HEADS):
        q_ref[:, _cols(0, hh, HEAD_DIM)] = _dot(h, wina_ref[:, _cols(OFF_Q, hh, HEAD_DIM)])


def _sample_in(x_t, cst_t, pst_t, sw, layer):
    n_t, n_b, _ = x_t.shape
    rows = n_t * n_b
    tb = (n_t, n_b, D_MODEL)
    rest = ("conv_w", "pool_w", "pool_scale")
    ins = (x_t, cst_t, pst_t, sw["g"], sw["win"]) + tuple(sw[n] for n in rest)
    return pl.pallas_call(
        functools.partial(_sample_in_kernel, n_t=n_t, n_b=n_b),
        grid=(1,),
        in_specs=([_const_spec(a.shape) for a in (x_t, cst_t, pst_t)] + _in_side_specs(sw, layer)
                  + [_layer_spec(sw[n], layer) for n in rest]),
        out_specs=[_const_spec((rows, D_MODEL))] * 3 + [_const_spec(tb)] * 2,
        out_shape=[
            jax.ShapeDtypeStruct((rows, D_MODEL), BF16),
            jax.ShapeDtypeStruct((rows, D_MODEL), BF16),
            jax.ShapeDtypeStruct((rows, D_MODEL), F32),
            jax.ShapeDtypeStruct(tb, F32),
            jax.ShapeDtypeStruct(tb, F32),
        ],
        compiler_params=pltpu.CompilerParams(vmem_limit_bytes=VMEM_LIMIT),
        name="sample_in",
    )(*ins)


def _sample_attn_kernel(q_ref, k_ref, v_ref, o_ref, *, n_t):
    scale = HEAD_DIM ** -0.5
    n_flat = N_MEM * HEADS
    col_head = lax.broadcasted_iota(jnp.int32, (HEADS * n_t, n_flat), 1) % HEADS
    row_head = lax.broadcasted_iota(jnp.int32, (HEADS * n_t, n_flat), 0) // n_t
    own = col_head == row_head
    for i in range(SB):
        qb = q_ref[i]
        qe = jnp.concatenate([qb[:, _cols(0, hh, HEAD_DIM)] for hh in range(HEADS)], axis=0).astype(BF16)
        kf = k_ref[0, i].reshape(n_flat, HEAD_DIM).astype(BF16)
        vf = v_ref[0, i].reshape(n_flat, HEAD_DIM).astype(BF16)
        s = jnp.where(own, _dot_nt(qe, kf) * scale, -jnp.inf)
        o = _dot(_softmax(s).astype(BF16), vf)
        o_ref[i] = jnp.concatenate([o[hh * n_t:(hh + 1) * n_t] for hh in range(HEADS)], axis=1)


def _sample_attn(q_b, cache_k, cache_v, layer):
    n_b, n_t, _ = q_b.shape
    kv_spec = pl.BlockSpec((1, SB, N_MEM, HEADS, HEAD_DIM), lambda s: (layer, s, 0, 0, 0))
    qo_spec = pl.BlockSpec((SB, n_t, D_MODEL), lambda s: (s, 0, 0))
    return pl.pallas_call(
        functools.partial(_sample_attn_kernel, n_t=n_t),
        grid=(n_b // SB,),
        in_specs=[qo_spec, kv_spec, kv_spec],
        out_specs=qo_spec,
        out_shape=jax.ShapeDtypeStruct(q_b.shape, F32),
        compiler_params=_params(1),
        name="sample_attn",
    )(q_b, cache_k, cache_v)


def _sample_out_kernel(x_ref, o_ref, ac_ref, ap_ref, g_ref, wga_ref, wmc_ref, wmp_ref, wma_ref,
                       wbc_ref, wbp_ref, wba_ref, wout_ref, fg_ref, y_ref, aa_s, m_s, *, final_norm):
    x = x_ref[...]
    h = _rmsnorm(x, g_ref[...]).astype(BF16)
    _att_gate(o_ref[...], h, aa_s, wga_ref)
    xn = _merge_and_out(x, h, ac_ref[...], ap_ref[...], aa_s[...], m_s,
                        wmc_ref, wmp_ref, wma_ref, wbc_ref, wbp_ref, wba_ref, wout_ref)
    if final_norm:
        xn = _rmsnorm(xn, fg_ref[...])
    y_ref[...] = xn


def _sample_out(x2d, o2d, ac, ap, sw, layer, final_norm):
    rows = x2d.shape[0]
    rest = ("wbc", "wbp", "wba", "wout")
    ins = (x2d, o2d, ac, ap, sw["g"]) + (sw["win"],) * 4 + tuple(sw[n] for n in rest) + (sw["fg"],)
    return pl.pallas_call(
        functools.partial(_sample_out_kernel, final_norm=final_norm),
        grid=(1,),
        in_specs=([_const_spec(a.shape) for a in (x2d, o2d, ac, ap)] + [_layer_spec(sw["g"], layer)]
                  + _c_side_specs(sw, layer) + [_layer_spec(sw[n], layer) for n in rest]
                  + [_const_spec(sw["fg"].shape)]),
        out_specs=_const_spec((rows, D_MODEL)),
        out_shape=jax.ShapeDtypeStruct((rows, D_MODEL), F32),
        scratch_shapes=[pltpu.VMEM((rows, D_MODEL), BF16), pltpu.VMEM((rows, D_MODEL), BF16)],
        compiler_params=pltpu.CompilerParams(vmem_limit_bytes=VMEM_LIMIT),
        name="sample_out",
    )(*ins)


def _to_batch_major(a2d, n_t, n_b):
    return jnp.transpose(a2d.reshape(n_t, n_b, D_MODEL), (1, 0, 2))


def kernel(x_prompt, x_sample, mem_prompt, cache_mem_k, cache_mem_v, state_conv, state_pool, norm_g, w_in, conv_w, pool_w, pool_scale, mem_norm_g, w_mem_kv, w_br_conv, w_br_pool, w_br_att, w_out, final_norm_g):
    depth = w_in.shape[0]
    b_p = x_prompt.shape[0]
    n_b, n_t, _ = x_sample.shape
    rows = n_b * n_t

    k_f, v_f, k_b, v_b = _mem_kv(mem_prompt.reshape(b_p * N_MEM, D_MODEL), mem_norm_g, w_mem_kv.astype(BF16))

    sw = dict(
        g=norm_g.reshape(depth, 1, D_MODEL), win=w_in.astype(BF16), conv_w=conv_w,
        pool_w=pool_w.astype(BF16), pool_scale=pool_scale.reshape(depth, 1, D_MODEL),
        wbc=w_br_conv.astype(BF16), wbp=w_br_pool.astype(BF16), wba=w_br_att.astype(BF16),
        wout=w_out.astype(BF16), fg=final_norm_g.reshape(1, D_MODEL))

    xp, xs = x_prompt, x_sample
    cv_p, pl_p, cv_s, pl_s = [], [], [], []
    for l in range(depth):
        final = l == depth - 1
        xp, c_new, p_new = _prompt_layer(xp, k_b, v_b, sw, l, final)
        cv_p.append(c_new)
        pl_p.append(p_new)

        ac, ap, q, u_t, hp_t = _sample_in(
            jnp.transpose(xs, (1, 0, 2)), jnp.transpose(state_conv[l], (1, 0, 2)),
            jnp.transpose(state_pool[l], (1, 0, 2)), sw, l)
        o = _sample_attn(_to_batch_major(q, n_t, n_b), cache_mem_k, cache_mem_v, l)
        xs = _sample_out(
            xs.reshape(rows, D_MODEL), o.reshape(rows, D_MODEL),
            _to_batch_major(ac, n_t, n_b).reshape(rows, D_MODEL),
            _to_batch_major(ap, n_t, n_b).reshape(rows, D_MODEL), sw, l, final).reshape(n_b, n_t, D_MODEL)
        cv_s.append(jnp.transpose(u_t[n_t - (CONV_WIDTH - 1):], (1, 0, 2)))
        pl_s.append(jnp.concatenate([state_pool[l][:, n_t:], jnp.transpose(hp_t, (1, 0, 2))], axis=1))

    return (xp, xs, k_f, v_f, jnp.stack(cv_p), jnp.stack(pl_p), jnp.stack(cv_s), jnp.stack(pl_s))
```

```python
import functools

import jax
import jax.numpy as jnp
from jax import lax
from jax.experimental import pallas as pl
from jax.experimental.pallas import tpu as pltpu

D_MODEL = 1024
N_MEM = 256
HEADS = 4
HEAD_DIM = 256
CONV_WIDTH = 3
POOL_WINDOWS = (2, 4, 8, 16)
POOL_GROUP = 256
POOL_STATE = 15
PAST_LEN = 16384
EPS = 1e-6
OFF_HC, OFF_BC, OFF_CC, OFF_GC = 0, 1024, 2048, 3072
OFF_HP, OFF_GP = 4096, 5120
OFF_Q = 6144
D_IN_A = 7168
GA_BLOCK = 7
MERGE_BLOCKS = (8, 9, 10)

CW = 256
N_CHUNKS = D_MODEL // CW
TM = 512
TM_OUT = 512
POOL_PAD = 32
CONV_PAD = 8
VMEM_BYTES_V7X = 64 * 1024 * 1024
VMEM_LIMIT = VMEM_BYTES_V7X - 2 * 1024 * 1024

F32 = jnp.float32
BF16 = jnp.bfloat16


def _dot(a, b):
    return jnp.dot(a, b, preferred_element_type=F32)


def _dot_nt(a, b):
    return lax.dot_general(a, b, (((1,), (1,)), ((), ())), preferred_element_type=F32)


def _sigmoid(x):
    return 0.5 * jnp.tanh(0.5 * x) + 0.5


def _silu(x):
    return x * _sigmoid(x)


def _rmsnorm(x, g):
    ms = jnp.mean(x * x, axis=-1, keepdims=True)
    return (x * lax.rsqrt(ms + EPS)) * g


def _softmax(s):
    e = jnp.exp(s - jnp.max(s, axis=-1, keepdims=True))
    return e * (1.0 / jnp.sum(e, axis=-1, keepdims=True))


def _cols(off, c, w=CW):
    return slice(off + c * w, off + (c + 1) * w)


def _params(n_axes):
    return pltpu.CompilerParams(
        dimension_semantics=("arbitrary",) * n_axes, vmem_limit_bytes=VMEM_LIMIT)


def _const_spec(shape):
    nd = len(shape)
    return pl.BlockSpec(shape, lambda *_: (0,) * nd, pipeline_mode=pl.Buffered(1))


def _layer_spec(arr, layer, cols=None, col_block=0):
    tail = list(arr.shape[1:])
    idx = [0] * len(tail)
    if cols is not None:
        tail[-1] = cols
        idx[-1] = col_block
    return pl.BlockSpec((None, *tail), lambda *_: (layer, *idx), pipeline_mode=pl.Buffered(1))


IN_SIDE_WEIGHTS = ("g", "win", "win", "conv_w", "pool_w", "pool_scale")


def _in_side_specs(sw, layer):
    return [_layer_spec(sw["g"], layer), _layer_spec(sw["win"], layer, D_IN_A, 0),
            _layer_spec(sw["win"], layer, D_MODEL, GA_BLOCK), _layer_spec(sw["conv_w"], layer),
            _layer_spec(sw["pool_w"], layer), _layer_spec(sw["pool_scale"], layer)]


def _kv_kernel(mem_ref, g_ref, w_ref, k_ref, v_ref, kb_ref, vb_ref):
    h = _rmsnorm(mem_ref[...], g_ref[0]).astype(BF16)
    k = _dot(h, w_ref[0, :, :D_MODEL])
    v = _dot(h, w_ref[0, :, D_MODEL:])
    k_ref[0] = k.reshape(k_ref.shape[1:])
    v_ref[0] = v.reshape(v_ref.shape[1:])
    kb_ref[0] = k.astype(BF16)
    vb_ref[0] = v.astype(BF16)


def _mem_kv(mem2d, mem_norm_g, w_kv_bf16):
    depth = w_kv_bf16.shape[0]
    rows = mem2d.shape[0]
    bt = 2
    rt = bt * N_MEM
    out_f = jax.ShapeDtypeStruct((depth, rows // N_MEM, N_MEM, HEADS, HEAD_DIM), F32)
    out_b = jax.ShapeDtypeStruct((depth, rows, D_MODEL), BF16)
    blk = pl.BlockSpec((1, rt, D_MODEL), lambda l, r: (l, r, 0))
    blk5 = pl.BlockSpec((1, bt, N_MEM, HEADS, HEAD_DIM), lambda l, r: (l, r, 0, 0, 0))
    return pl.pallas_call(
        _kv_kernel,
        grid=(depth, rows // rt),
        in_specs=[
            pl.BlockSpec((rt, D_MODEL), lambda l, r: (r, 0)),
            pl.BlockSpec((1, 1, D_MODEL), lambda l, r: (l, 0, 0)),
            pl.BlockSpec((1, D_MODEL, 2 * D_MODEL), lambda l, r: (l, 0, 0)),
        ],
        out_specs=[blk5, blk5, blk, blk],
        out_shape=[out_f, out_f, out_b, out_b],
        compiler_params=_params(2),
        name="mem_kv",
    )(mem2d, mem_norm_g.reshape(depth, 1, D_MODEL), w_kv_bf16)


def _sample_attention(qb, k, v, own):
    n_t = qb.shape[0]
    n_flat = N_MEM * HEADS
    qe = jnp.concatenate([qb[:, _cols(0, hh, HEAD_DIM)] for hh in range(HEADS)], axis=0).astype(BF16)
    kf = k.reshape(n_flat, HEAD_DIM).astype(BF16)
    vf = v.reshape(n_flat, HEAD_DIM).astype(BF16)
    s = jnp.where(own, _dot_nt(qe, kf) * HEAD_DIM ** -0.5, -jnp.inf)
    o = _dot(_softmax(s).astype(BF16), vf)
    return jnp.concatenate([o[hh * n_t:(hh + 1) * n_t] for hh in range(HEADS)], axis=1)


def _own_head_mask(n_t):
    n_flat = N_MEM * HEADS
    col_head = lax.broadcasted_iota(jnp.int32, (HEADS * n_t, n_flat), 1) % HEADS
    row_head = lax.broadcasted_iota(jnp.int32, (HEADS * n_t, n_flat), 0) // n_t
    return col_head == row_head


def _prompt_in_kernel(x_ref, k_ref, v_ref, sq_ref, ssg_ref, sk_ref, sv_ref,
                      g_ref, wina_ref, wga_ref, convw_ref, poolw_ref, pscale_ref,
                      ac_ref, ap_ref, aa_ref, saa_ref, cst_ref, pst_ref,
                      u_s, p_s, sa_s, sb_s, *, sb, n_t):
    t = pl.program_id(1)

    @pl.when(t == 0)
    def _():
        u_s[0:CONV_PAD, :] = jnp.zeros((CONV_PAD, D_MODEL), F32)
        p_s[0:POOL_PAD, :] = jnp.zeros((POOL_PAD, D_MODEL), F32)
        sa_s[0:16, :] = jnp.zeros((16, POOL_GROUP), F32)
        sb_s[0:16, :] = jnp.zeros((16, POOL_GROUP), F32)

    x = x_ref[0]
    h = _rmsnorm(x, g_ref[...]).astype(BF16)

    for c in range(N_CHUNKS):
        sl = _cols(0, c)
        hc = _dot(h, wina_ref[:, _cols(OFF_HC, c)])
        cc = _dot(h, wina_ref[:, _cols(OFF_CC, c)])
        u = cc * hc
        u_s[CONV_PAD:CONV_PAD + TM, sl] = u
        u1 = u_s[CONV_PAD - 1:CONV_PAD - 1 + TM, sl]
        u2 = u_s[CONV_PAD - 2:CONV_PAD - 2 + TM, sl]
        y = convw_ref[0:1, sl] * u2 + convw_ref[1:2, sl] * u1 + convw_ref[2:3, sl] * u
        bc = _dot(h, wina_ref[:, _cols(OFF_BC, c)])
        gc = _dot(h, wina_ref[:, _cols(OFF_GC, c)])
        ac_ref[0, :, sl] = (bc * y * _silu(gc)).astype(BF16)
    new_conv = u_s[CONV_PAD + TM - 2:CONV_PAD + TM, :]
    cst_ref[0] = new_conv
    u_s[CONV_PAD - 2:CONV_PAD, :] = new_conv

    own = _own_head_mask(n_t)
    for i in range(sb):
        o = _sample_attention(sq_ref[i], sk_ref[0, i], sv_ref[0, i], own)
        saa_ref[i] = o * ssg_ref[i]

    pos1 = (t * TM + 1 + lax.broadcasted_iota(jnp.int32, (TM, 1), 0)).astype(F32)
    n = TM + 16
    for g, w in enumerate(POOL_WINDOWS):
        sl = _cols(0, g, POOL_GROUP)
        hp = _dot(h, wina_ref[:, _cols(OFF_HP, g, POOL_GROUP)])
        p_s[POOL_PAD:POOL_PAD + TM, sl] = hp
        cur = p_s[16:16 + n, sl] + p_s[15:15 + n, sl]
        shift, src, dst = 2, sa_s, sb_s
        while shift < w:
            src[16:16 + n, :] = cur
            cur = src[16:16 + n, :] + src[16 - shift:16 - shift + n, :]
            shift *= 2
            src, dst = dst, src
        inv_cnt = 1.0 / jnp.minimum(pos1, float(w))
        mixed = cur[16:, :] * inv_cnt - hp
        pooled = _dot(mixed.astype(BF16), poolw_ref[g]) * pscale_ref[:, sl]
        gp = _dot(h, wina_ref[:, _cols(OFF_GP, g, POOL_GROUP)])
        ap_ref[0, :, sl] = (pooled * _silu(gp)).astype(BF16)
    pst_ref[0] = p_s[POOL_PAD + TM - POOL_STATE:POOL_PAD + TM, :]
    p_s[16:POOL_PAD, :] = p_s[TM + 16:TM + POOL_PAD, :]

    scale = HEAD_DIM ** -0.5
    for hh in range(HEADS):
        sl = _cols(0, hh, HEAD_DIM)
        q = _dot(h, wina_ref[:, _cols(OFF_Q, hh, HEAD_DIM)]).astype(BF16)
        p = _softmax(_dot_nt(q, k_ref[:, sl]) * scale)
        o = _dot(p.astype(BF16), v_ref[:, sl])
        ga = _dot(h, wga_ref[:, sl])
        aa_ref[0, :, sl] = (o * _silu(ga)).astype(BF16)


def _prompt_in(x, kb, vb, sq, ssg, cache_k, cache_v, sw, layer):
    b, t, _ = x.shape
    nt = t // TM
    n_b, n_t, _ = sq.shape
    sb = n_b // (b * nt)
    assert sb * b * nt == n_b
    tile = pl.BlockSpec((1, TM, D_MODEL), lambda i, j: (i, j, 0))
    smp = pl.BlockSpec((sb, n_t, D_MODEL), lambda i, j: (i * nt + j, 0, 0))
    skv = pl.BlockSpec((1, sb, N_MEM, HEADS, HEAD_DIM), lambda i, j: (layer, i * nt + j, 0, 0, 0))
    act = jax.ShapeDtypeStruct((b, t, D_MODEL), BF16)
    return pl.pallas_call(
        functools.partial(_prompt_in_kernel, sb=sb, n_t=n_t),
        grid=(b, nt),
        in_specs=[
            tile,
            pl.BlockSpec((None, N_MEM, D_MODEL), lambda i, j: (layer, i, 0)),
            pl.BlockSpec((None, N_MEM, D_MODEL), lambda i, j: (layer, i, 0)),
            smp, smp, skv, skv,
        ] + _in_side_specs(sw, layer),
        out_specs=[
            tile, tile, tile, smp,
            pl.BlockSpec((1, CONV_WIDTH - 1, D_MODEL), lambda i, j: (i, 0, 0)),
            pl.BlockSpec((1, POOL_STATE, D_MODEL), lambda i, j: (i, 0, 0)),
        ],
        out_shape=[
            act, act, act,
            jax.ShapeDtypeStruct(sq.shape, F32),
            jax.ShapeDtypeStruct((b, CONV_WIDTH - 1, D_MODEL), F32),
            jax.ShapeDtypeStruct((b, POOL_STATE, D_MODEL), F32),
        ],
        scratch_shapes=[
            pltpu.VMEM((CONV_PAD + TM, D_MODEL), F32),
            pltpu.VMEM((POOL_PAD + TM, D_MODEL), F32),
            pltpu.VMEM((POOL_PAD + TM, POOL_GROUP), F32),
            pltpu.VMEM((POOL_PAD + TM, POOL_GROUP), F32),
        ],
        compiler_params=_params(2),
        name="prompt_in",
    )(x, kb, vb, sq, ssg, cache_k, cache_v, *(sw[n] for n in IN_SIDE_WEIGHTS))


def _sample_in_kernel(x_ref, cst_ref, pst_ref, g_ref, wina_ref, wga_ref, convw_ref, poolw_ref, pscale_ref,
                      ac_ref, ap_ref, q_ref, sg_ref, u_ref, hp_ref, *, n_t, n_b):
    rows = n_t * n_b
    x = x_ref[...].reshape(rows, D_MODEL)
    h = _rmsnorm(x, g_ref[...]).astype(BF16)

    for c in range(N_CHUNKS):
        sl = _cols(0, c)
        hc = _dot(h, wina_ref[:, _cols(OFF_HC, c)])
        cc = _dot(h, wina_ref[:, _cols(OFF_CC, c)])
        u = cc * hc
        u_ref[:, :, sl] = u.reshape(n_t, n_b, CW)
        ext = [cst_ref[0, :, sl], cst_ref[1, :, sl]] + [u[i * n_b:(i + 1) * n_b] for i in range(n_t)]
        y = jnp.concatenate(
            [convw_ref[0:1, sl] * ext[i] + convw_ref[1:2, sl] * ext[i + 1] + convw_ref[2:3, sl] * ext[i + 2]
             for i in range(n_t)], axis=0)
        bc = _dot(h, wina_ref[:, _cols(OFF_BC, c)])
        gc = _dot(h, wina_ref[:, _cols(OFF_GC, c)])
        ac_ref[:, sl] = (bc * y * _silu(gc)).astype(BF16)

    for g, w in enumerate(POOL_WINDOWS):
        sl = _cols(0, g, POOL_GROUP)
        hp = _dot(h, wina_ref[:, _cols(OFF_HP, g, POOL_GROUP)])
        hp_ref[:, :, sl] = hp.reshape(n_t, n_b, POOL_GROUP)
        ext = [pst_ref[j, :, sl] for j in range(POOL_STATE)] + [hp[i * n_b:(i + 1) * n_b] for i in range(n_t)]
        mixed = []
        for i in range(n_t):
            cnt = float(min(PAST_LEN + i + 1, w))
            acc = ext[POOL_STATE + i]
            for j in range(1, w):
                acc = acc + ext[POOL_STATE + i - j]
            mixed.append(acc * (1.0 / cnt) - ext[POOL_STATE + i])
        mixed = jnp.concatenate(mixed, axis=0)
        pooled = _dot(mixed.astype(BF16), poolw_ref[g]) * pscale_ref[:, sl]
        gp = _dot(h, wina_ref[:, _cols(OFF_GP, g, POOL_GROUP)])
        ap_ref[:, sl] = (pooled * _silu(gp)).astype(BF16)

    for hh in range(HEADS):
        sl = _cols(0, hh, HEAD_DIM)
        q_ref[:, sl] = _dot(h, wina_ref[:, _cols(OFF_Q, hh, HEAD_DIM)])
        sg_ref[:, sl] = _silu(_dot(h, wga_ref[:, sl]))


def _sample_in(x_t, cst_t, pst_t, sw, layer):
    n_t, n_b, _ = x_t.shape
    rows = n_t * n_b
    tb = (n_t, n_b, D_MODEL)
    return pl.pallas_call(
        functools.partial(_sample_in_kernel, n_t=n_t, n_b=n_b),
        grid=(1,),
        in_specs=[_const_spec(a.shape) for a in (x_t, cst_t, pst_t)] + _in_side_specs(sw, layer),
        out_specs=[_const_spec((rows, D_MODEL))] * 4 + [_const_spec(tb)] * 2,
        out_shape=[
            jax.ShapeDtypeStruct((rows, D_MODEL), BF16),
            jax.ShapeDtypeStruct((rows, D_MODEL), BF16),
            jax.ShapeDtypeStruct((rows, D_MODEL), F32),
            jax.ShapeDtypeStruct((rows, D_MODEL), F32),
            jax.ShapeDtypeStruct(tb, F32),
            jax.ShapeDtypeStruct(tb, F32),
        ],
        compiler_params=_params(1),
        name="sample_in",
    )(x_t, cst_t, pst_t, *(sw[n] for n in IN_SIDE_WEIGHTS))


def _out_side_kernel(x_ref, ac_ref, ap_ref, aa_ref, g_ref, wmc_ref, wmp_ref, wma_ref,
                     wbc_ref, wbp_ref, wba_ref, wout_ref, fg_ref, y_ref, m_s, *, final_norm):
    x = x_ref[...]
    h = _rmsnorm(x, g_ref[...]).astype(BF16)
    ac, ap, aa = ac_ref[...], ap_ref[...], aa_ref[...].astype(BF16)
    for c in range(N_CHUNKS):
        sl = _cols(0, c)
        conv_br = _dot(ac, wbc_ref[:, sl])
        pool_br = _dot(ap, wbp_ref[:, sl])
        att_br = _dot(aa, wba_ref[:, sl])
        mc = _dot(h, wmc_ref[:, sl])
        mp = _dot(h, wmp_ref[:, sl])
        ma = _dot(h, wma_ref[:, sl])
        merged = _sigmoid(mc) * conv_br + _sigmoid(mp) * pool_br + _sigmoid(ma) * att_br
        m_s[:, sl] = merged.astype(BF16)
    xn = x + _dot(m_s[...], wout_ref[...])
    if final_norm:
        xn = _rmsnorm(xn, fg_ref[...])
    y_ref[...] = xn


def _out_side(x2d, ac, ap, aa, sw, layer, final_norm, name):
    rows = x2d.shape[0]
    tm = min(TM_OUT, rows)
    tile = pl.BlockSpec((tm, D_MODEL), lambda r: (r, 0))
    rest = ("wbc", "wbp", "wba", "wout")
    weights = (sw["g"],) + (sw["win"],) * 3 + tuple(sw[n] for n in rest) + (sw["fg"],)
    w_specs = ([_layer_spec(sw["g"], layer)]
               + [_layer_spec(sw["win"], layer, D_MODEL, blk) for blk in MERGE_BLOCKS]
               + [_layer_spec(sw[n], layer) for n in rest] + [_const_spec(sw["fg"].shape)])
    return pl.pallas_call(
        functools.partial(_out_side_kernel, final_norm=final_norm),
        grid=(rows // tm,),
        in_specs=[tile] * 4 + w_specs,
        out_specs=tile,
        out_shape=jax.ShapeDtypeStruct((rows, D_MODEL), F32),
        scratch_shapes=[pltpu.VMEM((tm, D_MODEL), BF16)],
        compiler_params=_params(1),
        name=name,
    )(x2d, ac, ap, aa, *weights)


def _to_batch_major(a2d, n_t, n_b):
    return jnp.transpose(a2d.reshape(n_t, n_b, D_MODEL), (1, 0, 2))


def kernel(x_prompt, x_sample, mem_prompt, cache_mem_k, cache_mem_v, state_conv, state_pool, norm_g, w_in, conv_w, pool_w, pool_scale, mem_norm_g, w_mem_kv, w_br_conv, w_br_pool, w_br_att, w_out, final_norm_g):
    depth = w_in.shape[0]
    b_p, t_p, _ = x_prompt.shape
    n_b, n_t, _ = x_sample.shape
    rows_p, rows_s = b_p * t_p, n_b * n_t

    k_f, v_f, k_b, v_b = _mem_kv(mem_prompt.reshape(b_p * N_MEM, D_MODEL), mem_norm_g, w_mem_kv.astype(BF16))

    sw = dict(
        g=norm_g.reshape(depth, 1, D_MODEL), win=w_in.astype(BF16), conv_w=conv_w,
        pool_w=pool_w.astype(BF16), pool_scale=pool_scale.reshape(depth, 1, D_MODEL),
        wbc=w_br_conv.astype(BF16), wbp=w_br_pool.astype(BF16), wba=w_br_att.astype(BF16),
        wout=w_out.astype(BF16), fg=final_norm_g.reshape(1, D_MODEL))

    xp, xs = x_prompt, x_sample
    cv_p, pl_p, cv_s, pl_s = [], [], [], []
    for l in range(depth):
        final = l == depth - 1
        ac_s, ap_s, q, sg, u_t, hp_t = _sample_in(
            jnp.transpose(xs, (1, 0, 2)), jnp.transpose(state_conv[l], (1, 0, 2)),
            jnp.transpose(state_pool[l], (1, 0, 2)), sw, l)
        cv_s.append(jnp.transpose(u_t[n_t - (CONV_WIDTH - 1):], (1, 0, 2)))
        pl_s.append(jnp.concatenate([state_pool[l][:, n_t:], jnp.transpose(hp_t, (1, 0, 2))], axis=1))

        ac_p, ap_p, aa_p, aa_s, c_new, p_new = _prompt_in(
            xp, k_b, v_b, _to_batch_major(q, n_t, n_b), _to_batch_major(sg, n_t, n_b),
            cache_mem_k, cache_mem_v, sw, l)
        cv_p.append(c_new)
        pl_p.append(p_new)

        flat = lambda a: a.reshape(rows_p, D_MODEL)
        xp = _out_side(flat(xp), flat(ac_p), flat(ap_p), flat(aa_p), sw, l, final,
                       "prompt_out").reshape(b_p, t_p, D_MODEL)
        xs = _out_side(
            xs.reshape(rows_s, D_MODEL),
            _to_batch_major(ac_s, n_t, n_b).reshape(rows_s, D_MODEL),
            _to_batch_major(ap_s, n_t, n_b).reshape(rows_s, D_MODEL),
            aa_s.reshape(rows_s, D_MODEL), sw, l, final, "sample_out").reshape(n_b, n_t, D_MODEL)

    return (xp, xs, k_f, v_f, jnp.stack(cv_p), jnp.stack(pl_p), jnp.stack(cv_s), jnp.stack(pl_s))
```

```python
import functools

import jax
import jax.numpy as jnp
from jax import lax
from jax.experimental import pallas as pl
from jax.experimental.pallas import tpu as pltpu

D_MODEL = 1024
N_MEM = 256
HEADS = 4
HEAD_DIM = 256
CONV_WIDTH = 3
POOL_WINDOWS = (2, 4, 8, 16)
POOL_GROUP = 256
POOL_STATE = 15
PAST_LEN = 16384
EPS = 1e-6
OFF_HC, OFF_BC, OFF_CC, OFF_GC = 0, 1024, 2048, 3072
OFF_HP, OFF_GP = 4096, 5120
OFF_Q = 6144
D_IN_A = 7168
GA_BLOCK = 7
MERGE_BLOCKS = (8, 9, 10)

CW = 256
N_CHUNKS = D_MODEL // CW
TM = 512
TM_OUT = 512
POOL_PAD = 32
CONV_PAD = 8
VMEM_BYTES_V7X = 64 * 1024 * 1024
VMEM_LIMIT = VMEM_BYTES_V7X - 2 * 1024 * 1024

F32 = jnp.float32
BF16 = jnp.bfloat16


def _dot(a, b):
    return jnp.dot(a, b, preferred_element_type=F32)


def _dot_nt(a, b):
    return lax.dot_general(a, b, (((1,), (1,)), ((), ())), preferred_element_type=F32)


def _sigmoid(x):
    return 0.5 * jnp.tanh(0.5 * x) + 0.5


def _silu(x):
    return x * _sigmoid(x)


def _rmsnorm(x, g):
    ms = jnp.mean(x * x, axis=-1, keepdims=True)
    return (x * lax.rsqrt(ms + EPS)) * g


def _softmax(s):
    e = jnp.exp(s - jnp.max(s, axis=-1, keepdims=True))
    return e * (1.0 / jnp.sum(e, axis=-1, keepdims=True))


def _cols(off, c, w=CW):
    return slice(off + c * w, off + (c + 1) * w)


def _params(n_axes):
    return pltpu.CompilerParams(
        dimension_semantics=("arbitrary",) * n_axes, vmem_limit_bytes=VMEM_LIMIT)


def _const_spec(shape):
    nd = len(shape)
    return pl.BlockSpec(shape, lambda *_: (0,) * nd, pipeline_mode=pl.Buffered(1))


def _layer_spec(arr, layer, cols=None, col_block=0):
    tail = list(arr.shape[1:])
    idx = [0] * len(tail)
    if cols is not None:
        tail[-1] = cols
        idx[-1] = col_block
    return pl.BlockSpec((None, *tail), lambda *_: (layer, *idx), pipeline_mode=pl.Buffered(1))


IN_SIDE_WEIGHTS = ("g", "win", "win", "conv_w", "pool_w", "pool_scale")


def _in_side_specs(sw, layer):
    return [_layer_spec(sw["g"], layer), _layer_spec(sw["win"], layer, D_IN_A, 0),
            _layer_spec(sw["win"], layer, D_MODEL, GA_BLOCK), _layer_spec(sw["conv_w"], layer),
            _layer_spec(sw["pool_w"], layer), _layer_spec(sw["pool_scale"], layer)]


def _kv_kernel(mem_ref, g_ref, w_ref, k_ref, v_ref, kb_ref, vb_ref):
    h = _rmsnorm(mem_ref[...], g_ref[0]).astype(BF16)
    k = _dot(h, w_ref[0, :, :D_MODEL])
    v = _dot(h, w_ref[0, :, D_MODEL:])
    k_ref[0] = k.reshape(k_ref.shape[1:])
    v_ref[0] = v.reshape(v_ref.shape[1:])
    kb_ref[0] = k.astype(BF16)
    vb_ref[0] = v.astype(BF16)


def _mem_kv(mem2d, mem_norm_g, w_kv_bf16):
    depth = w_kv_bf16.shape[0]
    rows = mem2d.shape[0]
    bt = 2
    rt = bt * N_MEM
    out_f = jax.ShapeDtypeStruct((depth, rows // N_MEM, N_MEM, HEADS, HEAD_DIM), F32)
    out_b = jax.ShapeDtypeStruct((depth, rows, D_MODEL), BF16)
    blk = pl.BlockSpec((1, rt, D_MODEL), lambda l, r: (l, r, 0))
    blk5 = pl.BlockSpec((1, bt, N_MEM, HEADS, HEAD_DIM), lambda l, r: (l, r, 0, 0, 0))
    return pl.pallas_call(
        _kv_kernel,
        grid=(depth, rows // rt),
        in_specs=[
            pl.BlockSpec((rt, D_MODEL), lambda l, r: (r, 0)),
            pl.BlockSpec((1, 1, D_MODEL), lambda l, r: (l, 0, 0)),
            pl.BlockSpec((1, D_MODEL, 2 * D_MODEL), lambda l, r: (l, 0, 0)),
        ],
        out_specs=[blk5, blk5, blk, blk],
        out_shape=[out_f, out_f, out_b, out_b],
        compiler_params=_params(2),
        name="mem_kv",
    )(mem2d, mem_norm_g.reshape(depth, 1, D_MODEL), w_kv_bf16)


KV_TILE = CW // HEADS
N_KV_TILES = N_MEM // KV_TILE


def _sample_attention_tiles(sq_ref, ssg_ref, sk_ref, sv_ref, saa_ref, sb, n_t):
    n_flat = N_MEM * HEADS
    col_head = lax.broadcasted_iota(jnp.int32, (HEADS * n_t, n_flat), 1) % HEADS
    row_head = lax.broadcasted_iota(jnp.int32, (HEADS * n_t, n_flat), 0) // n_t
    own = col_head == row_head
    for i in range(sb):
        qb = sq_ref[i]
        qe = jnp.concatenate([qb[:, _cols(0, hh, HEAD_DIM)] for hh in range(HEADS)], axis=0).astype(BF16)
        s = []
        for j in range(N_KV_TILES):
            kf = sk_ref[0, i, j * KV_TILE:(j + 1) * KV_TILE].reshape(CW, HEAD_DIM).astype(BF16)
            s.append(_dot_nt(qe, kf))
            yield
        s = jnp.where(own, jnp.concatenate(s, axis=1) * HEAD_DIM ** -0.5, -jnp.inf)
        p = _softmax(s).astype(BF16)
        o = None
        for j in range(N_KV_TILES):
            vf = sv_ref[0, i, j * KV_TILE:(j + 1) * KV_TILE].reshape(CW, HEAD_DIM).astype(BF16)
            part = _dot(p[:, _cols(0, j)], vf)
            o = part if o is None else o + part
            yield
        o = jnp.concatenate([o[hh * n_t:(hh + 1) * n_t] for hh in range(HEADS)], axis=1)
        saa_ref[i] = o * ssg_ref[i]


def _prompt_in_kernel(x_ref, k_ref, v_ref, sq_ref, ssg_ref, sk_ref, sv_ref,
                      g_ref, wina_ref, wga_ref, convw_ref, poolw_ref, pscale_ref,
                      ac_ref, ap_ref, aa_ref, saa_ref, cst_ref, pst_ref,
                      u_s, p_s, sa_s, sb_s, *, sb, n_t):
    t = pl.program_id(1)

    @pl.when(t == 0)
    def _():
        u_s[0:CONV_PAD, :] = jnp.zeros((CONV_PAD, D_MODEL), F32)
        p_s[0:POOL_PAD, :] = jnp.zeros((POOL_PAD, D_MODEL), F32)
        sa_s[0:16, :] = jnp.zeros((16, POOL_GROUP), F32)
        sb_s[0:16, :] = jnp.zeros((16, POOL_GROUP), F32)

    x = x_ref[0]
    h = _rmsnorm(x, g_ref[...]).astype(BF16)
    sample_attn = _sample_attention_tiles(sq_ref, ssg_ref, sk_ref, sv_ref, saa_ref, sb, n_t)

    def hdot(w_ref, off, c):
        r = _dot(h, w_ref[:, _cols(off, c)])
        next(sample_attn, None)
        return r

    for c in range(N_CHUNKS):
        sl = _cols(0, c)
        hc = hdot(wina_ref, OFF_HC, c)
        cc = hdot(wina_ref, OFF_CC, c)
        u = cc * hc
        u_s[CONV_PAD:CONV_PAD + TM, sl] = u
        u1 = u_s[CONV_PAD - 1:CONV_PAD - 1 + TM, sl]
        u2 = u_s[CONV_PAD - 2:CONV_PAD - 2 + TM, sl]
        y = convw_ref[0:1, sl] * u2 + convw_ref[1:2, sl] * u1 + convw_ref[2:3, sl] * u
        bc = hdot(wina_ref, OFF_BC, c)
        gc = hdot(wina_ref, OFF_GC, c)
        ac_ref[0, :, sl] = (bc * y * _silu(gc)).astype(BF16)
    new_conv = u_s[CONV_PAD + TM - 2:CONV_PAD + TM, :]
    cst_ref[0] = new_conv
    u_s[CONV_PAD - 2:CONV_PAD, :] = new_conv

    pos1 = (t * TM + 1 + lax.broadcasted_iota(jnp.int32, (TM, 1), 0)).astype(F32)
    n = TM + 16
    hp_next = hdot(wina_ref, OFF_HP, 0)
    for g, w in enumerate(POOL_WINDOWS):
        sl = _cols(0, g, POOL_GROUP)
        hp = hp_next
        p_s[POOL_PAD:POOL_PAD + TM, sl] = hp
        cur = p_s[16:16 + n, sl] + p_s[15:15 + n, sl]
        shift, src, dst = 2, sa_s, sb_s
        while shift < w:
            src[16:16 + n, :] = cur
            cur = src[16:16 + n, :] + src[16 - shift:16 - shift + n, :]
            shift *= 2
            src, dst = dst, src
        inv_cnt = 1.0 / jnp.minimum(pos1, float(w))
        mixed = (cur[16:, :] * inv_cnt - hp).astype(BF16)
        gp = hdot(wina_ref, OFF_GP, g)
        if g + 1 < len(POOL_WINDOWS):
            hp_next = hdot(wina_ref, OFF_HP, g + 1)
        pooled = _dot(mixed, poolw_ref[g]) * pscale_ref[:, sl]
        ap_ref[0, :, sl] = (pooled * _silu(gp)).astype(BF16)
    pst_ref[0] = p_s[POOL_PAD + TM - POOL_STATE:POOL_PAD + TM, :]
    p_s[16:POOL_PAD, :] = p_s[TM + 16:TM + POOL_PAD, :]

    def probs(hh):
        q = hdot(wina_ref, OFF_Q, hh).astype(BF16)
        return _softmax(_dot_nt(q, k_ref[:, _cols(0, hh)]) * HEAD_DIM ** -0.5).astype(BF16)

    p_next = probs(0)
    for hh in range(HEADS):
        sl = _cols(0, hh, HEAD_DIM)
        p = p_next
        ga = hdot(wga_ref, 0, hh)
        if hh + 1 < HEADS:
            p_next = probs(hh + 1)
        o = _dot(p, v_ref[:, sl])
        aa_ref[0, :, sl] = (o * _silu(ga)).astype(BF16)

    for _ in sample_attn:
        pass


def _prompt_in(x, kb, vb, sq, ssg, cache_k, cache_v, sw, layer):
    b, t, _ = x.shape
    nt = t // TM
    n_b, n_t, _ = sq.shape
    sb = n_b // (b * nt)
    assert sb * b * nt == n_b
    tile = pl.BlockSpec((1, TM, D_MODEL), lambda i, j: (i, j, 0))
    smp = pl.BlockSpec((sb, n_t, D_MODEL), lambda i, j: (i * nt + j, 0, 0))
    skv = pl.BlockSpec((1, sb, N_MEM, HEADS, HEAD_DIM), lambda i, j: (layer, i * nt + j, 0, 0, 0))
    act = jax.ShapeDtypeStruct((b, t, D_MODEL), BF16)
    return pl.pallas_call(
        functools.partial(_prompt_in_kernel, sb=sb, n_t=n_t),
        grid=(b, nt),
        in_specs=[
            tile,
            pl.BlockSpec((None, N_MEM, D_MODEL), lambda i, j: (layer, i, 0)),
            pl.BlockSpec((None, N_MEM, D_MODEL), lambda i, j: (layer, i, 0)),
            smp, smp, skv, skv,
        ] + _in_side_specs(sw, layer),
        out_specs=[
            tile, tile, tile, smp,
            pl.BlockSpec((1, CONV_WIDTH - 1, D_MODEL), lambda i, j: (i, 0, 0)),
            pl.BlockSpec((1, POOL_STATE, D_MODEL), lambda i, j: (i, 0, 0)),
        ],
        out_shape=[
            act, act, act,
            jax.ShapeDtypeStruct(sq.shape, F32),
            jax.ShapeDtypeStruct((b, CONV_WIDTH - 1, D_MODEL), F32),
            jax.ShapeDtypeStruct((b, POOL_STATE, D_MODEL), F32),
        ],
        scratch_shapes=[
            pltpu.VMEM((CONV_PAD + TM, D_MODEL), F32),
            pltpu.VMEM((POOL_PAD + TM, D_MODEL), F32),
            pltpu.VMEM((POOL_PAD + TM, POOL_GROUP), F32),
            pltpu.VMEM((POOL_PAD + TM, POOL_GROUP), F32),
        ],
        compiler_params=_params(2),
        name="prompt_in",
    )(x, kb, vb, sq, ssg, cache_k, cache_v, *(sw[n] for n in IN_SIDE_WEIGHTS))


def _sample_in_kernel(x_ref, cst_ref, pst_ref, g_ref, wina_ref, wga_ref, convw_ref, poolw_ref, pscale_ref,
                      ac_ref, ap_ref, q_ref, sg_ref, u_ref, hp_ref, *, n_t, n_b):
    rows = n_t * n_b
    x = x_ref[...].reshape(rows, D_MODEL)
    h = _rmsnorm(x, g_ref[...]).astype(BF16)

    for c in range(N_CHUNKS):
        sl = _cols(0, c)
        hc = _dot(h, wina_ref[:, _cols(OFF_HC, c)])
        cc = _dot(h, wina_ref[:, _cols(OFF_CC, c)])
        u = cc * hc
        u_ref[:, :, sl] = u.reshape(n_t, n_b, CW)
        ext = [cst_ref[0, :, sl], cst_ref[1, :, sl]] + [u[i * n_b:(i + 1) * n_b] for i in range(n_t)]
        y = jnp.concatenate(
            [convw_ref[0:1, sl] * ext[i] + convw_ref[1:2, sl] * ext[i + 1] + convw_ref[2:3, sl] * ext[i + 2]
             for i in range(n_t)], axis=0)
        bc = _dot(h, wina_ref[:, _cols(OFF_BC, c)])
        gc = _dot(h, wina_ref[:, _cols(OFF_GC, c)])
        ac_ref[:, sl] = (bc * y * _silu(gc)).astype(BF16)

    for g, w in enumerate(POOL_WINDOWS):
        sl = _cols(0, g, POOL_GROUP)
        hp = _dot(h, wina_ref[:, _cols(OFF_HP, g, POOL_GROUP)])
        hp_ref[:, :, sl] = hp.reshape(n_t, n_b, POOL_GROUP)
        ext = [pst_ref[j, :, sl] for j in range(POOL_STATE)] + [hp[i * n_b:(i + 1) * n_b] for i in range(n_t)]
        mixed = []
        for i in range(n_t):
            cnt = float(min(PAST_LEN + i + 1, w))
            acc = ext[POOL_STATE + i]
            for j in range(1, w):
                acc = acc + ext[POOL_STATE + i - j]
            mixed.append(acc * (1.0 / cnt) - ext[POOL_STATE + i])
        mixed = jnp.concatenate(mixed, axis=0)
        pooled = _dot(mixed.astype(BF16), poolw_ref[g]) * pscale_ref[:, sl]
        gp = _dot(h, wina_ref[:, _cols(OFF_GP, g, POOL_GROUP)])
        ap_ref[:, sl] = (pooled * _silu(gp)).astype(BF16)

    for hh in range(HEADS):
        sl = _cols(0, hh, HEAD_DIM)
        q_ref[:, sl] = _dot(h, wina_ref[:, _cols(OFF_Q, hh, HEAD_DIM)])
        sg_ref[:, sl] = _silu(_dot(h, wga_ref[:, sl]))


def _sample_in(x_t, cst_t, pst_t, sw, layer):
    n_t, n_b, _ = x_t.shape
    rows = n_t * n_b
    tb = (n_t, n_b, D_MODEL)
    return pl.pallas_call(
        functools.partial(_sample_in_kernel, n_t=n_t, n_b=n_b),
        grid=(1,),
        in_specs=[_const_spec(a.shape) for a in (x_t, cst_t, pst_t)] + _in_side_specs(sw, layer),
        out_specs=[_const_spec((rows, D_MODEL))] * 4 + [_const_spec(tb)] * 2,
        out_shape=[
            jax.ShapeDtypeStruct((rows, D_MODEL), BF16),
            jax.ShapeDtypeStruct((rows, D_MODEL), BF16),
            jax.ShapeDtypeStruct((rows, D_MODEL), F32),
            jax.ShapeDtypeStruct((rows, D_MODEL), F32),
            jax.ShapeDtypeStruct(tb, F32),
            jax.ShapeDtypeStruct(tb, F32),
        ],
        compiler_params=_params(1),
        name="sample_in",
    )(x_t, cst_t, pst_t, *(sw[n] for n in IN_SIDE_WEIGHTS))


def _out_side_kernel(x_ref, ac_ref, ap_ref, aa_ref, g_ref, wmc_ref, wmp_ref, wma_ref,
                     wbc_ref, wbp_ref, wba_ref, wout_ref, fg_ref, y_ref, m_s, *, final_norm):
    x = x_ref[...]
    h = _rmsnorm(x, g_ref[...]).astype(BF16)
    ac, ap, aa = ac_ref[...], ap_ref[...], aa_ref[...].astype(BF16)
    for c in range(N_CHUNKS):
        sl = _cols(0, c)
        conv_br = _dot(ac, wbc_ref[:, sl])
        pool_br = _dot(ap, wbp_ref[:, sl])
        att_br = _dot(aa, wba_ref[:, sl])
        mc = _dot(h, wmc_ref[:, sl])
        mp = _dot(h, wmp_ref[:, sl])
        ma = _dot(h, wma_ref[:, sl])
        merged = _sigmoid(mc) * conv_br + _sigmoid(mp) * pool_br + _sigmoid(ma) * att_br
        m_s[:, sl] = merged.astype(BF16)
    xn = x + _dot(m_s[...], wout_ref[...])
    if final_norm:
        xn = _rmsnorm(xn, fg_ref[...])
    y_ref[...] = xn


def _out_side(x2d, ac, ap, aa, sw, layer, final_norm, name):
    rows = x2d.shape[0]
    tm = min(TM_OUT, rows)
    tile = pl.BlockSpec((tm, D_MODEL), lambda r: (r, 0))
    rest = ("wbc", "wbp", "wba", "wout")
    weights = (sw["g"],) + (sw["win"],) * 3 + tuple(sw[n] for n in rest) + (sw["fg"],)
    w_specs = ([_layer_spec(sw["g"], layer)]
               + [_layer_spec(sw["win"], layer, D_MODEL, blk) for blk in MERGE_BLOCKS]
               + [_layer_spec(sw[n], layer) for n in rest] + [_const_spec(sw["fg"].shape)])
    return pl.pallas_call(
        functools.partial(_out_side_kernel, final_norm=final_norm),
        grid=(rows // tm,),
        in_specs=[tile] * 4 + w_specs,
        out_specs=tile,
        out_shape=jax.ShapeDtypeStruct((rows, D_MODEL), F32),
        scratch_shapes=[pltpu.VMEM((tm, D_MODEL), BF16)],
        compiler_params=_params(1),
        name=name,
    )(x2d, ac, ap, aa, *weights)


def _to_batch_major(a2d, n_t, n_b):
    return jnp.transpose(a2d.reshape(n_t, n_b, D_MODEL), (1, 0, 2))


def kernel(x_prompt, x_sample, mem_prompt, cache_mem_k, cache_mem_v, state_conv, state_pool, norm_g, w_in, conv_w, pool_w, pool_scale, mem_norm_g, w_mem_kv, w_br_conv, w_br_pool, w_br_att, w_out, final_norm_g):
    depth = w_in.shape[0]
    b_p, t_p, _ = x_prompt.shape
    n_b, n_t, _ = x_sample.shape
    rows_p, rows_s = b_p * t_p, n_b * n_t

    k_f, v_f, k_b, v_b = _mem_kv(mem_prompt.reshape(b_p * N_MEM, D_MODEL), mem_norm_g, w_mem_kv.astype(BF16))

    sw = dict(
        g=norm_g.reshape(depth, 1, D_MODEL), win=w_in.astype(BF16), conv_w=conv_w,
        pool_w=pool_w.astype(BF16), pool_scale=pool_scale.reshape(depth, 1, D_MODEL),
        wbc=w_br_conv.astype(BF16), wbp=w_br_pool.astype(BF16), wba=w_br_att.astype(BF16),
        wout=w_out.astype(BF16), fg=final_norm_g.reshape(1, D_MODEL))

    xp, xs = x_prompt, x_sample
    cv_p, pl_p, cv_s, pl_s = [], [], [], []
    for l in range(depth):
        final = l == depth - 1
        ac_s, ap_s, q, sg, u_t, hp_t = _sample_in(
            jnp.transpose(xs, (1, 0, 2)), jnp.transpose(state_conv[l], (1, 0, 2)),
            jnp.transpose(state_pool[l], (1, 0, 2)), sw, l)
        cv_s.append(jnp.transpose(u_t[n_t - (CONV_WIDTH - 1):], (1, 0, 2)))
        pl_s.append(jnp.concatenate([state_pool[l][:, n_t:], jnp.transpose(hp_t, (1, 0, 2))], axis=1))

        ac_p, ap_p, aa_p, aa_s, c_new, p_new = _prompt_in(
            xp, k_b, v_b, _to_batch_major(q, n_t, n_b), _to_batch_major(sg, n_t, n_b),
            cache_mem_k, cache_mem_v, sw, l)
        cv_p.append(c_new)
        pl_p.append(p_new)

        flat = lambda a: a.reshape(rows_p, D_MODEL)
        xp = _out_side(flat(xp), flat(ac_p), flat(ap_p), flat(aa_p), sw, l, final,
                       "prompt_out").reshape(b_p, t_p, D_MODEL)
        xs = _out_side(
            xs.reshape(rows_s, D_MODEL),
            _to_batch_major(ac_s, n_t, n_b).reshape(rows_s, D_MODEL),
            _to_batch_major(ap_s, n_t, n_b).reshape(rows_s, D_MODEL),
            aa_s.reshape(rows_s, D_MODEL), sw, l, final, "sample_out").reshape(n_b, n_t, D_MODEL)

    return (xp, xs, k_f, v_f, jnp.stack(cv_p), jnp.stack(pl_p), jnp.stack(cv_s), jnp.stack(pl_s))
```

```python
import functools

import jax
import jax.numpy as jnp
from jax import lax
from jax.experimental import pallas as pl
from jax.experimental.pallas import tpu as pltpu

D_MODEL = 1024
N_MEM = 256
HEADS = 4
HEAD_DIM = 256
CONV_WIDTH = 3
POOL_WINDOWS = (2, 4, 8, 16)
POOL_GROUP = 256
POOL_STATE = 15
PAST_LEN = 16384
EPS = 1e-6
OFF_HC, OFF_BC, OFF_CC, OFF_GC = 0, 1024, 2048, 3072
OFF_HP, OFF_GP = 4096, 5120
OFF_Q = 6144
D_IN_A = 7168
GA_BLOCK = 7
MERGE_BLOCKS = (8, 9, 10)

CW = 256
N_CHUNKS = D_MODEL // CW
TM = 512
TM_OUT = 1024
POOL_PAD = 32
CONV_PAD = 8
VMEM_BYTES_V7X = 64 * 1024 * 1024
VMEM_LIMIT = VMEM_BYTES_V7X - 2 * 1024 * 1024

F32 = jnp.float32
BF16 = jnp.bfloat16


def _dot(a, b):
    return jnp.dot(a, b, preferred_element_type=F32)


def _dot_nt(a, b):
    return lax.dot_general(a, b, (((1,), (1,)), ((), ())), preferred_element_type=F32)


def _sigmoid(x):
    return 0.5 * jnp.tanh(0.5 * x) + 0.5


def _silu(x):
    return x * _sigmoid(x)


def _rmsnorm(x, g):
    ms = jnp.mean(x * x, axis=-1, keepdims=True)
    return (x * lax.rsqrt(ms + EPS)) * g


def _softmax(s):
    e = jnp.exp(s - jnp.max(s, axis=-1, keepdims=True))
    return e * (1.0 / jnp.sum(e, axis=-1, keepdims=True))


def _cols(off, c, w=CW):
    return slice(off + c * w, off + (c + 1) * w)


def _params(n_axes):
    return pltpu.CompilerParams(
        dimension_semantics=("arbitrary",) * n_axes, vmem_limit_bytes=VMEM_LIMIT)


def _const_spec(shape):
    nd = len(shape)
    return pl.BlockSpec(shape, lambda *_: (0,) * nd, pipeline_mode=pl.Buffered(1))


def _layer_spec(arr, layer, cols=None, col_block=0):
    tail = list(arr.shape[1:])
    idx = [0] * len(tail)
    if cols is not None:
        tail[-1] = cols
        idx[-1] = col_block
    return pl.BlockSpec((None, *tail), lambda *_: (layer, *idx), pipeline_mode=pl.Buffered(1))


IN_SIDE_WEIGHTS = ("g", "win", "win", "conv_w", "pool_w", "pool_scale")


def _in_side_specs(sw, layer):
    return [_layer_spec(sw["g"], layer), _layer_spec(sw["win"], layer, D_IN_A, 0),
            _layer_spec(sw["win"], layer, D_MODEL, GA_BLOCK), _layer_spec(sw["conv_w"], layer),
            _layer_spec(sw["pool_w"], layer), _layer_spec(sw["pool_scale"], layer)]


def _kv_kernel(mem_ref, g_ref, w_ref, k_ref, v_ref, kb_ref, vb_ref):
    h = _rmsnorm(mem_ref[...], g_ref[0]).astype(BF16)
    k = _dot(h, w_ref[0, :, :D_MODEL])
    v = _dot(h, w_ref[0, :, D_MODEL:])
    k_ref[0] = k.reshape(k_ref.shape[1:])
    v_ref[0] = v.reshape(v_ref.shape[1:])
    kb_ref[0] = k.astype(BF16)
    vb_ref[0] = v.astype(BF16)


def _mem_kv(mem2d, mem_norm_g, w_kv_bf16):
    depth = w_kv_bf16.shape[0]
    rows = mem2d.shape[0]
    bt = 2
    rt = bt * N_MEM
    out_f = jax.ShapeDtypeStruct((depth, rows // N_MEM, N_MEM, HEADS, HEAD_DIM), F32)
    out_b = jax.ShapeDtypeStruct((depth, rows, D_MODEL), BF16)
    blk = pl.BlockSpec((1, rt, D_MODEL), lambda l, r: (l, r, 0))
    blk5 = pl.BlockSpec((1, bt, N_MEM, HEADS, HEAD_DIM), lambda l, r: (l, r, 0, 0, 0))
    return pl.pallas_call(
        _kv_kernel,
        grid=(depth, rows // rt),
        in_specs=[
            pl.BlockSpec((rt, D_MODEL), lambda l, r: (r, 0)),
            pl.BlockSpec((1, 1, D_MODEL), lambda l, r: (l, 0, 0)),
            pl.BlockSpec((1, D_MODEL, 2 * D_MODEL), lambda l, r: (l, 0, 0)),
        ],
        out_specs=[blk5, blk5, blk, blk],
        out_shape=[out_f, out_f, out_b, out_b],
        compiler_params=_params(2),
        name="mem_kv",
    )(mem2d, mem_norm_g.reshape(depth, 1, D_MODEL), w_kv_bf16)


KV_TILE = CW // HEADS
N_KV_TILES = N_MEM // KV_TILE


def _sample_attention_tiles(sq_ref, ssg_ref, sk_ref, sv_ref, saa_ref, sb, n_t):
    n_flat = N_MEM * HEADS
    col_head = lax.broadcasted_iota(jnp.int32, (HEADS * n_t, n_flat), 1) % HEADS
    row_head = lax.broadcasted_iota(jnp.int32, (HEADS * n_t, n_flat), 0) // n_t
    own = col_head == row_head
    for i in range(sb):
        qb = sq_ref[i]
        qe = jnp.concatenate([qb[:, _cols(0, hh, HEAD_DIM)] for hh in range(HEADS)], axis=0).astype(BF16)
        s = []
        for j in range(N_KV_TILES):
            kf = sk_ref[0, i, j * KV_TILE:(j + 1) * KV_TILE].reshape(CW, HEAD_DIM).astype(BF16)
            s.append(_dot_nt(qe, kf))
            yield
        s = jnp.where(own, jnp.concatenate(s, axis=1) * HEAD_DIM ** -0.5, -jnp.inf)
        p = _softmax(s).astype(BF16)
        o = None
        for j in range(N_KV_TILES):
            vf = sv_ref[0, i, j * KV_TILE:(j + 1) * KV_TILE].reshape(CW, HEAD_DIM).astype(BF16)
            part = _dot(p[:, _cols(0, j)], vf)
            o = part if o is None else o + part
            yield
        o = jnp.concatenate([o[hh * n_t:(hh + 1) * n_t] for hh in range(HEADS)], axis=1)
        saa_ref[i] = o * ssg_ref[i]


def _prompt_in_kernel(x_ref, k_ref, v_ref, sq_ref, ssg_ref, sk_ref, sv_ref,
                      g_ref, wina_ref, wga_ref, convw_ref, poolw_ref, pscale_ref,
                      ac_ref, ap_ref, aa_ref, saa_ref, cst_ref, pst_ref,
                      u_s, p_s, sa_s, sb_s, *, sb, n_t):
    t = pl.program_id(1)

    @pl.when(t == 0)
    def _():
        u_s[0:CONV_PAD, :] = jnp.zeros((CONV_PAD, D_MODEL), F32)
        p_s[0:POOL_PAD, :] = jnp.zeros((POOL_PAD, D_MODEL), F32)
        sa_s[0:16, :] = jnp.zeros((16, POOL_GROUP), F32)
        sb_s[0:16, :] = jnp.zeros((16, POOL_GROUP), F32)

    sample_attn = _sample_attention_tiles(sq_ref, ssg_ref, sk_ref, sv_ref, saa_ref, sb, n_t)
    for _ in range(N_KV_TILES):
        next(sample_attn, None)
    x = x_ref[0]
    h = _rmsnorm(x, g_ref[...]).astype(BF16)

    def hdot(w_ref, off, c):
        r = _dot(h, w_ref[:, _cols(off, c)])
        next(sample_attn, None)
        return r

    for c in range(N_CHUNKS):
        sl = _cols(0, c)
        hc = hdot(wina_ref, OFF_HC, c)
        cc = hdot(wina_ref, OFF_CC, c)
        u = cc * hc
        u_s[CONV_PAD:CONV_PAD + TM, sl] = u
        u1 = u_s[CONV_PAD - 1:CONV_PAD - 1 + TM, sl]
        u2 = u_s[CONV_PAD - 2:CONV_PAD - 2 + TM, sl]
        y = convw_ref[0:1, sl] * u2 + convw_ref[1:2, sl] * u1 + convw_ref[2:3, sl] * u
        bc = hdot(wina_ref, OFF_BC, c)
        gc = hdot(wina_ref, OFF_GC, c)
        ac_ref[0, :, sl] = (bc * y * _silu(gc)).astype(BF16)
    new_conv = u_s[CONV_PAD + TM - 2:CONV_PAD + TM, :]
    cst_ref[0] = new_conv
    u_s[CONV_PAD - 2:CONV_PAD, :] = new_conv

    pos1 = (t * TM + 1 + lax.broadcasted_iota(jnp.int32, (TM, 1), 0)).astype(F32)
    n = TM + 16
    hp_next = hdot(wina_ref, OFF_HP, 0)
    for g, w in enumerate(POOL_WINDOWS):
        sl = _cols(0, g, POOL_GROUP)
        hp = hp_next
        p_s[POOL_PAD:POOL_PAD + TM, sl] = hp
        cur = p_s[16:16 + n, sl] + p_s[15:15 + n, sl]
        shift, src, dst = 2, sa_s, sb_s
        while shift < w:
            src[16:16 + n, :] = cur
            cur = src[16:16 + n, :] + src[16 - shift:16 - shift + n, :]
            shift *= 2
            src, dst = dst, src
        inv_cnt = 1.0 / jnp.minimum(pos1, float(w))
        mixed = (cur[16:, :] * inv_cnt - hp).astype(BF16)
        gp = hdot(wina_ref, OFF_GP, g)
        if g + 1 < len(POOL_WINDOWS):
            hp_next = hdot(wina_ref, OFF_HP, g + 1)
        pooled = _dot(mixed, poolw_ref[g]) * pscale_ref[:, sl]
        ap_ref[0, :, sl] = (pooled * _silu(gp)).astype(BF16)
    pst_ref[0] = p_s[POOL_PAD + TM - POOL_STATE:POOL_PAD + TM, :]
    p_s[16:POOL_PAD, :] = p_s[TM + 16:TM + POOL_PAD, :]

    def probs(hh):
        q = hdot(wina_ref, OFF_Q, hh).astype(BF16)
        return _softmax(_dot_nt(q, k_ref[:, _cols(0, hh)]) * HEAD_DIM ** -0.5).astype(BF16)

    p_next = probs(0)
    for hh in range(HEADS):
        sl = _cols(0, hh, HEAD_DIM)
        p = p_next
        ga = hdot(wga_ref, 0, hh)
        if hh + 1 < HEADS:
            p_next = probs(hh + 1)
        o = _dot(p, v_ref[:, sl])
        aa_ref[0, :, sl] = (o * _silu(ga)).astype(BF16)

    for _ in sample_attn:
        pass


def _prompt_in(x, kb, vb, sq, ssg, cache_k, cache_v, sw, layer):
    b, t, _ = x.shape
    nt = t // TM
    n_b, n_t, _ = sq.shape
    sb = n_b // (b * nt)
    assert sb * b * nt == n_b
    tile = pl.BlockSpec((1, TM, D_MODEL), lambda i, j: (i, j, 0))
    smp = pl.BlockSpec((sb, n_t, D_MODEL), lambda i, j: (i * nt + j, 0, 0))
    skv = pl.BlockSpec((1, sb, N_MEM, HEADS, HEAD_DIM), lambda i, j: (layer, i * nt + j, 0, 0, 0))
    act = jax.ShapeDtypeStruct((b, t, D_MODEL), BF16)
    return pl.pallas_call(
        functools.partial(_prompt_in_kernel, sb=sb, n_t=n_t),
        grid=(b, nt),
        in_specs=[
            tile,
            pl.BlockSpec((None, N_MEM, D_MODEL), lambda i, j: (layer, i, 0)),
            pl.BlockSpec((None, N_MEM, D_MODEL), lambda i, j: (layer, i, 0)),
            smp, smp, skv, skv,
        ] + _in_side_specs(sw, layer),
        out_specs=[
            tile, tile, tile, smp,
            pl.BlockSpec((1, CONV_WIDTH - 1, D_MODEL), lambda i, j: (i, 0, 0)),
            pl.BlockSpec((1, POOL_STATE, D_MODEL), lambda i, j: (i, 0, 0)),
        ],
        out_shape=[
            act, act, act,
            jax.ShapeDtypeStruct(sq.shape, F32),
            jax.ShapeDtypeStruct((b, CONV_WIDTH - 1, D_MODEL), F32),
            jax.ShapeDtypeStruct((b, POOL_STATE, D_MODEL), F32),
        ],
        scratch_shapes=[
            pltpu.VMEM((CONV_PAD + TM, D_MODEL), F32),
            pltpu.VMEM((POOL_PAD + TM, D_MODEL), F32),
            pltpu.VMEM((POOL_PAD + TM, POOL_GROUP), F32),
            pltpu.VMEM((POOL_PAD + TM, POOL_GROUP), F32),
        ],
        compiler_params=_params(2),
        name="prompt_in",
    )(x, kb, vb, sq, ssg, cache_k, cache_v, *(sw[n] for n in IN_SIDE_WEIGHTS))


def _sample_in_kernel(x_ref, cst_ref, pst_ref, g_ref, wina_ref, wga_ref, convw_ref, poolw_ref, pscale_ref,
                      ac_ref, ap_ref, q_ref, sg_ref, u_ref, hp_ref, *, n_t, n_b):
    rows = n_t * n_b
    x = x_ref[...].reshape(rows, D_MODEL)
    h = _rmsnorm(x, g_ref[...]).astype(BF16)

    for c in range(N_CHUNKS):
        sl = _cols(0, c)
        hc = _dot(h, wina_ref[:, _cols(OFF_HC, c)])
        cc = _dot(h, wina_ref[:, _cols(OFF_CC, c)])
        u = cc * hc
        u_ref[:, :, sl] = u.reshape(n_t, n_b, CW)
        ext = [cst_ref[0, :, sl], cst_ref[1, :, sl]] + [u[i * n_b:(i + 1) * n_b] for i in range(n_t)]
        y = jnp.concatenate(
            [convw_ref[0:1, sl] * ext[i] + convw_ref[1:2, sl] * ext[i + 1] + convw_ref[2:3, sl] * ext[i + 2]
             for i in range(n_t)], axis=0)
        bc = _dot(h, wina_ref[:, _cols(OFF_BC, c)])
        gc = _dot(h, wina_ref[:, _cols(OFF_GC, c)])
        ac_ref[:, sl] = (bc * y * _silu(gc)).astype(BF16)

    for g, w in enumerate(POOL_WINDOWS):
        sl = _cols(0, g, POOL_GROUP)
        hp = _dot(h, wina_ref[:, _cols(OFF_HP, g, POOL_GROUP)])
        hp_ref[:, :, sl] = hp.reshape(n_t, n_b, POOL_GROUP)
        ext = [pst_ref[j, :, sl] for j in range(POOL_STATE)] + [hp[i * n_b:(i + 1) * n_b] for i in range(n_t)]
        mixed = []
        for i in range(n_t):
            cnt = float(min(PAST_LEN + i + 1, w))
            acc = ext[POOL_STATE + i]
            for j in range(1, w):
                acc = acc + ext[POOL_STATE + i - j]
            mixed.append(acc * (1.0 / cnt) - ext[POOL_STATE + i])
        mixed = jnp.concatenate(mixed, axis=0)
        pooled = _dot(mixed.astype(BF16), poolw_ref[g]) * pscale_ref[:, sl]
        gp = _dot(h, wina_ref[:, _cols(OFF_GP, g, POOL_GROUP)])
        ap_ref[:, sl] = (pooled * _silu(gp)).astype(BF16)

    for hh in range(HEADS):
        sl = _cols(0, hh, HEAD_DIM)
        q_ref[:, sl] = _dot(h, wina_ref[:, _cols(OFF_Q, hh, HEAD_DIM)])
        sg_ref[:, sl] = _silu(_dot(h, wga_ref[:, sl]))


def _sample_in(x_t, cst_t, pst_t, sw, layer):
    n_t, n_b, _ = x_t.shape
    rows = n_t * n_b
    tb = (n_t, n_b, D_MODEL)
    return pl.pallas_call(
        functools.partial(_sample_in_kernel, n_t=n_t, n_b=n_b),
        grid=(1,),
        in_specs=[_const_spec(a.shape) for a in (x_t, cst_t, pst_t)] + _in_side_specs(sw, layer),
        out_specs=[_const_spec((rows, D_MODEL))] * 4 + [_const_spec(tb)] * 2,
        out_shape=[
            jax.ShapeDtypeStruct((rows, D_MODEL), BF16),
            jax.ShapeDtypeStruct((rows, D_MODEL), BF16),
            jax.ShapeDtypeStruct((rows, D_MODEL), F32),
            jax.ShapeDtypeStruct((rows, D_MODEL), F32),
            jax.ShapeDtypeStruct(tb, F32),
            jax.ShapeDtypeStruct(tb, F32),
        ],
        compiler_params=_params(1),
        name="sample_in",
    )(x_t, cst_t, pst_t, *(sw[n] for n in IN_SIDE_WEIGHTS))


def _out_side_kernel(x_ref, ac_ref, ap_ref, aa_ref, g_ref, wmc_ref, wmp_ref, wma_ref,
                     wbc_ref, wbp_ref, wba_ref, wout_ref, fg_ref, y_ref, m_s, *, final_norm):
    x = x_ref[...]
    h = _rmsnorm(x, g_ref[...]).astype(BF16)
    ac, ap, aa = ac_ref[...], ap_ref[...], aa_ref[...].astype(BF16)
    for c in range(N_CHUNKS):
        sl = _cols(0, c)
        conv_br = _dot(ac, wbc_ref[:, sl])
        pool_br = _dot(ap, wbp_ref[:, sl])
        att_br = _dot(aa, wba_ref[:, sl])
        mc = _dot(h, wmc_ref[:, sl])
        mp = _dot(h, wmp_ref[:, sl])
        ma = _dot(h, wma_ref[:, sl])
        merged = _sigmoid(mc) * conv_br + _sigmoid(mp) * pool_br + _sigmoid(ma) * att_br
        m_s[:, sl] = merged.astype(BF16)
    xn = x + _dot(m_s[...], wout_ref[...])
    if final_norm:
        xn = _rmsnorm(xn, fg_ref[...])
    y_ref[...] = xn


def _out_side(x2d, ac, ap, aa, sw, layer, final_norm, name):
    rows = x2d.shape[0]
    tm = min(TM_OUT, rows)
    tile = pl.BlockSpec((tm, D_MODEL), lambda r: (r, 0))
    rest = ("wbc", "wbp", "wba", "wout")
    weights = (sw["g"],) + (sw["win"],) * 3 + tuple(sw[n] for n in rest) + (sw["fg"],)
    w_specs = ([_layer_spec(sw["g"], layer)]
               + [_layer_spec(sw["win"], layer, D_MODEL, blk) for blk in MERGE_BLOCKS]
               + [_layer_spec(sw[n], layer) for n in rest] + [_const_spec(sw["fg"].shape)])
    return pl.pallas_call(
        functools.partial(_out_side_kernel, final_norm=final_norm),
        grid=(rows // tm,),
        in_specs=[tile] * 4 + w_specs,
        out_specs=tile,
        out_shape=jax.ShapeDtypeStruct((rows, D_MODEL), F32),
        scratch_shapes=[pltpu.VMEM((tm, D_MODEL), BF16)],
        compiler_params=_params(1),
        name=name,
    )(x2d, ac, ap, aa, *weights)


def _to_batch_major(a2d, n_t, n_b):
    return jnp.transpose(a2d.reshape(n_t, n_b, D_MODEL), (1, 0, 2))


def kernel(x_prompt, x_sample, mem_prompt, cache_mem_k, cache_mem_v, state_conv, state_pool, norm_g, w_in, conv_w, pool_w, pool_scale, mem_norm_g, w_mem_kv, w_br_conv, w_br_pool, w_br_att, w_out, final_norm_g):
    depth = w_in.shape[0]
    b_p, t_p, _ = x_prompt.shape
    n_b, n_t, _ = x_sample.shape
    rows_p, rows_s = b_p * t_p, n_b * n_t

    k_f, v_f, k_b, v_b = _mem_kv(mem_prompt.reshape(b_p * N_MEM, D_MODEL), mem_norm_g, w_mem_kv.astype(BF16))

    sw = dict(
        g=norm_g.reshape(depth, 1, D_MODEL), win=w_in.astype(BF16), conv_w=conv_w,
        pool_w=pool_w.astype(BF16), pool_scale=pool_scale.reshape(depth, 1, D_MODEL),
        wbc=w_br_conv.astype(BF16), wbp=w_br_pool.astype(BF16), wba=w_br_att.astype(BF16),
        wout=w_out.astype(BF16), fg=final_norm_g.reshape(1, D_MODEL))

    xp, xs = x_prompt, x_sample
    cv_p, pl_p, cv_s, pl_s = [], [], [], []
    for l in range(depth):
        final = l == depth - 1
        ac_s, ap_s, q, sg, u_t, hp_t = _sample_in(
            jnp.transpose(xs, (1, 0, 2)), jnp.transpose(state_conv[l], (1, 0, 2)),
            jnp.transpose(state_pool[l], (1, 0, 2)), sw, l)
        cv_s.append(jnp.transpose(u_t[n_t - (CONV_WIDTH - 1):], (1, 0, 2)))
        pl_s.append(jnp.concatenate([state_pool[l][:, n_t:], jnp.transpose(hp_t, (1, 0, 2))], axis=1))

        ac_p, ap_p, aa_p, aa_s, c_new, p_new = _prompt_in(
            xp, k_b, v_b, _to_batch_major(q, n_t, n_b), _to_batch_major(sg, n_t, n_b),
            cache_mem_k, cache_mem_v, sw, l)
        cv_p.append(c_new)
        pl_p.append(p_new)

        flat = lambda a: a.reshape(rows_p, D_MODEL)
        xp = _out_side(flat(xp), flat(ac_p), flat(ap_p), flat(aa_p), sw, l, final,
                       "prompt_out").reshape(b_p, t_p, D_MODEL)
        xs = _out_side(
            xs.reshape(rows_s, D_MODEL),
            _to_batch_major(ac_s, n_t, n_b).reshape(rows_s, D_MODEL),
            _to_batch_major(ap_s, n_t, n_b).reshape(rows_s, D_MODEL),
            aa_s.reshape(rows_s, D_MODEL), sw, l, final, "sample_out").reshape(n_b, n_t, D_MODEL)

    return (xp, xs, k_f, v_f, jnp.stack(cv_p), jnp.stack(pl_p), jnp.stack(cv_s), jnp.stack(pl_s))
```

```python
import functools

import jax
import jax.numpy as jnp
from jax import lax
from jax.experimental import pallas as pl
from jax.experimental.pallas import tpu as pltpu

D_MODEL = 1024
N_MEM = 256
HEADS = 4
HEAD_DIM = 256
CONV_WIDTH = 3
POOL_WINDOWS = (2, 4, 8, 16)
POOL_GROUP = 256
POOL_STATE = 15
PAST_LEN = 16384
EPS = 1e-6
OFF_HC, OFF_BC, OFF_CC, OFF_GC = 0, 1024, 2048, 3072
OFF_HP, OFF_GP = 4096, 5120
OFF_Q = 6144
D_IN_A = 7168
GA_BLOCK = 7
MERGE_BLOCKS = (8, 9, 10)

CW = 256
N_CHUNKS = D_MODEL // CW
assert CW == POOL_GROUP == HEAD_DIM
TM = 512
TM_OUT = 1024
POOL_PAD = 32
CONV_PAD = 8
VMEM_BYTES_V7X = 64 * 1024 * 1024
VMEM_LIMIT = VMEM_BYTES_V7X - 2 * 1024 * 1024

F32 = jnp.float32
BF16 = jnp.bfloat16


def _dot(a, b):
    return jnp.dot(a, b, preferred_element_type=F32)


def _dot_nt(a, b):
    return lax.dot_general(a, b, (((1,), (1,)), ((), ())), preferred_element_type=F32)


def _sigmoid(x):
    return 0.5 * jnp.tanh(0.5 * x) + 0.5


def _silu(x):
    return x * _sigmoid(x)


def _rmsnorm(x, g):
    ms = jnp.mean(x * x, axis=-1, keepdims=True)
    return (x * lax.rsqrt(ms + EPS)) * g


def _softmax(s):
    e = jnp.exp(s - jnp.max(s, axis=-1, keepdims=True))
    return e * (1.0 / jnp.sum(e, axis=-1, keepdims=True))


def _cols(off, c, w=CW):
    return slice(off + c * w, off + (c + 1) * w)


def _params(n_axes):
    return pltpu.CompilerParams(
        dimension_semantics=("arbitrary",) * n_axes, vmem_limit_bytes=VMEM_LIMIT)


def _const_spec(shape):
    nd = len(shape)
    return pl.BlockSpec(shape, lambda *_: (0,) * nd, pipeline_mode=pl.Buffered(1))


def _layer_spec(arr, layer, cols=None, col_block=0):
    tail = list(arr.shape[1:])
    idx = [0] * len(tail)
    if cols is not None:
        tail[-1] = cols
        idx[-1] = col_block
    return pl.BlockSpec((None, *tail), lambda *_: (layer, *idx), pipeline_mode=pl.Buffered(1))


IN_SIDE_WEIGHTS = ("g", "win", "win", "conv_w", "pool_w", "pool_scale")


def _in_side_specs(sw, layer):
    return [_layer_spec(sw["g"], layer), _layer_spec(sw["win"], layer, D_IN_A, 0),
            _layer_spec(sw["win"], layer, D_MODEL, GA_BLOCK), _layer_spec(sw["conv_w"], layer),
            _layer_spec(sw["pool_w"], layer), _layer_spec(sw["pool_scale"], layer)]


def _kv_kernel(mem_ref, g_ref, w_ref, k_ref, v_ref, kb_ref, vb_ref):
    h = _rmsnorm(mem_ref[...], g_ref[0]).astype(BF16)
    k = _dot(h, w_ref[0, :, :D_MODEL])
    v = _dot(h, w_ref[0, :, D_MODEL:])
    k_ref[0] = k.reshape(k_ref.shape[1:])
    v_ref[0] = v.reshape(v_ref.shape[1:])
    kb_ref[0] = k.astype(BF16)
    vb_ref[0] = v.astype(BF16)


def _mem_kv(mem2d, mem_norm_g, w_kv_bf16):
    depth = w_kv_bf16.shape[0]
    rows = mem2d.shape[0]
    bt = 2
    rt = bt * N_MEM
    out_f = jax.ShapeDtypeStruct((depth, rows // N_MEM, N_MEM, HEADS, HEAD_DIM), F32)
    out_b = jax.ShapeDtypeStruct((depth, rows, D_MODEL), BF16)
    blk = pl.BlockSpec((1, rt, D_MODEL), lambda l, r: (l, r, 0))
    blk5 = pl.BlockSpec((1, bt, N_MEM, HEADS, HEAD_DIM), lambda l, r: (l, r, 0, 0, 0))
    return pl.pallas_call(
        _kv_kernel,
        grid=(depth, rows // rt),
        in_specs=[
            pl.BlockSpec((rt, D_MODEL), lambda l, r: (r, 0)),
            pl.BlockSpec((1, 1, D_MODEL), lambda l, r: (l, 0, 0)),
            pl.BlockSpec((1, D_MODEL, 2 * D_MODEL), lambda l, r: (l, 0, 0)),
        ],
        out_specs=[blk5, blk5, blk, blk],
        out_shape=[out_f, out_f, out_b, out_b],
        compiler_params=_params(2),
        name="mem_kv",
    )(mem2d, mem_norm_g.reshape(depth, 1, D_MODEL), w_kv_bf16)


KV_TILE = CW // HEADS
N_KV_TILES = N_MEM // KV_TILE


def _sample_attention_tiles(sq_ref, ssg_ref, sk_ref, sv_ref, saa_ref, sb, n_t):
    n_flat = N_MEM * HEADS
    col_head = lax.broadcasted_iota(jnp.int32, (HEADS * n_t, n_flat), 1) % HEADS
    row_head = lax.broadcasted_iota(jnp.int32, (HEADS * n_t, n_flat), 0) // n_t
    own = col_head == row_head
    for i in range(sb):
        qb = sq_ref[i]
        qe = jnp.concatenate([qb[:, _cols(0, hh, HEAD_DIM)] for hh in range(HEADS)], axis=0).astype(BF16)
        s = []
        for j in range(N_KV_TILES):
            kf = sk_ref[0, i, j * KV_TILE:(j + 1) * KV_TILE].reshape(CW, HEAD_DIM).astype(BF16)
            s.append(_dot_nt(qe, kf))
            yield
        s = jnp.where(own, jnp.concatenate(s, axis=1) * HEAD_DIM ** -0.5, -jnp.inf)
        p = _softmax(s).astype(BF16)
        o = None
        for j in range(N_KV_TILES):
            vf = sv_ref[0, i, j * KV_TILE:(j + 1) * KV_TILE].reshape(CW, HEAD_DIM).astype(BF16)
            part = _dot(p[:, _cols(0, j)], vf)
            o = part if o is None else o + part
            yield
        o = jnp.concatenate([o[hh * n_t:(hh + 1) * n_t] for hh in range(HEADS)], axis=1)
        saa_ref[i] = o * ssg_ref[i]


def _prompt_in_kernel(x_ref, k_ref, v_ref, sq_ref, ssg_ref, sk_ref, sv_ref,
                      g_ref, wina_ref, wga_ref, convw_ref, poolw_ref, pscale_ref,
                      ac_ref, ap_ref, aa_ref, saa_ref, cst_ref, pst_ref,
                      u_s, p_s, sa_s, sb_s, *, sb, n_t):
    t = pl.program_id(1)

    @pl.when(t == 0)
    def _():
        u_s[0:CONV_PAD, :] = jnp.zeros((CONV_PAD, D_MODEL), F32)
        p_s[0:POOL_PAD, :] = jnp.zeros((POOL_PAD, D_MODEL), F32)
        sa_s[0:16, :] = jnp.zeros((16, POOL_GROUP), F32)
        sb_s[0:16, :] = jnp.zeros((16, POOL_GROUP), F32)

    sample_attn = _sample_attention_tiles(sq_ref, ssg_ref, sk_ref, sv_ref, saa_ref, sb, n_t)
    for _ in range(N_KV_TILES):
        next(sample_attn, None)
    x = x_ref[0]
    h = _rmsnorm(x, g_ref[...]).astype(BF16)

    def hdot(w_ref, off, c):
        r = _dot(h, w_ref[:, _cols(off, c)])
        next(sample_attn, None)
        return r

    pos1 = (t * TM + 1 + lax.broadcasted_iota(jnp.int32, (TM, 1), 0)).astype(F32)
    n = TM + 16
    for c, w in enumerate(POOL_WINDOWS):
        sl = _cols(0, c)

        hp = hdot(wina_ref, OFF_HP, c)
        p_s[POOL_PAD:POOL_PAD + TM, sl] = hp
        cur = p_s[16:16 + n, sl] + p_s[15:15 + n, sl]
        shift, src, dst = 2, sa_s, sb_s
        while shift < w:
            src[16:16 + n, :] = cur
            cur = src[16:16 + n, :] + src[16 - shift:16 - shift + n, :]
            shift *= 2
            src, dst = dst, src
        inv_cnt = 1.0 / jnp.minimum(pos1, float(w))
        mixed = (cur[16:, :] * inv_cnt - hp).astype(BF16)

        q = hdot(wina_ref, OFF_Q, c).astype(BF16)

        hc = hdot(wina_ref, OFF_HC, c)
        cc = hdot(wina_ref, OFF_CC, c)
        u = cc * hc
        u_s[CONV_PAD:CONV_PAD + TM, sl] = u
        u1 = u_s[CONV_PAD - 1:CONV_PAD - 1 + TM, sl]
        u2 = u_s[CONV_PAD - 2:CONV_PAD - 2 + TM, sl]
        y = convw_ref[0:1, sl] * u2 + convw_ref[1:2, sl] * u1 + convw_ref[2:3, sl] * u

        p = _softmax(_dot_nt(q, k_ref[:, sl]) * HEAD_DIM ** -0.5).astype(BF16)

        bc = hdot(wina_ref, OFF_BC, c)
        gc = hdot(wina_ref, OFF_GC, c)
        ac_ref[0, :, sl] = (bc * y * _silu(gc)).astype(BF16)

        gp = hdot(wina_ref, OFF_GP, c)
        pooled = _dot(mixed, poolw_ref[c]) * pscale_ref[:, sl]
        ap_ref[0, :, sl] = (pooled * _silu(gp)).astype(BF16)

        ga = hdot(wga_ref, 0, c)
        o = _dot(p, v_ref[:, sl])
        aa_ref[0, :, sl] = (o * _silu(ga)).astype(BF16)

    new_conv = u_s[CONV_PAD + TM - 2:CONV_PAD + TM, :]
    cst_ref[0] = new_conv
    u_s[CONV_PAD - 2:CONV_PAD, :] = new_conv
    pst_ref[0] = p_s[POOL_PAD + TM - POOL_STATE:POOL_PAD + TM, :]
    p_s[16:POOL_PAD, :] = p_s[TM + 16:TM + POOL_PAD, :]

    for _ in sample_attn:
        pass


def _prompt_in(x, kb, vb, sq, ssg, cache_k, cache_v, sw, layer):
    b, t, _ = x.shape
    nt = t // TM
    n_b, n_t, _ = sq.shape
    sb = n_b // (b * nt)
    assert sb * b * nt == n_b
    tile = pl.BlockSpec((1, TM, D_MODEL), lambda i, j: (i, j, 0))
    smp = pl.BlockSpec((sb, n_t, D_MODEL), lambda i, j: (i * nt + j, 0, 0))
    skv = pl.BlockSpec((1, sb, N_MEM, HEADS, HEAD_DIM), lambda i, j: (layer, i * nt + j, 0, 0, 0))
    act = jax.ShapeDtypeStruct((b, t, D_MODEL), BF16)
    return pl.pallas_call(
        functools.partial(_prompt_in_kernel, sb=sb, n_t=n_t),
        grid=(b, nt),
        in_specs=[
            tile,
            pl.BlockSpec((None, N_MEM, D_MODEL), lambda i, j: (layer, i, 0)),
            pl.BlockSpec((None, N_MEM, D_MODEL), lambda i, j: (layer, i, 0)),
            smp, smp, skv, skv,
        ] + _in_side_specs(sw, layer),
        out_specs=[
            tile, tile, tile, smp,
            pl.BlockSpec((1, CONV_WIDTH - 1, D_MODEL), lambda i, j: (i, 0, 0)),
            pl.BlockSpec((1, POOL_STATE, D_MODEL), lambda i, j: (i, 0, 0)),
        ],
        out_shape=[
            act, act, act,
            jax.ShapeDtypeStruct(sq.shape, F32),
            jax.ShapeDtypeStruct((b, CONV_WIDTH - 1, D_MODEL), F32),
            jax.ShapeDtypeStruct((b, POOL_STATE, D_MODEL), F32),
        ],
        scratch_shapes=[
            pltpu.VMEM((CONV_PAD + TM, D_MODEL), F32),
            pltpu.VMEM((POOL_PAD + TM, D_MODEL), F32),
            pltpu.VMEM((POOL_PAD + TM, POOL_GROUP), F32),
            pltpu.VMEM((POOL_PAD + TM, POOL_GROUP), F32),
        ],
        compiler_params=_params(2),
        name="prompt_in",
    )(x, kb, vb, sq, ssg, cache_k, cache_v, *(sw[n] for n in IN_SIDE_WEIGHTS))


def _sample_in_kernel(x_ref, cst_ref, pst_ref, g_ref, wina_ref, wga_ref, convw_ref, poolw_ref, pscale_ref,
                      ac_ref, ap_ref, q_ref, sg_ref, u_ref, hp_ref, *, n_t, n_b):
    rows = n_t * n_b
    x = x_ref[...].reshape(rows, D_MODEL)
    h = _rmsnorm(x, g_ref[...]).astype(BF16)

    for c in range(N_CHUNKS):
        sl = _cols(0, c)
        hc = _dot(h, wina_ref[:, _cols(OFF_HC, c)])
        cc = _dot(h, wina_ref[:, _cols(OFF_CC, c)])
        u = cc * hc
        u_ref[:, :, sl] = u.reshape(n_t, n_b, CW)
        ext = [cst_ref[0, :, sl], cst_ref[1, :, sl]] + [u[i * n_b:(i + 1) * n_b] for i in range(n_t)]
        y = jnp.concatenate(
            [convw_ref[0:1, sl] * ext[i] + convw_ref[1:2, sl] * ext[i + 1] + convw_ref[2:3, sl] * ext[i + 2]
             for i in range(n_t)], axis=0)
        bc = _dot(h, wina_ref[:, _cols(OFF_BC, c)])
        gc = _dot(h, wina_ref[:, _cols(OFF_GC, c)])
        ac_ref[:, sl] = (bc * y * _silu(gc)).astype(BF16)

    for g, w in enumerate(POOL_WINDOWS):
        sl = _cols(0, g, POOL_GROUP)
        hp = _dot(h, wina_ref[:, _cols(OFF_HP, g, POOL_GROUP)])
        hp_ref[:, :, sl] = hp.reshape(n_t, n_b, POOL_GROUP)
        ext = [pst_ref[j, :, sl] for j in range(POOL_STATE)] + [hp[i * n_b:(i + 1) * n_b] for i in range(n_t)]
        mixed = []
        for i in range(n_t):
            cnt = float(min(PAST_LEN + i + 1, w))
            acc = ext[POOL_STATE + i]
            for j in range(1, w):
                acc = acc + ext[POOL_STATE + i - j]
            mixed.append(acc * (1.0 / cnt) - ext[POOL_STATE + i])
        mixed = jnp.concatenate(mixed, axis=0)
        pooled = _dot(mixed.astype(BF16), poolw_ref[g]) * pscale_ref[:, sl]
        gp = _dot(h, wina_ref[:, _cols(OFF_GP, g, POOL_GROUP)])
        ap_ref[:, sl] = (pooled * _silu(gp)).astype(BF16)

    for hh in range(HEADS):
        sl = _cols(0, hh, HEAD_DIM)
        q_ref[:, sl] = _dot(h, wina_ref[:, _cols(OFF_Q, hh, HEAD_DIM)])
        sg_ref[:, sl] = _silu(_dot(h, wga_ref[:, sl]))


def _sample_in(x_t, cst_t, pst_t, sw, layer):
    n_t, n_b, _ = x_t.shape
    rows = n_t * n_b
    tb = (n_t, n_b, D_MODEL)
    return pl.pallas_call(
        functools.partial(_sample_in_kernel, n_t=n_t, n_b=n_b),
        grid=(1,),
        in_specs=[_const_spec(a.shape) for a in (x_t, cst_t, pst_t)] + _in_side_specs(sw, layer),
        out_specs=[_const_spec((rows, D_MODEL))] * 4 + [_const_spec(tb)] * 2,
        out_shape=[
            jax.ShapeDtypeStruct((rows, D_MODEL), BF16),
            jax.ShapeDtypeStruct((rows, D_MODEL), BF16),
            jax.ShapeDtypeStruct((rows, D_MODEL), F32),
            jax.ShapeDtypeStruct((rows, D_MODEL), F32),
            jax.ShapeDtypeStruct(tb, F32),
            jax.ShapeDtypeStruct(tb, F32),
        ],
        compiler_params=_params(1),
        name="sample_in",
    )(x_t, cst_t, pst_t, *(sw[n] for n in IN_SIDE_WEIGHTS))


def _out_side_kernel(x_ref, ac_ref, ap_ref, aa_ref, g_ref, wmc_ref, wmp_ref, wma_ref,
                     wbc_ref, wbp_ref, wba_ref, wout_ref, fg_ref, y_ref, m_s, *, final_norm):
    x = x_ref[...]
    h = _rmsnorm(x, g_ref[...]).astype(BF16)
    ac, ap, aa = ac_ref[...], ap_ref[...], aa_ref[...].astype(BF16)
    for c in range(N_CHUNKS):
        sl = _cols(0, c)
        conv_br = _dot(ac, wbc_ref[:, sl])
        pool_br = _dot(ap, wbp_ref[:, sl])
        att_br = _dot(aa, wba_ref[:, sl])
        mc = _dot(h, wmc_ref[:, sl])
        mp = _dot(h, wmp_ref[:, sl])
        ma = _dot(h, wma_ref[:, sl])
        merged = _sigmoid(mc) * conv_br + _sigmoid(mp) * pool_br + _sigmoid(ma) * att_br
        m_s[:, sl] = merged.astype(BF16)
    xn = x + _dot(m_s[...], wout_ref[...])
    if final_norm:
        xn = _rmsnorm(xn, fg_ref[...])
    y_ref[...] = xn


def _out_side(x2d, ac, ap, aa, sw, layer, final_norm, name):
    rows = x2d.shape[0]
    tm = min(TM_OUT, rows)
    tile = pl.BlockSpec((tm, D_MODEL), lambda r: (r, 0))
    rest = ("wbc", "wbp", "wba", "wout")
    weights = (sw["g"],) + (sw["win"],) * 3 + tuple(sw[n] for n in rest) + (sw["fg"],)
    w_specs = ([_layer_spec(sw["g"], layer)]
               + [_layer_spec(sw["win"], layer, D_MODEL, blk) for blk in MERGE_BLOCKS]
               + [_layer_spec(sw[n], layer) for n in rest] + [_const_spec(sw["fg"].shape)])
    return pl.pallas_call(
        functools.partial(_out_side_kernel, final_norm=final_norm),
        grid=(rows // tm,),
        in_specs=[tile] * 4 + w_specs,
        out_specs=tile,
        out_shape=jax.ShapeDtypeStruct((rows, D_MODEL), F32),
        scratch_shapes=[pltpu.VMEM((tm, D_MODEL), BF16)],
        compiler_params=_params(1),
        name=name,
    )(x2d, ac, ap, aa, *weights)


def _to_batch_major(a2d, n_t, n_b):
    return jnp.transpose(a2d.reshape(n_t, n_b, D_MODEL), (1, 0, 2))


def kernel(x_prompt, x_sample, mem_prompt, cache_mem_k, cache_mem_v, state_conv, state_pool, norm_g, w_in, conv_w, pool_w, pool_scale, mem_norm_g, w_mem_kv, w_br_conv, w_br_pool, w_br_att, w_out, final_norm_g):
    depth = w_in.shape[0]
    b_p, t_p, _ = x_prompt.shape
    n_b, n_t, _ = x_sample.shape
    rows_p, rows_s = b_p * t_p, n_b * n_t

    k_f, v_f, k_b, v_b = _mem_kv(mem_prompt.reshape(b_p * N_MEM, D_MODEL), mem_norm_g, w_mem_kv.astype(BF16))

    sw = dict(
        g=norm_g.reshape(depth, 1, D_MODEL), win=w_in.astype(BF16), conv_w=conv_w,
        pool_w=pool_w.astype(BF16), pool_scale=pool_scale.reshape(depth, 1, D_MODEL),
        wbc=w_br_conv.astype(BF16), wbp=w_br_pool.astype(BF16), wba=w_br_att.astype(BF16),
        wout=w_out.astype(BF16), fg=final_norm_g.reshape(1, D_MODEL))

    xp, xs = x_prompt, x_sample
    cv_p, pl_p, cv_s, pl_s = [], [], [], []
    for l in range(depth):
        final = l == depth - 1
        ac_s, ap_s, q, sg, u_t, hp_t = _sample_in(
            jnp.transpose(xs, (1, 0, 2)), jnp.transpose(state_conv[l], (1, 0, 2)),
            jnp.transpose(state_pool[l], (1, 0, 2)), sw, l)
        cv_s.append(jnp.transpose(u_t[n_t - (CONV_WIDTH - 1):], (1, 0, 2)))
        pl_s.append(jnp.concatenate([state_pool[l][:, n_t:], jnp.transpose(hp_t, (1, 0, 2))], axis=1))

        ac_p, ap_p, aa_p, aa_s, c_new, p_new = _prompt_in(
            xp, k_b, v_b, _to_batch_major(q, n_t, n_b), _to_batch_major(sg, n_t, n_b),
            cache_mem_k, cache_mem_v, sw, l)
        cv_p.append(c_new)
        pl_p.append(p_new)

        flat = lambda a: a.reshape(rows_p, D_MODEL)
        xp = _out_side(flat(xp), flat(ac_p), flat(ap_p), flat(aa_p), sw, l, final,
                       "prompt_out").reshape(b_p, t_p, D_MODEL)
        xs = _out_side(
            xs.reshape(rows_s, D_MODEL),
            _to_batch_major(ac_s, n_t, n_b).reshape(rows_s, D_MODEL),
            _to_batch_major(ap_s, n_t, n_b).reshape(rows_s, D_MODEL),
            aa_s.reshape(rows_s, D_MODEL), sw, l, final, "sample_out").reshape(n_b, n_t, D_MODEL)

    return (xp, xs, k_f, v_f, jnp.stack(cv_p), jnp.stack(pl_p), jnp.stack(cv_s), jnp.stack(pl_s))
```

```python
import functools

import jax
import jax.numpy as jnp
from jax import lax
from jax.experimental import pallas as pl
from jax.experimental.pallas import tpu as pltpu

D_MODEL = 1024
N_MEM = 256
HEADS = 4
HEAD_DIM = 256
CONV_WIDTH = 3
POOL_WINDOWS = (2, 4, 8, 16)
POOL_GROUP = 256
POOL_STATE = 15
PAST_LEN = 16384
EPS = 1e-6
OFF_HC, OFF_BC, OFF_CC, OFF_GC = 0, 1024, 2048, 3072
OFF_HP, OFF_GP = 4096, 5120
OFF_Q = 6144
D_IN_A = 7168
GA_BLOCK = 7
MERGE_BLOCKS = (8, 9, 10)

CW = 256
N_CHUNKS = D_MODEL // CW
assert CW == POOL_GROUP == HEAD_DIM
TM = 512
TM_OUT = 512
POOL_PAD = 32
CONV_PAD = 8
VMEM_BYTES_V7X = 64 * 1024 * 1024
VMEM_LIMIT = VMEM_BYTES_V7X - 2 * 1024 * 1024

F32 = jnp.float32
BF16 = jnp.bfloat16


def _dot(a, b):
    return jnp.dot(a, b, preferred_element_type=F32)


def _dot_nt(a, b):
    return lax.dot_general(a, b, (((1,), (1,)), ((), ())), preferred_element_type=F32)


def _sigmoid(x):
    return 0.5 * jnp.tanh(0.5 * x) + 0.5


def _silu(x):
    return x * _sigmoid(x)


def _rmsnorm(x, g):
    ms = jnp.mean(x * x, axis=-1, keepdims=True)
    return (x * lax.rsqrt(ms + EPS)) * g


def _softmax(s):
    e = jnp.exp(s - jnp.max(s, axis=-1, keepdims=True))
    return e * (1.0 / jnp.sum(e, axis=-1, keepdims=True))


def _cols(off, c, w=CW):
    return slice(off + c * w, off + (c + 1) * w)


def _params(n_axes):
    return pltpu.CompilerParams(
        dimension_semantics=("arbitrary",) * n_axes, vmem_limit_bytes=VMEM_LIMIT)


def _const_spec(shape):
    nd = len(shape)
    return pl.BlockSpec(shape, lambda *_: (0,) * nd, pipeline_mode=pl.Buffered(1))


def _layer_spec(arr, layer, cols=None, col_block=0):
    tail = list(arr.shape[1:])
    idx = [0] * len(tail)
    if cols is not None:
        tail[-1] = cols
        idx[-1] = col_block
    return pl.BlockSpec((None, *tail), lambda *_: (layer, *idx), pipeline_mode=pl.Buffered(1))


IN_SIDE_WEIGHTS = ("g", "win", "win", "conv_w", "pool_w", "pool_scale")


def _in_side_specs(sw, layer):
    return [_layer_spec(sw["g"], layer), _layer_spec(sw["win"], layer, D_IN_A, 0),
            _layer_spec(sw["win"], layer, D_MODEL, GA_BLOCK), _layer_spec(sw["conv_w"], layer),
            _layer_spec(sw["pool_w"], layer), _layer_spec(sw["pool_scale"], layer)]


def _kv_kernel(mem_ref, g_ref, w_ref, k_ref, v_ref, kb_ref, vb_ref):
    h = _rmsnorm(mem_ref[...], g_ref[0]).astype(BF16)
    k = _dot(h, w_ref[0, :, :D_MODEL])
    v = _dot(h, w_ref[0, :, D_MODEL:])
    k_ref[0] = k.reshape(k_ref.shape[1:])
    v_ref[0] = v.reshape(v_ref.shape[1:])
    kb_ref[0] = k.astype(BF16)
    vb_ref[0] = v.astype(BF16)


def _mem_kv(mem2d, mem_norm_g, w_kv_bf16):
    depth = w_kv_bf16.shape[0]
    rows = mem2d.shape[0]
    bt = 2
    rt = bt * N_MEM
    out_f = jax.ShapeDtypeStruct((depth, rows // N_MEM, N_MEM, HEADS, HEAD_DIM), F32)
    out_b = jax.ShapeDtypeStruct((depth, rows, D_MODEL), BF16)
    blk = pl.BlockSpec((1, rt, D_MODEL), lambda l, r: (l, r, 0))
    blk5 = pl.BlockSpec((1, bt, N_MEM, HEADS, HEAD_DIM), lambda l, r: (l, r, 0, 0, 0))
    return pl.pallas_call(
        _kv_kernel,
        grid=(depth, rows // rt),
        in_specs=[
            pl.BlockSpec((rt, D_MODEL), lambda l, r: (r, 0)),
            pl.BlockSpec((1, 1, D_MODEL), lambda l, r: (l, 0, 0)),
            pl.BlockSpec((1, D_MODEL, 2 * D_MODEL), lambda l, r: (l, 0, 0)),
        ],
        out_specs=[blk5, blk5, blk, blk],
        out_shape=[out_f, out_f, out_b, out_b],
        compiler_params=_params(2),
        name="mem_kv",
    )(mem2d, mem_norm_g.reshape(depth, 1, D_MODEL), w_kv_bf16)


KV_TILE = CW // HEADS
N_KV_TILES = N_MEM // KV_TILE


def _sample_attention_tiles(sq_ref, ssg_ref, sk_ref, sv_ref, saa_ref, sb, n_t):
    n_flat = N_MEM * HEADS
    col_head = lax.broadcasted_iota(jnp.int32, (HEADS * n_t, n_flat), 1) % HEADS
    row_head = lax.broadcasted_iota(jnp.int32, (HEADS * n_t, n_flat), 0) // n_t
    own = col_head == row_head
    for i in range(sb):
        rows = slice(i * n_t, (i + 1) * n_t)
        qb = sq_ref[rows, :]
        qe = jnp.concatenate([qb[:, _cols(0, hh, HEAD_DIM)] for hh in range(HEADS)], axis=0).astype(BF16)
        s = []
        for j in range(N_KV_TILES):
            kf = sk_ref[0, i, j * KV_TILE:(j + 1) * KV_TILE].reshape(CW, HEAD_DIM).astype(BF16)
            s.append(_dot_nt(qe, kf))
            yield
        s = jnp.where(own, jnp.concatenate(s, axis=1) * HEAD_DIM ** -0.5, -jnp.inf)
        p = _softmax(s).astype(BF16)
        o = None
        for j in range(N_KV_TILES):
            vf = sv_ref[0, i, j * KV_TILE:(j + 1) * KV_TILE].reshape(CW, HEAD_DIM).astype(BF16)
            part = _dot(p[:, _cols(0, j)], vf)
            o = part if o is None else o + part
            yield
        o = jnp.concatenate([o[hh * n_t:(hh + 1) * n_t] for hh in range(HEADS)], axis=1)
        saa_ref[rows, :] = o * ssg_ref[rows, :]


def _prompt_in_kernel(x_ref, k_ref, v_ref, sq_ref, ssg_ref, sk_ref, sv_ref,
                      g_ref, wina_ref, wga_ref, convw_ref, poolw_ref, pscale_ref,
                      ac_ref, ap_ref, aa_ref, saa_ref, cst_ref, pst_ref,
                      u_s, p_s, sa_s, sb_s, *, sb, n_t):
    t = pl.program_id(1)

    @pl.when(t == 0)
    def _():
        u_s[0:CONV_PAD, :] = jnp.zeros((CONV_PAD, D_MODEL), F32)
        p_s[0:POOL_PAD, :] = jnp.zeros((POOL_PAD, D_MODEL), F32)
        sa_s[0:16, :] = jnp.zeros((16, POOL_GROUP), F32)
        sb_s[0:16, :] = jnp.zeros((16, POOL_GROUP), F32)

    sample_attn = _sample_attention_tiles(sq_ref, ssg_ref, sk_ref, sv_ref, saa_ref, sb, n_t)
    for _ in range(N_KV_TILES):
        next(sample_attn, None)
    x = x_ref[0]
    h = _rmsnorm(x, g_ref[...]).astype(BF16)

    def hdot(w_ref, off, c):
        r = _dot(h, w_ref[:, _cols(off, c)])
        next(sample_attn, None)
        return r

    pos1 = (t * TM + 1 + lax.broadcasted_iota(jnp.int32, (TM, 1), 0)).astype(F32)
    n = TM + 16
    for c, w in enumerate(POOL_WINDOWS):
        sl = _cols(0, c)

        hp = hdot(wina_ref, OFF_HP, c)
        p_s[POOL_PAD:POOL_PAD + TM, sl] = hp
        cur = p_s[16:16 + n, sl] + p_s[15:15 + n, sl]
        shift, src, dst = 2, sa_s, sb_s
        while shift < w:
            src[16:16 + n, :] = cur
            cur = src[16:16 + n, :] + src[16 - shift:16 - shift + n, :]
            shift *= 2
            src, dst = dst, src
        inv_cnt = 1.0 / jnp.minimum(pos1, float(w))
        mixed = (cur[16:, :] * inv_cnt - hp).astype(BF16)

        q = hdot(wina_ref, OFF_Q, c).astype(BF16)

        hc = hdot(wina_ref, OFF_HC, c)
        cc = hdot(wina_ref, OFF_CC, c)
        u = cc * hc
        u_s[CONV_PAD:CONV_PAD + TM, sl] = u
        u1 = u_s[CONV_PAD - 1:CONV_PAD - 1 + TM, sl]
        u2 = u_s[CONV_PAD - 2:CONV_PAD - 2 + TM, sl]
        y = convw_ref[0:1, sl] * u2 + convw_ref[1:2, sl] * u1 + convw_ref[2:3, sl] * u

        p = _softmax(_dot_nt(q, k_ref[:, sl]) * HEAD_DIM ** -0.5).astype(BF16)

        bc = hdot(wina_ref, OFF_BC, c)
        gc = hdot(wina_ref, OFF_GC, c)
        ac_ref[0, :, sl] = (bc * y * _silu(gc)).astype(BF16)

        gp = hdot(wina_ref, OFF_GP, c)
        pooled = _dot(mixed, poolw_ref[c]) * pscale_ref[:, sl]
        ap_ref[0, :, sl] = (pooled * _silu(gp)).astype(BF16)

        ga = hdot(wga_ref, 0, c)
        o = _dot(p, v_ref[:, sl])
        aa_ref[0, :, sl] = (o * _silu(ga)).astype(BF16)

    new_conv = u_s[CONV_PAD + TM - 2:CONV_PAD + TM, :]
    cst_ref[0] = new_conv
    u_s[CONV_PAD - 2:CONV_PAD, :] = new_conv
    pst_ref[0] = p_s[POOL_PAD + TM - POOL_STATE:POOL_PAD + TM, :]
    p_s[16:POOL_PAD, :] = p_s[TM + 16:TM + POOL_PAD, :]

    for _ in sample_attn:
        pass


def _prompt_in(x, kb, vb, sq, ssg, cache_k, cache_v, sw, layer, n_t):
    b, t, _ = x.shape
    nt = t // TM
    n_b = sq.shape[0] // n_t
    sb = n_b // (b * nt)
    assert sb * b * nt == n_b
    tile = pl.BlockSpec((1, TM, D_MODEL), lambda i, j: (i, j, 0))
    smp = pl.BlockSpec((sb * n_t, D_MODEL), lambda i, j: (i * nt + j, 0))
    skv = pl.BlockSpec((1, sb, N_MEM, HEADS, HEAD_DIM), lambda i, j: (layer, i * nt + j, 0, 0, 0))
    act = jax.ShapeDtypeStruct((b, t, D_MODEL), BF16)
    return pl.pallas_call(
        functools.partial(_prompt_in_kernel, sb=sb, n_t=n_t),
        grid=(b, nt),
        in_specs=[
            tile,
            pl.BlockSpec((None, N_MEM, D_MODEL), lambda i, j: (layer, i, 0)),
            pl.BlockSpec((None, N_MEM, D_MODEL), lambda i, j: (layer, i, 0)),
            smp, smp, skv, skv,
        ] + _in_side_specs(sw, layer),
        out_specs=[
            tile, tile, tile, smp,
            pl.BlockSpec((1, CONV_WIDTH - 1, D_MODEL), lambda i, j: (i, 0, 0)),
            pl.BlockSpec((1, POOL_STATE, D_MODEL), lambda i, j: (i, 0, 0)),
        ],
        out_shape=[
            act, act, act,
            jax.ShapeDtypeStruct(sq.shape, F32),
            jax.ShapeDtypeStruct((b, CONV_WIDTH - 1, D_MODEL), F32),
            jax.ShapeDtypeStruct((b, POOL_STATE, D_MODEL), F32),
        ],
        scratch_shapes=[
            pltpu.VMEM((CONV_PAD + TM, D_MODEL), F32),
            pltpu.VMEM((POOL_PAD + TM, D_MODEL), F32),
            pltpu.VMEM((POOL_PAD + TM, POOL_GROUP), F32),
            pltpu.VMEM((POOL_PAD + TM, POOL_GROUP), F32),
        ],
        compiler_params=_params(2),
        name="prompt_in",
    )(x, kb, vb, sq, ssg, cache_k, cache_v, *(sw[n] for n in IN_SIDE_WEIGHTS))


def _sample_in_kernel(x_ref, tb_ref, bt_ref, cst_ref, pst_ref, g_ref,
                      whp_ref, wq_ref, whc_ref, wcc_ref, wbc_ref, wgc_ref, wgp_ref, wga_ref,
                      convw_ref, poolw_ref, pscale_ref,
                      ac_ref, ap_ref, q_ref, sg_ref, u_ref, hp_ref, hb_s, ht_s, *, n_t, n_b):
    c = pl.program_id(0)

    @pl.when(c == 0)
    def _():
        hb = _rmsnorm(x_ref[...], g_ref[...]).astype(BF16)
        hb_s[...] = hb
        ht_s[...] = _dot(tb_ref[...], hb).astype(BF16)

    hb, ht = hb_s[...], ht_s[...]

    def proj(h, w_ref):
        return _dot(h, w_ref[...].astype(BF16))

    def to_batch_major(a):
        return _dot(bt_ref[...], a.astype(BF16)).astype(BF16)

    u = proj(ht, wcc_ref) * proj(ht, whc_ref)
    u_ref[...] = u.reshape(n_t, n_b, CW)
    ext = [cst_ref[0], cst_ref[1]] + [u[i * n_b:(i + 1) * n_b] for i in range(n_t)]
    y = jnp.concatenate(
        [convw_ref[0:1, :] * ext[i] + convw_ref[1:2, :] * ext[i + 1] + convw_ref[2:3, :] * ext[i + 2]
         for i in range(n_t)], axis=0)
    ac_ref[...] = to_batch_major(proj(ht, wbc_ref) * y * _silu(proj(ht, wgc_ref)))

    hp = proj(ht, whp_ref)
    hp_ref[...] = hp.reshape(n_t, n_b, POOL_GROUP)
    ext = [pst_ref[j] for j in range(POOL_STATE)] + [hp[i * n_b:(i + 1) * n_b] for i in range(n_t)]
    for g, w in enumerate(POOL_WINDOWS):
        @pl.when(c == g)
        def _(w=w):
            mixed = []
            for i in range(n_t):
                cnt = float(min(PAST_LEN + i + 1, w))
                acc = ext[POOL_STATE + i]
                for j in range(1, w):
                    acc = acc + ext[POOL_STATE + i - j]
                mixed.append(acc * (1.0 / cnt) - ext[POOL_STATE + i])
            mixed = jnp.concatenate(mixed, axis=0).astype(BF16)
            pooled = _dot(mixed, poolw_ref[...].astype(BF16)) * pscale_ref[...]
            ap_ref[...] = to_batch_major(pooled * _silu(proj(ht, wgp_ref)))

    q_ref[...] = proj(hb, wq_ref)
    sg_ref[...] = _silu(proj(hb, wga_ref))


def _row_permutations(n_t, n_b):
    r = jnp.arange(n_t * n_b)
    tb = jax.nn.one_hot((r % n_b) * n_t + r // n_b, n_t * n_b, dtype=BF16)
    return tb, tb.T


def _sample_in(xs2d, cst_t, pst_t, g, w_in, conv_w, pool_w, pool_scale, layer, n_t, n_b):
    rows = n_t * n_b
    tb, bt = _row_permutations(n_t, n_b)
    col_blocks = [off // CW for off in (OFF_HP, OFF_Q, OFF_HC, OFF_CC, OFF_BC, OFF_GC, OFF_GP, GA_BLOCK * D_MODEL)]
    w_specs = [pl.BlockSpec((None, D_MODEL, CW), functools.partial(lambda c, blk: (layer, 0, blk + c), blk=blk))
               for blk in col_blocks]
    chunk2d = pl.BlockSpec((rows, CW), lambda c: (0, c))
    chunk3d = pl.BlockSpec((n_t, n_b, CW), lambda c: (0, 0, c))
    return pl.pallas_call(
        functools.partial(_sample_in_kernel, n_t=n_t, n_b=n_b),
        grid=(N_CHUNKS,),
        in_specs=[
            _const_spec(xs2d.shape), _const_spec(tb.shape), _const_spec(bt.shape),
            pl.BlockSpec((None, CONV_WIDTH - 1, n_b, CW), lambda c: (layer, 0, 0, c)),
            pl.BlockSpec((None, POOL_STATE, n_b, CW), lambda c: (layer, 0, 0, c)),
            _layer_spec(g, layer),
        ] + w_specs + [
            pl.BlockSpec((None, CONV_WIDTH, CW), lambda c: (layer, 0, c)),
            pl.BlockSpec((None, None, POOL_GROUP, POOL_GROUP), lambda c: (layer, c, 0, 0)),
            pl.BlockSpec((None, 1, CW), lambda c: (layer, 0, c)),
        ],
        out_specs=[chunk2d] * 4 + [chunk3d] * 2,
        out_shape=[
            jax.ShapeDtypeStruct((rows, D_MODEL), BF16),
            jax.ShapeDtypeStruct((rows, D_MODEL), BF16),
            jax.ShapeDtypeStruct((rows, D_MODEL), F32),
            jax.ShapeDtypeStruct((rows, D_MODEL), F32),
            jax.ShapeDtypeStruct((n_t, n_b, D_MODEL), F32),
            jax.ShapeDtypeStruct((n_t, n_b, D_MODEL), F32),
        ],
        scratch_shapes=[pltpu.VMEM((rows, D_MODEL), BF16), pltpu.VMEM((rows, D_MODEL), BF16)],
        compiler_params=_params(1),
        name="sample_in",
    )(xs2d, tb, bt, cst_t, pst_t, g, *([w_in] * len(col_blocks)), conv_w, pool_w, pool_scale)


def _out_side_kernel(x_ref, ac_ref, ap_ref, aa_ref, xs_ref, acs_ref, aps_ref, aas_ref,
                     g_ref, wmc_ref, wmp_ref, wma_ref, wbc_ref, wbp_ref, wba_ref, wout_ref, fg_ref,
                     y_ref, ys_ref, m_s, *, final_norm, n_prompt_tiles):
    def tile(x_ref, ac_ref, ap_ref, aa_ref, y_ref):
        rows = x_ref.shape[0]
        x = x_ref[...]
        h = _rmsnorm(x, g_ref[...]).astype(BF16)
        ac, ap, aa = ac_ref[...], ap_ref[...], aa_ref[...].astype(BF16)
        for c in range(N_CHUNKS):
            sl = _cols(0, c)
            conv_br = _dot(ac, wbc_ref[:, sl])
            pool_br = _dot(ap, wbp_ref[:, sl])
            att_br = _dot(aa, wba_ref[:, sl])
            mc = _dot(h, wmc_ref[:, sl])
            mp = _dot(h, wmp_ref[:, sl])
            ma = _dot(h, wma_ref[:, sl])
            merged = _sigmoid(mc) * conv_br + _sigmoid(mp) * pool_br + _sigmoid(ma) * att_br
            m_s[0:rows, sl] = merged.astype(BF16)
        xn = x + _dot(m_s[0:rows, :], wout_ref[...])
        if final_norm:
            xn = _rmsnorm(xn, fg_ref[...])
        y_ref[...] = xn

    r = pl.program_id(0)

    @pl.when(r < n_prompt_tiles)
    def _():
        tile(x_ref, ac_ref, ap_ref, aa_ref, y_ref)

    @pl.when(r == n_prompt_tiles)
    def _():
        tile(xs_ref, acs_ref, aps_ref, aas_ref, ys_ref)


def _out_side(x2d, ac, ap, aa, xs2d, acs, aps, aas, sw, layer, final_norm):
    rows, rows_s = x2d.shape[0], xs2d.shape[0]
    n_tiles = rows // TM_OUT
    assert n_tiles * TM_OUT == rows and rows_s <= TM_OUT
    tile = pl.BlockSpec((TM_OUT, D_MODEL), lambda r: (jnp.minimum(r, n_tiles - 1), 0))
    rest = ("wbc", "wbp", "wba", "wout")
    weights = (sw["g"],) + (sw["win"],) * 3 + tuple(sw[n] for n in rest) + (sw["fg"],)
    w_specs = ([_layer_spec(sw["g"], layer)]
               + [_layer_spec(sw["win"], layer, D_MODEL, blk) for blk in MERGE_BLOCKS]
               + [_layer_spec(sw[n], layer) for n in rest] + [_const_spec(sw["fg"].shape)])
    smp = _const_spec((rows_s, D_MODEL))
    return pl.pallas_call(
        functools.partial(_out_side_kernel, final_norm=final_norm, n_prompt_tiles=n_tiles),
        grid=(n_tiles + 1,),
        in_specs=[tile] * 4 + [smp] * 4 + w_specs,
        out_specs=[tile, smp],
        out_shape=[jax.ShapeDtypeStruct((rows, D_MODEL), F32), jax.ShapeDtypeStruct((rows_s, D_MODEL), F32)],
        scratch_shapes=[pltpu.VMEM((TM_OUT, D_MODEL), BF16)],
        compiler_params=_params(1),
        name="out_side",
    )(x2d, ac, ap, aa, xs2d, acs, aps, aas, *weights)


def kernel(x_prompt, x_sample, mem_prompt, cache_mem_k, cache_mem_v, state_conv, state_pool, norm_g, w_in, conv_w, pool_w, pool_scale, mem_norm_g, w_mem_kv, w_br_conv, w_br_pool, w_br_att, w_out, final_norm_g):
    depth = w_in.shape[0]
    b_p, t_p, _ = x_prompt.shape
    n_b, n_t, _ = x_sample.shape
    rows_p, rows_s = b_p * t_p, n_b * n_t

    k_f, v_f, k_b, v_b = _mem_kv(mem_prompt.reshape(b_p * N_MEM, D_MODEL), mem_norm_g, w_mem_kv.astype(BF16))

    sw = dict(
        g=norm_g.reshape(depth, 1, D_MODEL), win=w_in.astype(BF16), conv_w=conv_w,
        pool_w=pool_w.astype(BF16), pool_scale=pool_scale.reshape(depth, 1, D_MODEL),
        wbc=w_br_conv.astype(BF16), wbp=w_br_pool.astype(BF16), wba=w_br_att.astype(BF16),
        wout=w_out.astype(BF16), fg=final_norm_g.reshape(1, D_MODEL))

    cst_t = jnp.transpose(state_conv, (0, 2, 1, 3))
    pst_t = jnp.transpose(state_pool, (0, 2, 1, 3))

    xp, xs = x_prompt, x_sample.reshape(rows_s, D_MODEL)
    cv_p, pl_p, u_s, hp_s = [], [], [], []
    flat = lambda a: a.reshape(rows_p, D_MODEL)
    for l in range(depth):
        ac_s, ap_s, q, sg, u_t, hp_t = _sample_in(
            xs, cst_t, pst_t, sw["g"], w_in, conv_w, pool_w, sw["pool_scale"], l, n_t, n_b)
        u_s.append(u_t)
        hp_s.append(hp_t)

        ac_p, ap_p, aa_p, aa_s, c_new, p_new = _prompt_in(
            xp, k_b, v_b, q, sg, cache_mem_k, cache_mem_v, sw, l, n_t)
        cv_p.append(c_new)
        pl_p.append(p_new)

        xp, xs = _out_side(flat(xp), flat(ac_p), flat(ap_p), flat(aa_p), xs, ac_s, ap_s, aa_s,
                           sw, l, l == depth - 1)
        xp = xp.reshape(b_p, t_p, D_MODEL)

    new_conv_s = jnp.transpose(jnp.stack(u_s)[:, n_t - (CONV_WIDTH - 1):], (0, 2, 1, 3))
    new_pool_s = jnp.concatenate(
        [state_pool[:, :, n_t:], jnp.transpose(jnp.stack(hp_s), (0, 2, 1, 3))], axis=2)
    return (xp, xs.reshape(n_b, n_t, D_MODEL), k_f, v_f, jnp.stack(cv_p), jnp.stack(pl_p),
            new_conv_s, new_pool_s)
```

```python
import functools

import jax
import jax.numpy as jnp
from jax import lax
from jax.experimental import pallas as pl
from jax.experimental.pallas import tpu as pltpu

D_MODEL = 1024
N_MEM = 256
HEADS = 4
HEAD_DIM = 256
CONV_WIDTH = 3
POOL_WINDOWS = (2, 4, 8, 16)
POOL_GROUP = 256
POOL_STATE = 15
PAST_LEN = 16384
EPS = 1e-6
OFF_HC, OFF_BC, OFF_CC, OFF_GC = 0, 1024, 2048, 3072
OFF_HP, OFF_GP = 4096, 5120
OFF_Q = 6144
OFF_GA = 7168
OFF_MERGE = 8192
N_MERGE = 3

CW = 256
N_CHUNKS = D_MODEL // CW
assert CW == POOL_GROUP == HEAD_DIM
TM = 512
TM_OUT = 512
POOL_PAD = 32
CONV_PAD = 8
VMEM_BYTES_V7X = 64 * 1024 * 1024
VMEM_LIMIT = VMEM_BYTES_V7X - 2 * 1024 * 1024

F32 = jnp.float32
BF16 = jnp.bfloat16


def _dot(a, b):
    return jnp.dot(a, b, preferred_element_type=F32)


def _dot_nt(a, b):
    return lax.dot_general(a, b, (((1,), (1,)), ((), ())), preferred_element_type=F32)


def _sigmoid(x):
    return 0.5 * jnp.tanh(0.5 * x) + 0.5


def _silu(x):
    return x * _sigmoid(x)


def _rmsnorm(x, g):
    ms = jnp.mean(x * x, axis=-1, keepdims=True)
    return (x * lax.rsqrt(ms + EPS)) * g


def _softmax(s):
    e = jnp.exp(s - jnp.max(s, axis=-1, keepdims=True))
    return e * (1.0 / jnp.sum(e, axis=-1, keepdims=True))


def _cols(off, c, w=CW):
    return slice(off + c * w, off + (c + 1) * w)


def _params(n_axes):
    return pltpu.CompilerParams(
        dimension_semantics=("arbitrary",) * n_axes, vmem_limit_bytes=VMEM_LIMIT)


def _const_spec(shape):
    nd = len(shape)
    return pl.BlockSpec(shape, lambda *_: (0,) * nd, pipeline_mode=pl.Buffered(1))


def _layer_spec(arr, layer, cols=None, col_block=0):
    tail = list(arr.shape[1:])
    idx = [0] * len(tail)
    if cols is not None:
        tail[-1] = cols
        idx[-1] = col_block
    return pl.BlockSpec((None, *tail), lambda *_: (layer, *idx), pipeline_mode=pl.Buffered(1))


def _kv_kernel(mem_ref, g_ref, w_ref, k_ref, v_ref, kb_ref, vb_ref):
    h = _rmsnorm(mem_ref[...], g_ref[0]).astype(BF16)
    k = _dot(h, w_ref[0, :, :D_MODEL].astype(BF16))
    v = _dot(h, w_ref[0, :, D_MODEL:].astype(BF16))
    k_ref[0] = k.reshape(k_ref.shape[1:])
    v_ref[0] = v.reshape(v_ref.shape[1:])
    kb_ref[0] = k.astype(BF16)
    vb_ref[0] = v.astype(BF16)


def _mem_kv(mem2d, mem_norm_g, w_kv):
    depth = w_kv.shape[0]
    rows = mem2d.shape[0]
    bt = 2
    rt = bt * N_MEM
    out_f = jax.ShapeDtypeStruct((depth, rows // N_MEM, N_MEM, HEADS, HEAD_DIM), F32)
    out_b = jax.ShapeDtypeStruct((depth, rows, D_MODEL), BF16)
    blk = pl.BlockSpec((1, rt, D_MODEL), lambda l, r: (l, r, 0))
    blk5 = pl.BlockSpec((1, bt, N_MEM, HEADS, HEAD_DIM), lambda l, r: (l, r, 0, 0, 0))
    return pl.pallas_call(
        _kv_kernel,
        grid=(depth, rows // rt),
        in_specs=[
            pl.BlockSpec((rt, D_MODEL), lambda l, r: (r, 0)),
            pl.BlockSpec((1, 1, D_MODEL), lambda l, r: (l, 0, 0)),
            pl.BlockSpec((1, D_MODEL, 2 * D_MODEL), lambda l, r: (l, 0, 0)),
        ],
        out_specs=[blk5, blk5, blk, blk],
        out_shape=[out_f, out_f, out_b, out_b],
        compiler_params=_params(2),
        name="mem_kv",
    )(mem2d, mem_norm_g.reshape(depth, 1, D_MODEL), w_kv)


KV_TILE = CW // HEADS
N_KV_TILES = N_MEM // KV_TILE


def _sample_attention_tiles(sq_ref, ssg_ref, sk_ref, sv_ref, saa_ref, sb, n_t):
    n_flat = N_MEM * HEADS
    col_head = lax.broadcasted_iota(jnp.int32, (HEADS * n_t, n_flat), 1) % HEADS
    row_head = lax.broadcasted_iota(jnp.int32, (HEADS * n_t, n_flat), 0) // n_t
    own = col_head == row_head
    for i in range(sb):
        rows = slice(i * n_t, (i + 1) * n_t)
        qb = sq_ref[rows, :]
        qe = jnp.concatenate([qb[:, _cols(0, hh, HEAD_DIM)] for hh in range(HEADS)], axis=0).astype(BF16)
        s = []
        for j in range(N_KV_TILES):
            kf = sk_ref[0, i, j * KV_TILE:(j + 1) * KV_TILE].reshape(CW, HEAD_DIM).astype(BF16)
            s.append(_dot_nt(qe, kf))
            yield
        s = jnp.where(own, jnp.concatenate(s, axis=1) * HEAD_DIM ** -0.5, -jnp.inf)
        p = _softmax(s).astype(BF16)
        o = None
        for j in range(N_KV_TILES):
            vf = sv_ref[0, i, j * KV_TILE:(j + 1) * KV_TILE].reshape(CW, HEAD_DIM).astype(BF16)
            part = _dot(p[:, _cols(0, j)], vf)
            o = part if o is None else o + part
            yield
        o = jnp.concatenate([o[hh * n_t:(hh + 1) * n_t] for hh in range(HEADS)], axis=1)
        saa_ref[rows, :] = o * ssg_ref[rows, :]


def _prompt_in_kernel(x_ref, k_ref, v_ref, sq_ref, ssg_ref, sk_ref, sv_ref,
                      g_ref, whp_ref, wq_ref, whc_ref, wcc_ref, wbc_ref, wgc_ref, wgp_ref, wga_ref,
                      convw_ref, poolw_ref, pscale_ref,
                      ac_ref, ap_ref, aa_ref, saa_ref, cst_ref, pst_ref,
                      u_s, p_s, sa_s, sb_s, *, sb, n_t):
    t = pl.program_id(1)

    @pl.when(t == 0)
    def _():
        u_s[0:CONV_PAD, :] = jnp.zeros((CONV_PAD, D_MODEL), F32)
        p_s[0:POOL_PAD, :] = jnp.zeros((POOL_PAD, D_MODEL), F32)
        sa_s[0:16, :] = jnp.zeros((16, POOL_GROUP), F32)
        sb_s[0:16, :] = jnp.zeros((16, POOL_GROUP), F32)

    sample_attn = _sample_attention_tiles(sq_ref, ssg_ref, sk_ref, sv_ref, saa_ref, sb, n_t)
    for _ in range(N_KV_TILES):
        next(sample_attn, None)
    x = x_ref[0]
    h = _rmsnorm(x, g_ref[...]).astype(BF16)

    def hdot(w_ref, c):
        r = _dot(h, w_ref[:, _cols(0, c)])
        next(sample_attn, None)
        return r

    pos1 = (t * TM + 1 + lax.broadcasted_iota(jnp.int32, (TM, 1), 0)).astype(F32)
    n = TM + 16
    for c, w in enumerate(POOL_WINDOWS):
        sl = _cols(0, c)

        hp = hdot(whp_ref, c)
        p_s[POOL_PAD:POOL_PAD + TM, sl] = hp
        cur = p_s[16:16 + n, sl] + p_s[15:15 + n, sl]
        shift, src, dst = 2, sa_s, sb_s
        while shift < w:
            src[16:16 + n, :] = cur
            cur = src[16:16 + n, :] + src[16 - shift:16 - shift + n, :]
            shift *= 2
            src, dst = dst, src
        inv_cnt = 1.0 / jnp.minimum(pos1, float(w))
        mixed = (cur[16:, :] * inv_cnt - hp).astype(BF16)

        q = hdot(wq_ref, c).astype(BF16)

        hc = hdot(whc_ref, c)
        cc = hdot(wcc_ref, c)
        u = cc * hc
        u_s[CONV_PAD:CONV_PAD + TM, sl] = u
        u1 = u_s[CONV_PAD - 1:CONV_PAD - 1 + TM, sl]
        u2 = u_s[CONV_PAD - 2:CONV_PAD - 2 + TM, sl]
        y = convw_ref[0:1, sl] * u2 + convw_ref[1:2, sl] * u1 + convw_ref[2:3, sl] * u

        p = _softmax(_dot_nt(q, k_ref[:, sl]) * HEAD_DIM ** -0.5).astype(BF16)

        bc = hdot(wbc_ref, c)
        gc = hdot(wgc_ref, c)
        ac_ref[0, :, sl] = (bc * y * _silu(gc)).astype(BF16)

        gp = hdot(wgp_ref, c)
        pooled = _dot(mixed, poolw_ref[c]) * pscale_ref[:, sl]
        ap_ref[0, :, sl] = (pooled * _silu(gp)).astype(BF16)

        ga = hdot(wga_ref, c)
        o = _dot(p, v_ref[:, sl])
        aa_ref[0, :, sl] = (o * _silu(ga)).astype(BF16)

    new_conv = u_s[CONV_PAD + TM - 2:CONV_PAD + TM, :]
    cst_ref[0] = new_conv
    u_s[CONV_PAD - 2:CONV_PAD, :] = new_conv
    pst_ref[0] = p_s[POOL_PAD + TM - POOL_STATE:POOL_PAD + TM, :]
    p_s[16:POOL_PAD, :] = p_s[TM + 16:TM + POOL_PAD, :]

    for _ in sample_attn:
        pass


def _prompt_in(x, kb, vb, sq, ssg, cache_k, cache_v, w_a, sw, layer, n_t):
    b, t, _ = x.shape
    nt = t // TM
    n_b = sq.shape[0] // n_t
    sb = n_b // (b * nt)
    assert sb * b * nt == n_b
    tile = pl.BlockSpec((1, TM, D_MODEL), lambda i, j: (i, j, 0))
    smp = pl.BlockSpec((sb * n_t, D_MODEL), lambda i, j: (i * nt + j, 0))
    skv = pl.BlockSpec((1, sb, N_MEM, HEADS, HEAD_DIM), lambda i, j: (layer, i * nt + j, 0, 0, 0))
    act = jax.ShapeDtypeStruct((b, t, D_MODEL), BF16)
    return pl.pallas_call(
        functools.partial(_prompt_in_kernel, sb=sb, n_t=n_t),
        grid=(b, nt),
        in_specs=[
            tile,
            pl.BlockSpec((None, N_MEM, D_MODEL), lambda i, j: (layer, i, 0)),
            pl.BlockSpec((None, N_MEM, D_MODEL), lambda i, j: (layer, i, 0)),
            smp, smp, skv, skv,
            _layer_spec(sw["g"], layer),
        ] + [_const_spec(w.shape) for w in w_a] + [
            _layer_spec(sw["conv_w"], layer), _layer_spec(sw["pool_w"], layer), _layer_spec(sw["pool_scale"], layer),
        ],
        out_specs=[
            tile, tile, tile, smp,
            pl.BlockSpec((1, CONV_WIDTH - 1, D_MODEL), lambda i, j: (i, 0, 0)),
            pl.BlockSpec((1, POOL_STATE, D_MODEL), lambda i, j: (i, 0, 0)),
        ],
        out_shape=[
            act, act, act,
            jax.ShapeDtypeStruct(sq.shape, F32),
            jax.ShapeDtypeStruct((b, CONV_WIDTH - 1, D_MODEL), F32),
            jax.ShapeDtypeStruct((b, POOL_STATE, D_MODEL), F32),
        ],
        scratch_shapes=[
            pltpu.VMEM((CONV_PAD + TM, D_MODEL), F32),
            pltpu.VMEM((POOL_PAD + TM, D_MODEL), F32),
            pltpu.VMEM((POOL_PAD + TM, POOL_GROUP), F32),
            pltpu.VMEM((POOL_PAD + TM, POOL_GROUP), F32),
        ],
        compiler_params=_params(2),
        name="prompt_in",
    )(x, kb, vb, sq, ssg, cache_k, cache_v, sw["g"], *w_a, sw["conv_w"], sw["pool_w"], sw["pool_scale"])


def _sample_in_kernel(x_ref, tb_ref, bt_ref, cst_ref, pst_ref, g_ref,
                      whp_ref, wq_ref, whc_ref, wcc_ref, wbc_ref, wgc_ref, wgp_ref, wga_ref,
                      convw_ref, poolw_ref, pscale_ref,
                      ac_ref, ap_ref, q_ref, sg_ref, u_ref, hp_ref,
                      bhp_ref, bq_ref, bhc_ref, bcc_ref, bbc_ref, bgc_ref, bgp_ref, bga_ref,
                      hb_s, ht_s, *, n_t, n_b):
    c = pl.program_id(0)

    @pl.when(c == 0)
    def _():
        hb = _rmsnorm(x_ref[...], g_ref[...]).astype(BF16)
        hb_s[...] = hb
        ht_s[...] = _dot(tb_ref[...], hb).astype(BF16)

    hb, ht = hb_s[...], ht_s[...]

    def proj(h, w_ref, wb_ref):
        w = w_ref[...].astype(BF16)
        wb_ref[...] = w
        return _dot(h, w)

    def to_batch_major(a):
        return _dot(bt_ref[...], a.astype(BF16)).astype(BF16)

    u = proj(ht, wcc_ref, bcc_ref) * proj(ht, whc_ref, bhc_ref)
    u_ref[...] = u.reshape(n_t, n_b, CW)
    ext = [cst_ref[0], cst_ref[1]] + [u[i * n_b:(i + 1) * n_b] for i in range(n_t)]
    y = jnp.concatenate(
        [convw_ref[0:1, :] * ext[i] + convw_ref[1:2, :] * ext[i + 1] + convw_ref[2:3, :] * ext[i + 2]
         for i in range(n_t)], axis=0)
    ac_ref[...] = to_batch_major(proj(ht, wbc_ref, bbc_ref) * y * _silu(proj(ht, wgc_ref, bgc_ref)))

    hp = proj(ht, whp_ref, bhp_ref)
    silu_gp = _silu(proj(ht, wgp_ref, bgp_ref))
    hp_ref[...] = hp.reshape(n_t, n_b, POOL_GROUP)
    ext = [pst_ref[j] for j in range(POOL_STATE)] + [hp[i * n_b:(i + 1) * n_b] for i in range(n_t)]
    for g, w in enumerate(POOL_WINDOWS):
        @pl.when(c == g)
        def _(w=w):
            mixed = []
            for i in range(n_t):
                cnt = float(min(PAST_LEN + i + 1, w))
                acc = ext[POOL_STATE + i]
                for j in range(1, w):
                    acc = acc + ext[POOL_STATE + i - j]
                mixed.append(acc * (1.0 / cnt) - ext[POOL_STATE + i])
            mixed = jnp.concatenate(mixed, axis=0).astype(BF16)
            pooled = _dot(mixed, poolw_ref[...].astype(BF16)) * pscale_ref[...]
            ap_ref[...] = to_batch_major(pooled * silu_gp)

    q_ref[...] = proj(hb, wq_ref, bq_ref)
    sg_ref[...] = _silu(proj(hb, wga_ref, bga_ref))


def _row_permutations(n_t, n_b):
    r = jnp.arange(n_t * n_b)
    tb = jax.nn.one_hot((r % n_b) * n_t + r // n_b, n_t * n_b, dtype=BF16)
    return tb, tb.T


def _sample_in(xs2d, cst_t, pst_t, g, w_in, conv_w, pool_w, pool_scale, layer, n_t, n_b):
    rows = n_t * n_b
    tb, bt = _row_permutations(n_t, n_b)
    col_blocks = [off // CW for off in (OFF_HP, OFF_Q, OFF_HC, OFF_CC, OFF_BC, OFF_GC, OFF_GP, OFF_GA)]
    w_specs = [pl.BlockSpec((None, D_MODEL, CW), functools.partial(lambda c, blk: (layer, 0, blk + c), blk=blk))
               for blk in col_blocks]
    chunk2d = pl.BlockSpec((rows, CW), lambda c: (0, c))
    chunk3d = pl.BlockSpec((n_t, n_b, CW), lambda c: (0, 0, c))
    return pl.pallas_call(
        functools.partial(_sample_in_kernel, n_t=n_t, n_b=n_b),
        grid=(N_CHUNKS,),
        in_specs=[
            _const_spec(xs2d.shape), _const_spec(tb.shape), _const_spec(bt.shape),
            pl.BlockSpec((None, CONV_WIDTH - 1, n_b, CW), lambda c: (layer, 0, 0, c)),
            pl.BlockSpec((None, POOL_STATE, n_b, CW), lambda c: (layer, 0, 0, c)),
            _layer_spec(g, layer),
        ] + w_specs + [
            pl.BlockSpec((None, CONV_WIDTH, CW), lambda c: (layer, 0, c)),
            pl.BlockSpec((None, None, POOL_GROUP, POOL_GROUP), lambda c: (layer, c, 0, 0)),
            pl.BlockSpec((None, 1, CW), lambda c: (layer, 0, c)),
        ],
        out_specs=[chunk2d] * 4 + [chunk3d] * 2 + [pl.BlockSpec((D_MODEL, CW), lambda c: (0, c))] * len(col_blocks),
        out_shape=[
            jax.ShapeDtypeStruct((rows, D_MODEL), BF16),
            jax.ShapeDtypeStruct((rows, D_MODEL), BF16),
            jax.ShapeDtypeStruct((rows, D_MODEL), F32),
            jax.ShapeDtypeStruct((rows, D_MODEL), F32),
            jax.ShapeDtypeStruct((n_t, n_b, D_MODEL), F32),
            jax.ShapeDtypeStruct((n_t, n_b, D_MODEL), F32),
        ] + [jax.ShapeDtypeStruct((D_MODEL, D_MODEL), BF16)] * len(col_blocks),
        scratch_shapes=[pltpu.VMEM((rows, D_MODEL), BF16), pltpu.VMEM((rows, D_MODEL), BF16)],
        compiler_params=_params(1),
        name="sample_in",
    )(xs2d, tb, bt, cst_t, pst_t, g, *([w_in] * len(col_blocks)), conv_w, pool_w, pool_scale)


def _out_side_kernel(x_ref, ac_ref, ap_ref, aa_ref, xs_ref, acs_ref, aps_ref, aas_ref,
                     g_ref, wmc_ref, wmp_ref, wma_ref, wbc_ref, wbp_ref, wba_ref, wout_ref, fg_ref,
                     y_ref, ys_ref, m_s, *, final_norm, n_prompt_tiles):
    def tile(x_ref, ac_ref, ap_ref, aa_ref, y_ref):
        rows = x_ref.shape[0]
        x = x_ref[...]
        h = _rmsnorm(x, g_ref[...]).astype(BF16)
        ac, ap, aa = ac_ref[...], ap_ref[...], aa_ref[...].astype(BF16)
        for c in range(N_CHUNKS):
            sl = _cols(0, c)
            conv_br = _dot(ac, wbc_ref[:, sl])
            pool_br = _dot(ap, wbp_ref[:, sl])
            att_br = _dot(aa, wba_ref[:, sl])
            mc = _dot(h, wmc_ref[:, sl])
            mp = _dot(h, wmp_ref[:, sl])
            ma = _dot(h, wma_ref[:, sl])
            merged = _sigmoid(mc) * conv_br + _sigmoid(mp) * pool_br + _sigmoid(ma) * att_br
            m_s[0:rows, sl] = merged.astype(BF16)
        xn = x + _dot(m_s[0:rows, :], wout_ref[...])
        if final_norm:
            xn = _rmsnorm(xn, fg_ref[...])
        y_ref[...] = xn

    r = pl.program_id(0)

    @pl.when(r < n_prompt_tiles)
    def _():
        tile(x_ref, ac_ref, ap_ref, aa_ref, y_ref)

    @pl.when(r == n_prompt_tiles)
    def _():
        tile(xs_ref, acs_ref, aps_ref, aas_ref, ys_ref)


def _out_side(x2d, ac, ap, aa, xs2d, acs, aps, aas, sw, layer, final_norm):
    rows, rows_s = x2d.shape[0], xs2d.shape[0]
    n_tiles = rows // TM_OUT
    assert n_tiles * TM_OUT == rows and rows_s <= TM_OUT
    tile = pl.BlockSpec((TM_OUT, D_MODEL), lambda r: (jnp.minimum(r, n_tiles - 1), 0))
    rest = ("wbc", "wbp", "wba", "wout")
    weights = (sw["g"],) + (sw["wmerge"],) * N_MERGE + tuple(sw[n] for n in rest) + (sw["fg"],)
    w_specs = ([_layer_spec(sw["g"], layer)]
               + [_layer_spec(sw["wmerge"], layer, D_MODEL, blk) for blk in range(N_MERGE)]
               + [_layer_spec(sw[n], layer) for n in rest] + [_const_spec(sw["fg"].shape)])
    smp = _const_spec((rows_s, D_MODEL))
    return pl.pallas_call(
        functools.partial(_out_side_kernel, final_norm=final_norm, n_prompt_tiles=n_tiles),
        grid=(n_tiles + 1,),
        in_specs=[tile] * 4 + [smp] * 4 + w_specs,
        out_specs=[tile, smp],
        out_shape=[jax.ShapeDtypeStruct((rows, D_MODEL), F32), jax.ShapeDtypeStruct((rows_s, D_MODEL), F32)],
        scratch_shapes=[pltpu.VMEM((TM_OUT, D_MODEL), BF16)],
        compiler_params=_params(1),
        name="out_side",
    )(x2d, ac, ap, aa, xs2d, acs, aps, aas, *weights)


def kernel(x_prompt, x_sample, mem_prompt, cache_mem_k, cache_mem_v, state_conv, state_pool, norm_g, w_in, conv_w, pool_w, pool_scale, mem_norm_g, w_mem_kv, w_br_conv, w_br_pool, w_br_att, w_out, final_norm_g):
    depth = w_in.shape[0]
    b_p, t_p, _ = x_prompt.shape
    n_b, n_t, _ = x_sample.shape
    rows_p, rows_s = b_p * t_p, n_b * n_t

    k_f, v_f, k_b, v_b = _mem_kv(mem_prompt.reshape(b_p * N_MEM, D_MODEL), mem_norm_g, w_mem_kv)

    sw = dict(
        g=norm_g.reshape(depth, 1, D_MODEL), wmerge=w_in[:, :, OFF_MERGE:].astype(BF16), conv_w=conv_w,
        pool_w=pool_w.astype(BF16), pool_scale=pool_scale.reshape(depth, 1, D_MODEL),
        wbc=w_br_conv.astype(BF16), wbp=w_br_pool.astype(BF16), wba=w_br_att.astype(BF16),
        wout=w_out.astype(BF16), fg=final_norm_g.reshape(1, D_MODEL))

    cst_t = jnp.transpose(state_conv, (0, 2, 1, 3))
    pst_t = jnp.transpose(state_pool, (0, 2, 1, 3))

    xp, xs = x_prompt, x_sample.reshape(rows_s, D_MODEL)
    cv_p, pl_p, u_s, hp_s = [], [], [], []
    flat = lambda a: a.reshape(rows_p, D_MODEL)
    for l in range(depth):
        ac_s, ap_s, q, sg, u_t, hp_t, *w_a = _sample_in(
            xs, cst_t, pst_t, sw["g"], w_in, conv_w, pool_w, sw["pool_scale"], l, n_t, n_b)
        u_s.append(u_t)
        hp_s.append(hp_t)

        ac_p, ap_p, aa_p, aa_s, c_new, p_new = _prompt_in(
            xp, k_b, v_b, q, sg, cache_mem_k, cache_mem_v, w_a, sw, l, n_t)
        cv_p.append(c_new)
        pl_p.append(p_new)

        xp, xs = _out_side(flat(xp), flat(ac_p), flat(ap_p), flat(aa_p), xs, ac_s, ap_s, aa_s,
                           sw, l, l == depth - 1)
        xp = xp.reshape(b_p, t_p, D_MODEL)

    new_conv_s = jnp.transpose(jnp.stack(u_s)[:, n_t - (CONV_WIDTH - 1):], (0, 2, 1, 3))
    new_pool_s = jnp.concatenate(
        [state_pool[:, :, n_t:], jnp.transpose(jnp.stack(hp_s), (0, 2, 1, 3))], axis=2)
    return (xp, xs.reshape(n_b, n_t, D_MODEL), k_f, v_f, jnp.stack(cv_p), jnp.stack(pl_p),
            new_conv_s, new_pool_s)
```

```python
import functools

import jax
import jax.numpy as jnp
from jax import lax
from jax.experimental import pallas as pl
from jax.experimental.pallas import tpu as pltpu

D_MODEL = 1024
N_MEM = 256
HEADS = 4
HEAD_DIM = 256
CONV_WIDTH = 3
POOL_WINDOWS = (2, 4, 8, 16)
POOL_GROUP = 256
POOL_STATE = 15
PAST_LEN = 16384
EPS = 1e-6
OFF_HC, OFF_BC, OFF_CC, OFF_GC = 0, 1024, 2048, 3072
OFF_HP, OFF_GP = 4096, 5120
OFF_Q = 6144
OFF_GA = 7168
OFF_MERGE = 8192
N_MERGE = 3
N_PROJ = 8
N_CAST_ONLY = N_MERGE

CW = 256
N_CHUNKS = D_MODEL // CW
assert CW == POOL_GROUP == HEAD_DIM
TM = 512
TM_OUT = 512
POOL_PAD = 32
CONV_PAD = 8
VMEM_BYTES_V7X = 64 * 1024 * 1024
VMEM_LIMIT = VMEM_BYTES_V7X - 2 * 1024 * 1024

F32 = jnp.float32
BF16 = jnp.bfloat16


def _dot(a, b):
    return jnp.dot(a, b, preferred_element_type=F32)


def _dot_nt(a, b):
    return lax.dot_general(a, b, (((1,), (1,)), ((), ())), preferred_element_type=F32)


def _sigmoid(x):
    return 0.5 * jnp.tanh(0.5 * x) + 0.5


def _silu(x):
    return x * _sigmoid(x)


def _rmsnorm(x, g):
    ms = jnp.mean(x * x, axis=-1, keepdims=True)
    return (x * lax.rsqrt(ms + EPS)) * g


def _softmax(s):
    e = jnp.exp(s - jnp.max(s, axis=-1, keepdims=True))
    return e * (1.0 / jnp.sum(e, axis=-1, keepdims=True))


def _cols(off, c, w=CW):
    return slice(off + c * w, off + (c + 1) * w)


def _params(n_axes):
    return pltpu.CompilerParams(
        dimension_semantics=("arbitrary",) * n_axes, vmem_limit_bytes=VMEM_LIMIT)


def _const_spec(shape):
    nd = len(shape)
    return pl.BlockSpec(shape, lambda *_: (0,) * nd, pipeline_mode=pl.Buffered(1))


def _layer_spec(arr, layer, cols=None, col_block=0):
    tail = list(arr.shape[1:])
    idx = [0] * len(tail)
    if cols is not None:
        tail[-1] = cols
        idx[-1] = col_block
    return pl.BlockSpec((None, *tail), lambda *_: (layer, *idx), pipeline_mode=pl.Buffered(1))


def _kv_kernel(mem_ref, g_ref, w_ref, k_ref, v_ref, kb_ref, vb_ref):
    h = _rmsnorm(mem_ref[...], g_ref[0]).astype(BF16)
    k = _dot(h, w_ref[0, :, :D_MODEL].astype(BF16))
    v = _dot(h, w_ref[0, :, D_MODEL:].astype(BF16))
    k_ref[0] = k.reshape(k_ref.shape[1:])
    v_ref[0] = v.reshape(v_ref.shape[1:])
    kb_ref[0] = k.astype(BF16)
    vb_ref[0] = v.astype(BF16)


def _mem_kv(mem2d, mem_norm_g, w_kv):
    depth = w_kv.shape[0]
    rows = mem2d.shape[0]
    bt = 2
    rt = bt * N_MEM
    out_f = jax.ShapeDtypeStruct((depth, rows // N_MEM, N_MEM, HEADS, HEAD_DIM), F32)
    out_b = jax.ShapeDtypeStruct((depth, rows, D_MODEL), BF16)
    blk = pl.BlockSpec((1, rt, D_MODEL), lambda l, r: (l, r, 0))
    blk5 = pl.BlockSpec((1, bt, N_MEM, HEADS, HEAD_DIM), lambda l, r: (l, r, 0, 0, 0))
    return pl.pallas_call(
        _kv_kernel,
        grid=(depth, rows // rt),
        in_specs=[
            pl.BlockSpec((rt, D_MODEL), lambda l, r: (r, 0)),
            pl.BlockSpec((1, 1, D_MODEL), lambda l, r: (l, 0, 0)),
            pl.BlockSpec((1, D_MODEL, 2 * D_MODEL), lambda l, r: (l, 0, 0)),
        ],
        out_specs=[blk5, blk5, blk, blk],
        out_shape=[out_f, out_f, out_b, out_b],
        compiler_params=_params(2),
        name="mem_kv",
    )(mem2d, mem_norm_g.reshape(depth, 1, D_MODEL), w_kv)


KV_TILE = CW // HEADS
N_KV_TILES = N_MEM // KV_TILE


def _sample_attention_tiles(sq_ref, ssg_ref, sk_ref, sv_ref, saa_ref, sb, n_t):
    n_flat = N_MEM * HEADS
    col_head = lax.broadcasted_iota(jnp.int32, (HEADS * n_t, n_flat), 1) % HEADS
    row_head = lax.broadcasted_iota(jnp.int32, (HEADS * n_t, n_flat), 0) // n_t
    own = col_head == row_head
    for i in range(sb):
        rows = slice(i * n_t, (i + 1) * n_t)
        qb = sq_ref[rows, :]
        qe = jnp.concatenate([qb[:, _cols(0, hh, HEAD_DIM)] for hh in range(HEADS)], axis=0).astype(BF16)
        s = []
        for j in range(N_KV_TILES):
            kf = sk_ref[0, i, j * KV_TILE:(j + 1) * KV_TILE].reshape(CW, HEAD_DIM).astype(BF16)
            s.append(_dot_nt(qe, kf))
            yield
        s = jnp.where(own, jnp.concatenate(s, axis=1) * HEAD_DIM ** -0.5, -jnp.inf)
        p = _softmax(s).astype(BF16)
        o = None
        for j in range(N_KV_TILES):
            vf = sv_ref[0, i, j * KV_TILE:(j + 1) * KV_TILE].reshape(CW, HEAD_DIM).astype(BF16)
            part = _dot(p[:, _cols(0, j)], vf)
            o = part if o is None else o + part
            yield
        o = jnp.concatenate([o[hh * n_t:(hh + 1) * n_t] for hh in range(HEADS)], axis=1)
        saa_ref[rows, :] = o * ssg_ref[rows, :]


def _prompt_in_kernel(x_ref, k_ref, v_ref, sq_ref, ssg_ref, sk_ref, sv_ref,
                      g_ref, whp_ref, wq_ref, whc_ref, wcc_ref, wbc_ref, wgc_ref, wgp_ref, wga_ref,
                      convw_ref, poolw_ref, pscale_ref,
                      ac_ref, ap_ref, aa_ref, saa_ref, cst_ref, pst_ref,
                      u_s, p_s, sa_s, sb_s, *, sb, n_t):
    t = pl.program_id(1)

    @pl.when(t == 0)
    def _():
        u_s[0:CONV_PAD, :] = jnp.zeros((CONV_PAD, D_MODEL), F32)
        p_s[0:POOL_PAD, :] = jnp.zeros((POOL_PAD, D_MODEL), F32)
        sa_s[0:16, :] = jnp.zeros((16, POOL_GROUP), F32)
        sb_s[0:16, :] = jnp.zeros((16, POOL_GROUP), F32)

    sample_attn = _sample_attention_tiles(sq_ref, ssg_ref, sk_ref, sv_ref, saa_ref, sb, n_t)
    for _ in range(N_KV_TILES):
        next(sample_attn, None)
    x = x_ref[0]
    h = _rmsnorm(x, g_ref[...]).astype(BF16)

    def hdot(w_ref, c):
        r = _dot(h, w_ref[:, _cols(0, c)])
        next(sample_attn, None)
        return r

    pos1 = (t * TM + 1 + lax.broadcasted_iota(jnp.int32, (TM, 1), 0)).astype(F32)
    n = TM + 16
    for c, w in enumerate(POOL_WINDOWS):
        sl = _cols(0, c)

        hp = hdot(whp_ref, c)
        p_s[POOL_PAD:POOL_PAD + TM, sl] = hp
        cur = p_s[16:16 + n, sl] + p_s[15:15 + n, sl]
        shift, src, dst = 2, sa_s, sb_s
        while shift < w:
            src[16:16 + n, :] = cur
            cur = src[16:16 + n, :] + src[16 - shift:16 - shift + n, :]
            shift *= 2
            src, dst = dst, src
        inv_cnt = 1.0 / jnp.minimum(pos1, float(w))
        mixed = (cur[16:, :] * inv_cnt - hp).astype(BF16)

        q = hdot(wq_ref, c).astype(BF16)

        hc = hdot(whc_ref, c)
        cc = hdot(wcc_ref, c)
        u = cc * hc
        u_s[CONV_PAD:CONV_PAD + TM, sl] = u
        u1 = u_s[CONV_PAD - 1:CONV_PAD - 1 + TM, sl]
        u2 = u_s[CONV_PAD - 2:CONV_PAD - 2 + TM, sl]
        y = convw_ref[0:1, sl] * u2 + convw_ref[1:2, sl] * u1 + convw_ref[2:3, sl] * u

        p = _softmax(_dot_nt(q, k_ref[:, sl]) * HEAD_DIM ** -0.5).astype(BF16)

        bc = hdot(wbc_ref, c)
        gc = hdot(wgc_ref, c)
        ac_ref[0, :, sl] = (bc * y * _silu(gc)).astype(BF16)

        gp = hdot(wgp_ref, c)
        pooled = _dot(mixed, poolw_ref[c]) * pscale_ref[:, sl]
        ap_ref[0, :, sl] = (pooled * _silu(gp)).astype(BF16)

        ga = hdot(wga_ref, c)
        o = _dot(p, v_ref[:, sl])
        aa_ref[0, :, sl] = (o * _silu(ga)).astype(BF16)

    new_conv = u_s[CONV_PAD + TM - 2:CONV_PAD + TM, :]
    cst_ref[0] = new_conv
    u_s[CONV_PAD - 2:CONV_PAD, :] = new_conv
    pst_ref[0] = p_s[POOL_PAD + TM - POOL_STATE:POOL_PAD + TM, :]
    p_s[16:POOL_PAD, :] = p_s[TM + 16:TM + POOL_PAD, :]

    for _ in sample_attn:
        pass


def _prompt_in(x, kb, vb, sq, ssg, cache_k, cache_v, w_a, sw, layer, n_t):
    b, t, _ = x.shape
    nt = t // TM
    n_b = sq.shape[0] // n_t
    sb = n_b // (b * nt)
    assert sb * b * nt == n_b
    tile = pl.BlockSpec((1, TM, D_MODEL), lambda i, j: (i, j, 0))
    smp = pl.BlockSpec((sb * n_t, D_MODEL), lambda i, j: (i * nt + j, 0))
    skv = pl.BlockSpec((1, sb, N_MEM, HEADS, HEAD_DIM), lambda i, j: (layer, i * nt + j, 0, 0, 0))
    act = jax.ShapeDtypeStruct((b, t, D_MODEL), BF16)
    return pl.pallas_call(
        functools.partial(_prompt_in_kernel, sb=sb, n_t=n_t),
        grid=(b, nt),
        in_specs=[
            tile,
            pl.BlockSpec((None, N_MEM, D_MODEL), lambda i, j: (layer, i, 0)),
            pl.BlockSpec((None, N_MEM, D_MODEL), lambda i, j: (layer, i, 0)),
            smp, smp, skv, skv,
            _layer_spec(sw["g"], layer),
        ] + [_const_spec(w.shape) for w in w_a] + [
            _layer_spec(sw["conv_w"], layer), _layer_spec(sw["pool_w"], layer), _layer_spec(sw["pool_scale"], layer),
        ],
        out_specs=[
            tile, tile, tile, smp,
            pl.BlockSpec((1, CONV_WIDTH - 1, D_MODEL), lambda i, j: (i, 0, 0)),
            pl.BlockSpec((1, POOL_STATE, D_MODEL), lambda i, j: (i, 0, 0)),
        ],
        out_shape=[
            act, act, act,
            jax.ShapeDtypeStruct(sq.shape, F32),
            jax.ShapeDtypeStruct((b, CONV_WIDTH - 1, D_MODEL), F32),
            jax.ShapeDtypeStruct((b, POOL_STATE, D_MODEL), F32),
        ],
        scratch_shapes=[
            pltpu.VMEM((CONV_PAD + TM, D_MODEL), F32),
            pltpu.VMEM((POOL_PAD + TM, D_MODEL), F32),
            pltpu.VMEM((POOL_PAD + TM, POOL_GROUP), F32),
            pltpu.VMEM((POOL_PAD + TM, POOL_GROUP), F32),
        ],
        compiler_params=_params(2),
        name="prompt_in",
    )(x, kb, vb, sq, ssg, cache_k, cache_v, sw["g"], *w_a, sw["conv_w"], sw["pool_w"], sw["pool_scale"])


def _sample_in_kernel(*refs, n_t, n_b):
    refs = list(refs)
    x_ref, tb_ref, bt_ref, cst_ref, pst_ref, g_ref = refs[:6]
    w_refs = refs[6:6 + N_PROJ + N_CAST_ONLY]
    convw_ref, poolw_ref, pscale_ref = refs[6 + N_PROJ + N_CAST_ONLY:9 + N_PROJ + N_CAST_ONLY]
    outs = refs[9 + N_PROJ + N_CAST_ONLY:]
    ac_ref, ap_ref, q_ref, sg_ref, ncst_ref, npst_ref = outs[:6]
    wb_refs = outs[6:6 + N_PROJ + N_CAST_ONLY]
    hb_s, ht_s = outs[6 + N_PROJ + N_CAST_ONLY:]
    c = pl.program_id(0)

    @pl.when(c == 0)
    def _():
        hb = _rmsnorm(x_ref[...], g_ref[...]).astype(BF16)
        hb_s[...] = hb
        ht_s[...] = _dot(tb_ref[...], hb).astype(BF16)

    hb, ht = hb_s[...], ht_s[...]
    w = []
    for w_ref, wb_ref in zip(w_refs, wb_refs):
        w.append(w_ref[...].astype(BF16))
        wb_ref[...] = w[-1]
    whp, wq, whc, wcc, wbc, wgc, wgp, wga = w[:N_PROJ]

    def to_batch_major(a):
        return _dot(bt_ref[...], a.astype(BF16)).astype(BF16)

    def planes(a):
        return [a[i * n_b:(i + 1) * n_b] for i in range(n_t)]

    u = _dot(ht, wcc) * _dot(ht, whc)
    ext = [cst_ref[0], cst_ref[1]] + planes(u)
    y = jnp.concatenate(
        [convw_ref[0:1, :] * ext[i] + convw_ref[1:2, :] * ext[i + 1] + convw_ref[2:3, :] * ext[i + 2]
         for i in range(n_t)], axis=0)
    ac_ref[...] = to_batch_major(_dot(ht, wbc) * y * _silu(_dot(ht, wgc)))
    for j, plane in enumerate(ext[-(CONV_WIDTH - 1):]):
        ncst_ref[j] = plane

    hp = _dot(ht, whp)
    silu_gp = _silu(_dot(ht, wgp))
    ext = [pst_ref[j] for j in range(POOL_STATE)] + planes(hp)
    for j, plane in enumerate(ext[-POOL_STATE:]):
        npst_ref[j] = plane
    for g, win in enumerate(POOL_WINDOWS):
        @pl.when(c == g)
        def _(win=win):
            mixed = []
            for i in range(n_t):
                cnt = float(min(PAST_LEN + i + 1, win))
                acc = ext[POOL_STATE + i]
                for j in range(1, win):
                    acc = acc + ext[POOL_STATE + i - j]
                mixed.append(acc * (1.0 / cnt) - ext[POOL_STATE + i])
            mixed = jnp.concatenate(mixed, axis=0).astype(BF16)
            pooled = _dot(mixed, poolw_ref[...].astype(BF16)) * pscale_ref[...]
            ap_ref[...] = to_batch_major(pooled * silu_gp)

    q_ref[...] = _dot(hb, wq)
    sg_ref[...] = _silu(_dot(hb, wga))


def _row_permutations(n_t, n_b):
    r = jnp.arange(n_t * n_b)
    tb = jax.nn.one_hot((r % n_b) * n_t + r // n_b, n_t * n_b, dtype=BF16)
    return tb, tb.T


def _sample_in(xs2d, cst_t, pst_t, g, w_in, conv_w, pool_w, pool_scale, layer, n_t, n_b):
    rows = n_t * n_b
    tb, bt = _row_permutations(n_t, n_b)
    offs = (OFF_HP, OFF_Q, OFF_HC, OFF_CC, OFF_BC, OFF_GC, OFF_GP, OFF_GA) + tuple(
        OFF_MERGE + k * D_MODEL for k in range(N_MERGE))
    weights = [w_in] * len(offs)
    blocks = [off // CW for off in offs]
    assert len(weights) == N_PROJ + N_CAST_ONLY
    w_specs = [pl.BlockSpec((None, D_MODEL, CW), functools.partial(lambda c, blk: (layer, 0, blk + c), blk=blk))
               for blk in blocks]
    chunk2d = pl.BlockSpec((rows, CW), lambda c: (0, c))

    def hist(n_rows):
        return pl.BlockSpec((None, n_rows, n_b, CW), lambda c: (layer, 0, 0, c))

    def new_hist(n_rows):
        return pl.BlockSpec((n_rows, n_b, CW), lambda c: (0, 0, c))

    return pl.pallas_call(
        functools.partial(_sample_in_kernel, n_t=n_t, n_b=n_b),
        grid=(N_CHUNKS,),
        in_specs=[
            _const_spec(xs2d.shape), _const_spec(tb.shape), _const_spec(bt.shape),
            hist(CONV_WIDTH - 1), hist(POOL_STATE), _layer_spec(g, layer),
        ] + w_specs + [
            pl.BlockSpec((None, CONV_WIDTH, CW), lambda c: (layer, 0, c)),
            pl.BlockSpec((None, None, POOL_GROUP, POOL_GROUP), lambda c: (layer, c, 0, 0)),
            pl.BlockSpec((None, 1, CW), lambda c: (layer, 0, c)),
        ],
        out_specs=([chunk2d] * 4 + [new_hist(CONV_WIDTH - 1), new_hist(POOL_STATE)]
                   + [pl.BlockSpec((D_MODEL, CW), lambda c: (0, c))] * len(weights)),
        out_shape=[
            jax.ShapeDtypeStruct((rows, D_MODEL), BF16),
            jax.ShapeDtypeStruct((rows, D_MODEL), BF16),
            jax.ShapeDtypeStruct((rows, D_MODEL), F32),
            jax.ShapeDtypeStruct((rows, D_MODEL), F32),
            jax.ShapeDtypeStruct((CONV_WIDTH - 1, n_b, D_MODEL), F32),
            jax.ShapeDtypeStruct((POOL_STATE, n_b, D_MODEL), F32),
        ] + [jax.ShapeDtypeStruct((D_MODEL, D_MODEL), BF16)] * len(weights),
        scratch_shapes=[pltpu.VMEM((rows, D_MODEL), BF16), pltpu.VMEM((rows, D_MODEL), BF16)],
        compiler_params=_params(1),
        name="sample_in",
    )(xs2d, tb, bt, cst_t, pst_t, g, *weights, conv_w, pool_w, pool_scale)


def _out_side_kernel(x_ref, ac_ref, ap_ref, aa_ref, xs_ref, acs_ref, aps_ref, aas_ref,
                     g_ref, wmc_ref, wmp_ref, wma_ref, wbc_ref, wbp_ref, wba_ref, wout_ref, fg_ref,
                     y_ref, ys_ref, m_s, *, final_norm, n_prompt_tiles):
    def tile(x_ref, ac_ref, ap_ref, aa_ref, y_ref):
        rows = x_ref.shape[0]
        x = x_ref[...]
        h = _rmsnorm(x, g_ref[...]).astype(BF16)
        ac, ap, aa = ac_ref[...], ap_ref[...], aa_ref[...].astype(BF16)
        for c in range(N_CHUNKS):
            sl = _cols(0, c)
            conv_br = _dot(ac, wbc_ref[:, sl])
            pool_br = _dot(ap, wbp_ref[:, sl])
            att_br = _dot(aa, wba_ref[:, sl])
            mc = _dot(h, wmc_ref[:, sl])
            mp = _dot(h, wmp_ref[:, sl])
            ma = _dot(h, wma_ref[:, sl])
            merged = _sigmoid(mc) * conv_br + _sigmoid(mp) * pool_br + _sigmoid(ma) * att_br
            m_s[0:rows, sl] = merged.astype(BF16)
        xn = x + _dot(m_s[0:rows, :], wout_ref[...])
        if final_norm:
            xn = _rmsnorm(xn, fg_ref[...])
        y_ref[...] = xn

    r = pl.program_id(0)

    @pl.when(r < n_prompt_tiles)
    def _():
        tile(x_ref, ac_ref, ap_ref, aa_ref, y_ref)

    @pl.when(r == n_prompt_tiles)
    def _():
        tile(xs_ref, acs_ref, aps_ref, aas_ref, ys_ref)


def _out_side(x2d, ac, ap, aa, xs2d, acs, aps, aas, w_merge, sw, layer, final_norm):
    rows, rows_s = x2d.shape[0], xs2d.shape[0]
    n_tiles = rows // TM_OUT
    assert n_tiles * TM_OUT == rows and rows_s <= TM_OUT
    tile = pl.BlockSpec((TM_OUT, D_MODEL), lambda r: (jnp.minimum(r, n_tiles - 1), 0))
    rest = ("wbc", "wbp", "wba", "wout")
    weights = (sw["g"], *w_merge, *(sw[n] for n in rest), sw["fg"])
    w_specs = ([_layer_spec(sw["g"], layer)] + [_const_spec(w.shape) for w in w_merge]
               + [_layer_spec(sw[n], layer) for n in rest] + [_const_spec(sw["fg"].shape)])
    smp = _const_spec((rows_s, D_MODEL))
    return pl.pallas_call(
        functools.partial(_out_side_kernel, final_norm=final_norm, n_prompt_tiles=n_tiles),
        grid=(n_tiles + 1,),
        in_specs=[tile] * 4 + [smp] * 4 + w_specs,
        out_specs=[tile, smp],
        out_shape=[jax.ShapeDtypeStruct((rows, D_MODEL), F32), jax.ShapeDtypeStruct((rows_s, D_MODEL), F32)],
        scratch_shapes=[pltpu.VMEM((TM_OUT, D_MODEL), BF16)],
        compiler_params=_params(1),
        name="out_side",
    )(x2d, ac, ap, aa, xs2d, acs, aps, aas, *weights)


def kernel(x_prompt, x_sample, mem_prompt, cache_mem_k, cache_mem_v, state_conv, state_pool, norm_g, w_in, conv_w, pool_w, pool_scale, mem_norm_g, w_mem_kv, w_br_conv, w_br_pool, w_br_att, w_out, final_norm_g):
    depth = w_in.shape[0]
    b_p, t_p, _ = x_prompt.shape
    n_b, n_t, _ = x_sample.shape
    rows_p, rows_s = b_p * t_p, n_b * n_t

    k_f, v_f, k_b, v_b = _mem_kv(mem_prompt.reshape(b_p * N_MEM, D_MODEL), mem_norm_g, w_mem_kv)

    sw = dict(
        g=norm_g.reshape(depth, 1, D_MODEL), conv_w=conv_w, pool_w=pool_w.astype(BF16),
        pool_scale=pool_scale.reshape(depth, 1, D_MODEL), fg=final_norm_g.reshape(1, D_MODEL),
        wbc=w_br_conv.astype(BF16), wbp=w_br_pool.astype(BF16), wba=w_br_att.astype(BF16),
        wout=w_out.astype(BF16))

    cst_t = jnp.transpose(state_conv, (0, 2, 1, 3))
    pst_t = jnp.transpose(state_pool, (0, 2, 1, 3))

    xp, xs = x_prompt, x_sample.reshape(rows_s, D_MODEL)
    cv_p, pl_p, cv_s, pl_s = [], [], [], []
    flat = lambda a: a.reshape(rows_p, D_MODEL)
    for l in range(depth):
        ac_s, ap_s, q, sg, c_new_s, p_new_s, *w_bf16 = _sample_in(
            xs, cst_t, pst_t, sw["g"], w_in, conv_w, pool_w, sw["pool_scale"], l, n_t, n_b)
        cv_s.append(c_new_s)
        pl_s.append(p_new_s)

        ac_p, ap_p, aa_p, aa_s, c_new, p_new = _prompt_in(
            xp, k_b, v_b, q, sg, cache_mem_k, cache_mem_v, w_bf16[:N_PROJ], sw, l, n_t)
        cv_p.append(c_new)
        pl_p.append(p_new)

        xp, xs = _out_side(flat(xp), flat(ac_p), flat(ap_p), flat(aa_p), xs, ac_s, ap_s, aa_s,
                           w_bf16[N_PROJ:], sw, l, l == depth - 1)
        xp = xp.reshape(b_p, t_p, D_MODEL)

    batch_first = lambda hist: jnp.transpose(jnp.stack(hist), (0, 2, 1, 3))
    return (xp, xs.reshape(n_b, n_t, D_MODEL), k_f, v_f, jnp.stack(cv_p), jnp.stack(pl_p),
            batch_first(cv_s), batch_first(pl_s))
```

```python
import functools

import jax
import jax.numpy as jnp
from jax import lax
from jax.experimental import pallas as pl
from jax.experimental.pallas import tpu as pltpu

D_MODEL = 1024
N_MEM = 256
HEADS = 4
HEAD_DIM = 256
CONV_WIDTH = 3
POOL_WINDOWS = (2, 4, 8, 16)
POOL_GROUP = 256
POOL_STATE = 15
PAST_LEN = 16384
EPS = 1e-6
OFF_HC, OFF_BC, OFF_CC, OFF_GC = 0, 1024, 2048, 3072
OFF_HP, OFF_GP = 4096, 5120
OFF_Q = 6144
OFF_GA = 7168
OFF_MERGE = 8192
N_MERGE = 3
N_PROJ = 8
N_CAST_ONLY = N_MERGE

CW = 256
N_CHUNKS = D_MODEL // CW
assert CW == POOL_GROUP == HEAD_DIM
assert POOL_WINDOWS == tuple(2 ** (g + 1) for g in range(len(POOL_WINDOWS)))
TM = 512
TM_OUT = 512
POOL_PAD = 32
CONV_PAD = 8
VMEM_BYTES_V7X = 64 * 1024 * 1024
VMEM_LIMIT = VMEM_BYTES_V7X - 2 * 1024 * 1024

F32 = jnp.float32
BF16 = jnp.bfloat16


def _dot(a, b):
    return jnp.dot(a, b, preferred_element_type=F32)


def _dot_nt(a, b):
    return lax.dot_general(a, b, (((1,), (1,)), ((), ())), preferred_element_type=F32)


def _sigmoid(x):
    return 0.5 * jnp.tanh(0.5 * x) + 0.5


def _silu(x):
    return x * _sigmoid(x)


def _rmsnorm(x, g):
    ms = jnp.mean(x * x, axis=-1, keepdims=True)
    return (x * lax.rsqrt(ms + EPS)) * g


def _softmax(s):
    e = jnp.exp(s - jnp.max(s, axis=-1, keepdims=True))
    return e * (1.0 / jnp.sum(e, axis=-1, keepdims=True))


def _cols(off, c, w=CW):
    return slice(off + c * w, off + (c + 1) * w)


def _params(n_axes):
    return pltpu.CompilerParams(
        dimension_semantics=("arbitrary",) * n_axes, vmem_limit_bytes=VMEM_LIMIT)


def _const_spec(shape):
    nd = len(shape)
    return pl.BlockSpec(shape, lambda *_: (0,) * nd, pipeline_mode=pl.Buffered(1))


def _layer_spec(arr, layer, cols=None, col_block=0):
    tail = list(arr.shape[1:])
    idx = [0] * len(tail)
    if cols is not None:
        tail[-1] = cols
        idx[-1] = col_block
    return pl.BlockSpec((None, *tail), lambda *_: (layer, *idx), pipeline_mode=pl.Buffered(1))


def _kv_kernel(mem_ref, g_ref, w_ref, k_ref, v_ref, kb_ref, vb_ref):
    h = _rmsnorm(mem_ref[...], g_ref[0]).astype(BF16)
    k = _dot(h, w_ref[0, :, :D_MODEL].astype(BF16))
    v = _dot(h, w_ref[0, :, D_MODEL:].astype(BF16))
    k_ref[0] = k.reshape(k_ref.shape[1:])
    v_ref[0] = v.reshape(v_ref.shape[1:])
    kb_ref[0] = k.astype(BF16)
    vb_ref[0] = v.astype(BF16)


def _mem_kv(mem2d, mem_norm_g, w_kv):
    depth = w_kv.shape[0]
    rows = mem2d.shape[0]
    bt = 2
    rt = bt * N_MEM
    out_f = jax.ShapeDtypeStruct((depth, rows // N_MEM, N_MEM, HEADS, HEAD_DIM), F32)
    out_b = jax.ShapeDtypeStruct((depth, rows, D_MODEL), BF16)
    blk = pl.BlockSpec((1, rt, D_MODEL), lambda l, r: (l, r, 0))
    blk5 = pl.BlockSpec((1, bt, N_MEM, HEADS, HEAD_DIM), lambda l, r: (l, r, 0, 0, 0))
    return pl.pallas_call(
        _kv_kernel,
        grid=(depth, rows // rt),
        in_specs=[
            pl.BlockSpec((rt, D_MODEL), lambda l, r: (r, 0)),
            pl.BlockSpec((1, 1, D_MODEL), lambda l, r: (l, 0, 0)),
            pl.BlockSpec((1, D_MODEL, 2 * D_MODEL), lambda l, r: (l, 0, 0)),
        ],
        out_specs=[blk5, blk5, blk, blk],
        out_shape=[out_f, out_f, out_b, out_b],
        compiler_params=_params(2),
        name="mem_kv",
    )(mem2d, mem_norm_g.reshape(depth, 1, D_MODEL), w_kv)


KV_TILE = CW // HEADS
N_KV_TILES = N_MEM // KV_TILE


def _sample_attention_tiles(sq_ref, ssg_ref, sk_ref, sv_ref, saa_ref, sb, n_t):
    n_flat = N_MEM * HEADS
    col_head = lax.broadcasted_iota(jnp.int32, (HEADS * n_t, n_flat), 1) % HEADS
    row_head = lax.broadcasted_iota(jnp.int32, (HEADS * n_t, n_flat), 0) // n_t
    own = col_head == row_head
    for i in range(sb):
        rows = slice(i * n_t, (i + 1) * n_t)
        qb = sq_ref[rows, :]
        qe = jnp.concatenate([qb[:, _cols(0, hh, HEAD_DIM)] for hh in range(HEADS)], axis=0).astype(BF16)
        s = []
        for j in range(N_KV_TILES):
            kf = sk_ref[0, i, j * KV_TILE:(j + 1) * KV_TILE].reshape(CW, HEAD_DIM).astype(BF16)
            s.append(_dot_nt(qe, kf))
            yield
        s = jnp.where(own, jnp.concatenate(s, axis=1) * HEAD_DIM ** -0.5, -jnp.inf)
        p = _softmax(s).astype(BF16)
        o = None
        for j in range(N_KV_TILES):
            vf = sv_ref[0, i, j * KV_TILE:(j + 1) * KV_TILE].reshape(CW, HEAD_DIM).astype(BF16)
            part = _dot(p[:, _cols(0, j)], vf)
            o = part if o is None else o + part
            yield
        o = jnp.concatenate([o[hh * n_t:(hh + 1) * n_t] for hh in range(HEADS)], axis=1)
        saa_ref[rows, :] = o * ssg_ref[rows, :]


def _prompt_in_kernel(x_ref, k_ref, v_ref, sq_ref, ssg_ref, sk_ref, sv_ref,
                      g_ref, whp_ref, wq_ref, whc_ref, wcc_ref, wbc_ref, wgc_ref, wgp_ref, wga_ref,
                      convw_ref, poolw_ref, pscale_ref,
                      ac_ref, ap_ref, aa_ref, saa_ref, cst_ref, pst_ref,
                      u_s, p_s, sa_s, sb_s, *, sb, n_t):
    t = pl.program_id(1)

    @pl.when(t == 0)
    def _():
        u_s[0:CONV_PAD, :] = jnp.zeros((CONV_PAD, D_MODEL), F32)
        p_s[0:POOL_PAD, :] = jnp.zeros((POOL_PAD, D_MODEL), F32)
        sa_s[0:16, :] = jnp.zeros((16, POOL_GROUP), F32)
        sb_s[0:16, :] = jnp.zeros((16, POOL_GROUP), F32)

    sample_attn = _sample_attention_tiles(sq_ref, ssg_ref, sk_ref, sv_ref, saa_ref, sb, n_t)
    for _ in range(N_KV_TILES):
        next(sample_attn, None)
    x = x_ref[0]
    h = _rmsnorm(x, g_ref[...]).astype(BF16)

    def hdot(w_ref, c):
        r = _dot(h, w_ref[:, _cols(0, c)])
        next(sample_attn, None)
        return r

    pos1 = (t * TM + 1 + lax.broadcasted_iota(jnp.int32, (TM, 1), 0)).astype(F32)
    n = TM + 16
    for c, w in enumerate(POOL_WINDOWS):
        sl = _cols(0, c)

        hp = hdot(whp_ref, c)
        p_s[POOL_PAD:POOL_PAD + TM, sl] = hp
        cur = p_s[16:16 + n, sl] + p_s[15:15 + n, sl]
        shift, src, dst = 2, sa_s, sb_s
        while shift < w:
            src[16:16 + n, :] = cur
            cur = src[16:16 + n, :] + src[16 - shift:16 - shift + n, :]
            shift *= 2
            src, dst = dst, src
        inv_cnt = 1.0 / jnp.minimum(pos1, float(w))
        mixed = (cur[16:, :] * inv_cnt - hp).astype(BF16)

        q = hdot(wq_ref, c).astype(BF16)

        hc = hdot(whc_ref, c)
        cc = hdot(wcc_ref, c)
        u = cc * hc
        u_s[CONV_PAD:CONV_PAD + TM, sl] = u
        u1 = u_s[CONV_PAD - 1:CONV_PAD - 1 + TM, sl]
        u2 = u_s[CONV_PAD - 2:CONV_PAD - 2 + TM, sl]
        y = convw_ref[0:1, sl] * u2 + convw_ref[1:2, sl] * u1 + convw_ref[2:3, sl] * u

        p = _softmax(_dot_nt(q, k_ref[:, sl]) * HEAD_DIM ** -0.5).astype(BF16)

        bc = hdot(wbc_ref, c)
        gc = hdot(wgc_ref, c)
        ac_ref[0, :, sl] = (bc * y * _silu(gc)).astype(BF16)

        gp = hdot(wgp_ref, c)
        pooled = _dot(mixed, poolw_ref[c]) * pscale_ref[:, sl]
        ap_ref[0, :, sl] = (pooled * _silu(gp)).astype(BF16)

        ga = hdot(wga_ref, c)
        o = _dot(p, v_ref[:, sl])
        aa_ref[0, :, sl] = (o * _silu(ga)).astype(BF16)

    new_conv = u_s[CONV_PAD + TM - 2:CONV_PAD + TM, :]
    cst_ref[0] = new_conv
    u_s[CONV_PAD - 2:CONV_PAD, :] = new_conv
    pst_ref[0] = p_s[POOL_PAD + TM - POOL_STATE:POOL_PAD + TM, :]
    p_s[16:POOL_PAD, :] = p_s[TM + 16:TM + POOL_PAD, :]

    for _ in sample_attn:
        pass


def _prompt_in(x, kb, vb, sq, ssg, cache_k, cache_v, w_a, sw, layer, n_t):
    b, t, _ = x.shape
    nt = t // TM
    n_b = sq.shape[0] // n_t
    sb = n_b // (b * nt)
    assert sb * b * nt == n_b
    tile = pl.BlockSpec((1, TM, D_MODEL), lambda i, j: (i, j, 0))
    smp = pl.BlockSpec((sb * n_t, D_MODEL), lambda i, j: (i * nt + j, 0))
    skv = pl.BlockSpec((1, sb, N_MEM, HEADS, HEAD_DIM), lambda i, j: (layer, i * nt + j, 0, 0, 0))
    act = jax.ShapeDtypeStruct((b, t, D_MODEL), BF16)
    return pl.pallas_call(
        functools.partial(_prompt_in_kernel, sb=sb, n_t=n_t),
        grid=(b, nt),
        in_specs=[
            tile,
            pl.BlockSpec((None, N_MEM, D_MODEL), lambda i, j: (layer, i, 0)),
            pl.BlockSpec((None, N_MEM, D_MODEL), lambda i, j: (layer, i, 0)),
            smp, smp, skv, skv,
            _layer_spec(sw["g"], layer),
        ] + [_const_spec(w.shape) for w in w_a] + [
            _layer_spec(sw["conv_w"], layer), _layer_spec(sw["pool_w"], layer), _layer_spec(sw["pool_scale"], layer),
        ],
        out_specs=[
            tile, tile, tile, smp,
            pl.BlockSpec((1, CONV_WIDTH - 1, D_MODEL), lambda i, j: (i, 0, 0)),
            pl.BlockSpec((1, POOL_STATE, D_MODEL), lambda i, j: (i, 0, 0)),
        ],
        out_shape=[
            act, act, act,
            jax.ShapeDtypeStruct(sq.shape, F32),
            jax.ShapeDtypeStruct((b, CONV_WIDTH - 1, D_MODEL), F32),
            jax.ShapeDtypeStruct((b, POOL_STATE, D_MODEL), F32),
        ],
        scratch_shapes=[
            pltpu.VMEM((CONV_PAD + TM, D_MODEL), F32),
            pltpu.VMEM((POOL_PAD + TM, D_MODEL), F32),
            pltpu.VMEM((POOL_PAD + TM, POOL_GROUP), F32),
            pltpu.VMEM((POOL_PAD + TM, POOL_GROUP), F32),
        ],
        compiler_params=_params(2),
        name="prompt_in",
    )(x, kb, vb, sq, ssg, cache_k, cache_v, sw["g"], *w_a, sw["conv_w"], sw["pool_w"], sw["pool_scale"])


def _sample_in_kernel(*refs, n_t, n_b):
    refs = list(refs)
    x_ref, tb_ref, bt_ref, cst_ref, pst_ref, g_ref = refs[:6]
    w_refs = refs[6:6 + N_PROJ + N_CAST_ONLY]
    convw_ref, poolw_ref, pscale_ref = refs[6 + N_PROJ + N_CAST_ONLY:9 + N_PROJ + N_CAST_ONLY]
    outs = refs[9 + N_PROJ + N_CAST_ONLY:]
    ac_ref, ap_ref, q_ref, sg_ref, ncst_ref, npst_ref = outs[:6]
    wb_refs = outs[6:6 + N_PROJ + N_CAST_ONLY]
    hb_s, ht_s = outs[6 + N_PROJ + N_CAST_ONLY:]
    c = pl.program_id(0)

    @pl.when(c == 0)
    def _():
        hb = _rmsnorm(x_ref[...], g_ref[...]).astype(BF16)
        hb_s[...] = hb
        ht_s[...] = _dot(tb_ref[...], hb).astype(BF16)

    hb, ht = hb_s[...], ht_s[...]
    w = []
    for w_ref, wb_ref in zip(w_refs, wb_refs):
        w.append(w_ref[...].astype(BF16))
        wb_ref[...] = w[-1]
    whp, wq, whc, wcc, wbc, wgc, wgp, wga = w[:N_PROJ]

    def to_batch_major(a):
        return _dot(bt_ref[...], a.astype(BF16)).astype(BF16)

    def planes(a):
        return [a[i * n_b:(i + 1) * n_b] for i in range(n_t)]

    hp = _dot(ht, whp)
    u = _dot(ht, wcc) * _dot(ht, whc)
    bc = _dot(ht, wbc)
    silu_gc = _silu(_dot(ht, wgc))
    silu_gp = _silu(_dot(ht, wgp))
    q_ref[...] = _dot(hb, wq)
    sg_ref[...] = _silu(_dot(hb, wga))

    ext = [pst_ref[j] for j in range(POOL_STATE)] + planes(hp)
    for j, plane in enumerate(ext[-POOL_STATE:]):
        npst_ref[j] = plane
    wsum = {}

    def window_sum(k, i):
        if k == 0:
            return ext[i]
        if (k, i) not in wsum:
            wsum[(k, i)] = window_sum(k - 1, i) + window_sum(k - 1, i - 2 ** (k - 1))
        return wsum[(k, i)]

    mixed = []
    for i in range(n_t):
        mean = None
        for g, win in enumerate(POOL_WINDOWS):
            cnt = float(min(PAST_LEN + i + 1, win))
            cand = window_sum(g + 1, POOL_STATE + i) * (1.0 / cnt)
            mean = cand if mean is None else jnp.where(c == g, cand, mean)
        mixed.append(mean - ext[POOL_STATE + i])
    mixed = jnp.concatenate(mixed, axis=0).astype(BF16)
    pooled = _dot(mixed, poolw_ref[...].astype(BF16)) * pscale_ref[...]
    ap_ref[...] = to_batch_major(pooled * silu_gp)

    ext = [cst_ref[0], cst_ref[1]] + planes(u)
    y = jnp.concatenate(
        [convw_ref[0:1, :] * ext[i] + convw_ref[1:2, :] * ext[i + 1] + convw_ref[2:3, :] * ext[i + 2]
         for i in range(n_t)], axis=0)
    ac_ref[...] = to_batch_major(bc * y * silu_gc)
    for j, plane in enumerate(ext[-(CONV_WIDTH - 1):]):
        ncst_ref[j] = plane


def _row_permutations(n_t, n_b):
    r = jnp.arange(n_t * n_b)
    tb = jax.nn.one_hot((r % n_b) * n_t + r // n_b, n_t * n_b, dtype=BF16)
    return tb, tb.T


def _sample_in(xs2d, cst_t, pst_t, g, w_in, conv_w, pool_w, pool_scale, layer, n_t, n_b):
    rows = n_t * n_b
    tb, bt = _row_permutations(n_t, n_b)
    offs = (OFF_HP, OFF_Q, OFF_HC, OFF_CC, OFF_BC, OFF_GC, OFF_GP, OFF_GA) + tuple(
        OFF_MERGE + k * D_MODEL for k in range(N_MERGE))
    weights = [w_in] * len(offs)
    blocks = [off // CW for off in offs]
    assert len(weights) == N_PROJ + N_CAST_ONLY
    w_specs = [pl.BlockSpec((None, D_MODEL, CW), functools.partial(lambda c, blk: (layer, 0, blk + c), blk=blk))
               for blk in blocks]
    chunk2d = pl.BlockSpec((rows, CW), lambda c: (0, c))

    def hist(n_rows):
        return pl.BlockSpec((None, n_rows, n_b, CW), lambda c: (layer, 0, 0, c))

    def new_hist(n_rows):
        return pl.BlockSpec((n_rows, n_b, CW), lambda c: (0, 0, c))

    return pl.pallas_call(
        functools.partial(_sample_in_kernel, n_t=n_t, n_b=n_b),
        grid=(N_CHUNKS,),
        in_specs=[
            _const_spec(xs2d.shape), _const_spec(tb.shape), _const_spec(bt.shape),
            hist(CONV_WIDTH - 1), hist(POOL_STATE), _layer_spec(g, layer),
        ] + w_specs + [
            pl.BlockSpec((None, CONV_WIDTH, CW), lambda c: (layer, 0, c)),
            pl.BlockSpec((None, None, POOL_GROUP, POOL_GROUP), lambda c: (layer, c, 0, 0)),
            pl.BlockSpec((None, 1, CW), lambda c: (layer, 0, c)),
        ],
        out_specs=([chunk2d] * 4 + [new_hist(CONV_WIDTH - 1), new_hist(POOL_STATE)]
                   + [pl.BlockSpec((D_MODEL, CW), lambda c: (0, c))] * len(weights)),
        out_shape=[
            jax.ShapeDtypeStruct((rows, D_MODEL), BF16),
            jax.ShapeDtypeStruct((rows, D_MODEL), BF16),
            jax.ShapeDtypeStruct((rows, D_MODEL), F32),
            jax.ShapeDtypeStruct((rows, D_MODEL), F32),
            jax.ShapeDtypeStruct((CONV_WIDTH - 1, n_b, D_MODEL), F32),
            jax.ShapeDtypeStruct((POOL_STATE, n_b, D_MODEL), F32),
        ] + [jax.ShapeDtypeStruct((D_MODEL, D_MODEL), BF16)] * len(weights),
        scratch_shapes=[pltpu.VMEM((rows, D_MODEL), BF16), pltpu.VMEM((rows, D_MODEL), BF16)],
        compiler_params=_params(1),
        name="sample_in",
    )(xs2d, tb, bt, cst_t, pst_t, g, *weights, conv_w, pool_w, pool_scale)


def _out_side_kernel(x_ref, ac_ref, ap_ref, aa_ref, xs_ref, acs_ref, aps_ref, aas_ref,
                     g_ref, wmc_ref, wmp_ref, wma_ref, wbc_ref, wbp_ref, wba_ref, wout_ref, fg_ref,
                     y_ref, ys_ref, m_s, *, final_norm, n_prompt_tiles):
    def tile(x_ref, ac_ref, ap_ref, aa_ref, y_ref):
        rows = x_ref.shape[0]
        x = x_ref[...]
        h = _rmsnorm(x, g_ref[...]).astype(BF16)
        ac, ap, aa = ac_ref[...], ap_ref[...], aa_ref[...].astype(BF16)
        for c in range(N_CHUNKS):
            sl = _cols(0, c)
            conv_br = _dot(ac, wbc_ref[:, sl])
            pool_br = _dot(ap, wbp_ref[:, sl])
            att_br = _dot(aa, wba_ref[:, sl])
            mc = _dot(h, wmc_ref[:, sl])
            mp = _dot(h, wmp_ref[:, sl])
            ma = _dot(h, wma_ref[:, sl])
            merged = _sigmoid(mc) * conv_br + _sigmoid(mp) * pool_br + _sigmoid(ma) * att_br
            m_s[0:rows, sl] = merged.astype(BF16)
        xn = x + _dot(m_s[0:rows, :], wout_ref[...])
        if final_norm:
            xn = _rmsnorm(xn, fg_ref[...])
        y_ref[...] = xn

    r = pl.program_id(0)

    @pl.when(r < n_prompt_tiles)
    def _():
        tile(x_ref, ac_ref, ap_ref, aa_ref, y_ref)

    @pl.when(r == n_prompt_tiles)
    def _():
        tile(xs_ref, acs_ref, aps_ref, aas_ref, ys_ref)


def _out_side(x2d, ac, ap, aa, xs2d, acs, aps, aas, w_merge, sw, layer, final_norm):
    rows, rows_s = x2d.shape[0], xs2d.shape[0]
    n_tiles = rows // TM_OUT
    assert n_tiles * TM_OUT == rows and rows_s <= TM_OUT
    tile = pl.BlockSpec((TM_OUT, D_MODEL), lambda r: (jnp.minimum(r, n_tiles - 1), 0))
    rest = ("wbc", "wbp", "wba", "wout")
    weights = (sw["g"], *w_merge, *(sw[n] for n in rest), sw["fg"])
    w_specs = ([_layer_spec(sw["g"], layer)] + [_const_spec(w.shape) for w in w_merge]
               + [_layer_spec(sw[n], layer) for n in rest] + [_const_spec(sw["fg"].shape)])
    smp = _const_spec((rows_s, D_MODEL))
    return pl.pallas_call(
        functools.partial(_out_side_kernel, final_norm=final_norm, n_prompt_tiles=n_tiles),
        grid=(n_tiles + 1,),
        in_specs=[tile] * 4 + [smp] * 4 + w_specs,
        out_specs=[tile, smp],
        out_shape=[jax.ShapeDtypeStruct((rows, D_MODEL), F32), jax.ShapeDtypeStruct((rows_s, D_MODEL), F32)],
        scratch_shapes=[pltpu.VMEM((TM_OUT, D_MODEL), BF16)],
        compiler_params=_params(1),
        name="out_side",
    )(x2d, ac, ap, aa, xs2d, acs, aps, aas, *weights)


def kernel(x_prompt, x_sample, mem_prompt, cache_mem_k, cache_mem_v, state_conv, state_pool, norm_g, w_in, conv_w, pool_w, pool_scale, mem_norm_g, w_mem_kv, w_br_conv, w_br_pool, w_br_att, w_out, final_norm_g):
    depth = w_in.shape[0]
    b_p, t_p, _ = x_prompt.shape
    n_b, n_t, _ = x_sample.shape
    rows_p, rows_s = b_p * t_p, n_b * n_t

    k_f, v_f, k_b, v_b = _mem_kv(mem_prompt.reshape(b_p * N_MEM, D_MODEL), mem_norm_g, w_mem_kv)

    sw = dict(
        g=norm_g.reshape(depth, 1, D_MODEL), conv_w=conv_w, pool_w=pool_w.astype(BF16),
        pool_scale=pool_scale.reshape(depth, 1, D_MODEL), fg=final_norm_g.reshape(1, D_MODEL),
        wbc=w_br_conv.astype(BF16), wbp=w_br_pool.astype(BF16), wba=w_br_att.astype(BF16),
        wout=w_out.astype(BF16))

    cst_t = jnp.transpose(state_conv, (0, 2, 1, 3))
    pst_t = jnp.transpose(state_pool, (0, 2, 1, 3))

    xp, xs = x_prompt, x_sample.reshape(rows_s, D_MODEL)
    cv_p, pl_p, cv_s, pl_s = [], [], [], []
    flat = lambda a: a.reshape(rows_p, D_MODEL)
    for l in range(depth):
        ac_s, ap_s, q, sg, c_new_s, p_new_s, *w_bf16 = _sample_in(
            xs, cst_t, pst_t, sw["g"], w_in, conv_w, pool_w, sw["pool_scale"], l, n_t, n_b)
        cv_s.append(c_new_s)
        pl_s.append(p_new_s)

        ac_p, ap_p, aa_p, aa_s, c_new, p_new = _prompt_in(
            xp, k_b, v_b, q, sg, cache_mem_k, cache_mem_v, w_bf16[:N_PROJ], sw, l, n_t)
        cv_p.append(c_new)
        pl_p.append(p_new)

        xp, xs = _out_side(flat(xp), flat(ac_p), flat(ap_p), flat(aa_p), xs, ac_s, ap_s, aa_s,
                           w_bf16[N_PROJ:], sw, l, l == depth - 1)
        xp = xp.reshape(b_p, t_p, D_MODEL)

    batch_first = lambda hist: jnp.transpose(jnp.stack(hist), (0, 2, 1, 3))
    return (xp, xs.reshape(n_b, n_t, D_MODEL), k_f, v_f, jnp.stack(cv_p), jnp.stack(pl_p),
            batch_first(cv_s), batch_first(pl_s))
```

```python
import functools

import jax
import jax.numpy as jnp
from jax import lax
from jax.experimental import pallas as pl
from jax.experimental.pallas import tpu as pltpu

D_MODEL = 1024
N_MEM = 256
HEADS = 4
HEAD_DIM = 256
CONV_WIDTH = 3
POOL_WINDOWS = (2, 4, 8, 16)
POOL_GROUP = 256
POOL_STATE = 15
PAST_LEN = 16384
EPS = 1e-6
OFF_HC, OFF_BC, OFF_CC, OFF_GC = 0, 1024, 2048, 3072
OFF_HP, OFF_GP = 4096, 5120
OFF_Q = 6144
OFF_GA = 7168
OFF_MERGE = 8192
N_MERGE = 3
N_PROJ = 8
N_CAST_ONLY = N_MERGE

CW = 256
N_CHUNKS = D_MODEL // CW
assert CW == POOL_GROUP == HEAD_DIM
assert POOL_WINDOWS == tuple(2 ** (g + 1) for g in range(len(POOL_WINDOWS)))
TM = 512
TM_OUT = 512
POOL_PAD = 32
CONV_PAD = 8
VMEM_BYTES_V7X = 64 * 1024 * 1024
VMEM_LIMIT = VMEM_BYTES_V7X - 2 * 1024 * 1024

F32 = jnp.float32
BF16 = jnp.bfloat16


def _dot(a, b):
    return jnp.dot(a, b, preferred_element_type=F32)


def _dot_nt(a, b):
    return lax.dot_general(a, b, (((1,), (1,)), ((), ())), preferred_element_type=F32)


def _sigmoid(x):
    return 0.5 * jnp.tanh(0.5 * x) + 0.5


def _silu(x):
    return x * _sigmoid(x)


def _rmsnorm(x, g):
    ms = jnp.mean(x * x, axis=-1, keepdims=True)
    return (x * lax.rsqrt(ms + EPS)) * g


def _softmax(s):
    e = jnp.exp(s - jnp.max(s, axis=-1, keepdims=True))
    return e * (1.0 / jnp.sum(e, axis=-1, keepdims=True))


def _cols(off, c, w=CW):
    return slice(off + c * w, off + (c + 1) * w)


def _params(n_axes):
    return pltpu.CompilerParams(
        dimension_semantics=("arbitrary",) * n_axes, vmem_limit_bytes=VMEM_LIMIT)


def _const_spec(shape):
    nd = len(shape)
    return pl.BlockSpec(shape, lambda *_: (0,) * nd, pipeline_mode=pl.Buffered(1))


def _layer_spec(arr, layer, cols=None, col_block=0):
    tail = list(arr.shape[1:])
    idx = [0] * len(tail)
    if cols is not None:
        tail[-1] = cols
        idx[-1] = col_block
    return pl.BlockSpec((None, *tail), lambda *_: (layer, *idx), pipeline_mode=pl.Buffered(1))


def _kv_kernel(mem_ref, g_ref, w_ref, k_ref, v_ref, kb_ref, vb_ref, wb_s):
    @pl.when(pl.program_id(1) == 0)
    def _():
        wb_s[...] = w_ref[0].astype(BF16)

    h = _rmsnorm(mem_ref[...], g_ref[0]).astype(BF16)
    k = _dot(h, wb_s[:, :D_MODEL])
    v = _dot(h, wb_s[:, D_MODEL:])
    k_ref[0] = k.reshape(k_ref.shape[1:])
    v_ref[0] = v.reshape(v_ref.shape[1:])
    kb_ref[0] = k.astype(BF16)
    vb_ref[0] = v.astype(BF16)


def _mem_kv(mem2d, mem_norm_g, w_kv):
    depth = w_kv.shape[0]
    rows = mem2d.shape[0]
    bt = 2
    rt = bt * N_MEM
    out_f = jax.ShapeDtypeStruct((depth, rows // N_MEM, N_MEM, HEADS, HEAD_DIM), F32)
    out_b = jax.ShapeDtypeStruct((depth, rows, D_MODEL), BF16)
    blk = pl.BlockSpec((1, rt, D_MODEL), lambda l, r: (l, r, 0))
    blk5 = pl.BlockSpec((1, bt, N_MEM, HEADS, HEAD_DIM), lambda l, r: (l, r, 0, 0, 0))
    return pl.pallas_call(
        _kv_kernel,
        grid=(depth, rows // rt),
        in_specs=[
            pl.BlockSpec((rt, D_MODEL), lambda l, r: (r, 0)),
            pl.BlockSpec((1, 1, D_MODEL), lambda l, r: (l, 0, 0)),
            pl.BlockSpec((1, D_MODEL, 2 * D_MODEL), lambda l, r: (l, 0, 0)),
        ],
        out_specs=[blk5, blk5, blk, blk],
        out_shape=[out_f, out_f, out_b, out_b],
        scratch_shapes=[pltpu.VMEM((D_MODEL, 2 * D_MODEL), BF16)],
        compiler_params=_params(2),
        name="mem_kv",
    )(mem2d, mem_norm_g.reshape(depth, 1, D_MODEL), w_kv)


KV_TILE = CW // HEADS
N_KV_TILES = N_MEM // KV_TILE


def _sample_attention_tiles(sq_ref, ssg_ref, sk_ref, sv_ref, saa_ref, sb, n_t):
    n_flat = N_MEM * HEADS
    col_head = lax.broadcasted_iota(jnp.int32, (HEADS * n_t, n_flat), 1) % HEADS
    row_head = lax.broadcasted_iota(jnp.int32, (HEADS * n_t, n_flat), 0) // n_t
    own = col_head == row_head
    for i in range(sb):
        rows = slice(i * n_t, (i + 1) * n_t)
        qb = sq_ref[rows, :]
        qe = jnp.concatenate([qb[:, _cols(0, hh, HEAD_DIM)] for hh in range(HEADS)], axis=0).astype(BF16)
        s = []
        for j in range(N_KV_TILES):
            kf = sk_ref[0, i, j * KV_TILE:(j + 1) * KV_TILE].reshape(CW, HEAD_DIM).astype(BF16)
            s.append(_dot_nt(qe, kf))
            yield
        s = jnp.where(own, jnp.concatenate(s, axis=1) * HEAD_DIM ** -0.5, -jnp.inf)
        p = _softmax(s).astype(BF16)
        o = None
        for j in range(N_KV_TILES):
            vf = sv_ref[0, i, j * KV_TILE:(j + 1) * KV_TILE].reshape(CW, HEAD_DIM).astype(BF16)
            part = _dot(p[:, _cols(0, j)], vf)
            o = part if o is None else o + part
            yield
        o = jnp.concatenate([o[hh * n_t:(hh + 1) * n_t] for hh in range(HEADS)], axis=1)
        saa_ref[rows, :] = o * ssg_ref[rows, :]


def _prompt_in_kernel(x_ref, k_ref, v_ref, sq_ref, ssg_ref, sk_ref, sv_ref,
                      g_ref, whp_ref, wq_ref, whc_ref, wcc_ref, wbc_ref, wgc_ref, wgp_ref, wga_ref,
                      convw_ref, poolw_ref, pscale_ref,
                      ac_ref, ap_ref, aa_ref, saa_ref, cst_ref, pst_ref,
                      u_s, p_s, sa_s, sb_s, *, sb, n_t):
    t = pl.program_id(1)

    @pl.when(t == 0)
    def _():
        u_s[0:CONV_PAD, :] = jnp.zeros((CONV_PAD, D_MODEL), F32)
        p_s[0:POOL_PAD, :] = jnp.zeros((POOL_PAD, D_MODEL), F32)
        sa_s[0:16, :] = jnp.zeros((16, POOL_GROUP), F32)
        sb_s[0:16, :] = jnp.zeros((16, POOL_GROUP), F32)

    sample_attn = _sample_attention_tiles(sq_ref, ssg_ref, sk_ref, sv_ref, saa_ref, sb, n_t)
    for _ in range(N_KV_TILES):
        next(sample_attn, None)
    x = x_ref[0]
    h = _rmsnorm(x, g_ref[...]).astype(BF16)

    def hdot(w_ref, c):
        r = _dot(h, w_ref[:, _cols(0, c)])
        next(sample_attn, None)
        return r

    pos1 = (t * TM + 1 + lax.broadcasted_iota(jnp.int32, (TM, 1), 0)).astype(F32)
    n = TM + 16
    for c, w in enumerate(POOL_WINDOWS):
        sl = _cols(0, c)

        hp = hdot(whp_ref, c)
        p_s[POOL_PAD:POOL_PAD + TM, sl] = hp
        cur = p_s[16:16 + n, sl] + p_s[15:15 + n, sl]
        shift, src, dst = 2, sa_s, sb_s
        while shift < w:
            src[16:16 + n, :] = cur
            cur = src[16:16 + n, :] + src[16 - shift:16 - shift + n, :]
            shift *= 2
            src, dst = dst, src
        inv_cnt = 1.0 / jnp.minimum(pos1, float(w))
        mixed = (cur[16:, :] * inv_cnt - hp).astype(BF16)

        q = hdot(wq_ref, c).astype(BF16)

        hc = hdot(whc_ref, c)
        cc = hdot(wcc_ref, c)
        u = cc * hc
        u_s[CONV_PAD:CONV_PAD + TM, sl] = u
        u1 = u_s[CONV_PAD - 1:CONV_PAD - 1 + TM, sl]
        u2 = u_s[CONV_PAD - 2:CONV_PAD - 2 + TM, sl]
        y = convw_ref[0:1, sl] * u2 + convw_ref[1:2, sl] * u1 + convw_ref[2:3, sl] * u

        p = _softmax(_dot_nt(q, k_ref[:, sl]) * HEAD_DIM ** -0.5).astype(BF16)

        bc = hdot(wbc_ref, c)
        gc = hdot(wgc_ref, c)
        ac_ref[0, :, sl] = (bc * y * _silu(gc)).astype(BF16)

        gp = hdot(wgp_ref, c)
        pooled = _dot(mixed, poolw_ref[c]) * pscale_ref[:, sl]
        ap_ref[0, :, sl] = (pooled * _silu(gp)).astype(BF16)

        ga = hdot(wga_ref, c)
        o = _dot(p, v_ref[:, sl])
        aa_ref[0, :, sl] = (o * _silu(ga)).astype(BF16)

    new_conv = u_s[CONV_PAD + TM - 2:CONV_PAD + TM, :]
    cst_ref[0] = new_conv
    u_s[CONV_PAD - 2:CONV_PAD, :] = new_conv
    pst_ref[0] = p_s[POOL_PAD + TM - POOL_STATE:POOL_PAD + TM, :]
    p_s[16:POOL_PAD, :] = p_s[TM + 16:TM + POOL_PAD, :]

    for _ in sample_attn:
        pass


def _prompt_in(x, kb, vb, sq, ssg, cache_k, cache_v, w_a, sw, layer, n_t):
    b, t, _ = x.shape
    nt = t // TM
    n_b = sq.shape[0] // n_t
    sb = n_b // (b * nt)
    assert sb * b * nt == n_b
    tile = pl.BlockSpec((1, TM, D_MODEL), lambda i, j: (i, j, 0))
    smp = pl.BlockSpec((sb * n_t, D_MODEL), lambda i, j: (i * nt + j, 0))
    skv = pl.BlockSpec((1, sb, N_MEM, HEADS, HEAD_DIM), lambda i, j: (layer, i * nt + j, 0, 0, 0))
    act = jax.ShapeDtypeStruct((b, t, D_MODEL), BF16)
    return pl.pallas_call(
        functools.partial(_prompt_in_kernel, sb=sb, n_t=n_t),
        grid=(b, nt),
        in_specs=[
            tile,
            pl.BlockSpec((None, N_MEM, D_MODEL), lambda i, j: (layer, i, 0)),
            pl.BlockSpec((None, N_MEM, D_MODEL), lambda i, j: (layer, i, 0)),
            smp, smp, skv, skv,
            _layer_spec(sw["g"], layer),
        ] + [_const_spec(w.shape) for w in w_a] + [
            _layer_spec(sw["conv_w"], layer), _layer_spec(sw["pool_w"], layer), _layer_spec(sw["pool_scale"], layer),
        ],
        out_specs=[
            tile, tile, tile, smp,
            pl.BlockSpec((1, CONV_WIDTH - 1, D_MODEL), lambda i, j: (i, 0, 0)),
            pl.BlockSpec((1, POOL_STATE, D_MODEL), lambda i, j: (i, 0, 0)),
        ],
        out_shape=[
            act, act, act,
            jax.ShapeDtypeStruct(sq.shape, F32),
            jax.ShapeDtypeStruct((b, CONV_WIDTH - 1, D_MODEL), F32),
            jax.ShapeDtypeStruct((b, POOL_STATE, D_MODEL), F32),
        ],
        scratch_shapes=[
            pltpu.VMEM((CONV_PAD + TM, D_MODEL), F32),
            pltpu.VMEM((POOL_PAD + TM, D_MODEL), F32),
            pltpu.VMEM((POOL_PAD + TM, POOL_GROUP), F32),
            pltpu.VMEM((POOL_PAD + TM, POOL_GROUP), F32),
        ],
        compiler_params=_params(2),
        name="prompt_in",
    )(x, kb, vb, sq, ssg, cache_k, cache_v, sw["g"], *w_a, sw["conv_w"], sw["pool_w"], sw["pool_scale"])


def _sample_in_kernel(*refs, n_t, n_b):
    refs = list(refs)
    x_ref, tb_ref, bt_ref, cst_ref, pst_ref, g_ref = refs[:6]
    w_refs = refs[6:6 + N_PROJ + N_CAST_ONLY]
    convw_ref, poolw_ref, pscale_ref = refs[6 + N_PROJ + N_CAST_ONLY:9 + N_PROJ + N_CAST_ONLY]
    outs = refs[9 + N_PROJ + N_CAST_ONLY:]
    ac_ref, ap_ref, q_ref, sg_ref, ncst_ref, npst_ref = outs[:6]
    wb_refs = outs[6:6 + N_PROJ + N_CAST_ONLY]
    hb_s, ht_s = outs[6 + N_PROJ + N_CAST_ONLY:]
    c = pl.program_id(0)

    @pl.when(c == 0)
    def _():
        hb = _rmsnorm(x_ref[...], g_ref[...]).astype(BF16)
        hb_s[...] = hb
        ht_s[...] = _dot(tb_ref[...], hb).astype(BF16)

    hb, ht = hb_s[...], ht_s[...]
    w = []
    for w_ref, wb_ref in zip(w_refs, wb_refs):
        w.append(w_ref[...].astype(BF16))
        wb_ref[...] = w[-1]
    whp, wq, whc, wcc, wbc, wgc, wgp, wga = w[:N_PROJ]

    def to_batch_major(a):
        return _dot(bt_ref[...], a.astype(BF16)).astype(BF16)

    def planes(a):
        return [a[i * n_b:(i + 1) * n_b] for i in range(n_t)]

    hp = _dot(ht, whp)
    u = _dot(ht, wcc) * _dot(ht, whc)
    bc = _dot(ht, wbc)
    silu_gc = _silu(_dot(ht, wgc))
    silu_gp = _silu(_dot(ht, wgp))
    q_ref[...] = _dot(hb, wq)
    sg_ref[...] = _silu(_dot(hb, wga))

    ext = [pst_ref[j] for j in range(POOL_STATE)] + planes(hp)
    for j, plane in enumerate(ext[-POOL_STATE:]):
        npst_ref[j] = plane
    wsum = {}

    def window_sum(k, i):
        if k == 0:
            return ext[i]
        if (k, i) not in wsum:
            wsum[(k, i)] = window_sum(k - 1, i) + window_sum(k - 1, i - 2 ** (k - 1))
        return wsum[(k, i)]

    mixed = []
    for i in range(n_t):
        mean = None
        for g, win in enumerate(POOL_WINDOWS):
            cnt = float(min(PAST_LEN + i + 1, win))
            cand = window_sum(g + 1, POOL_STATE + i) * (1.0 / cnt)
            mean = cand if mean is None else jnp.where(c == g, cand, mean)
        mixed.append(mean - ext[POOL_STATE + i])
    mixed = jnp.concatenate(mixed, axis=0).astype(BF16)
    pooled = _dot(mixed, poolw_ref[...].astype(BF16)) * pscale_ref[...]
    ap_ref[...] = to_batch_major(pooled * silu_gp)

    ext = [cst_ref[0], cst_ref[1]] + planes(u)
    y = jnp.concatenate(
        [convw_ref[0:1, :] * ext[i] + convw_ref[1:2, :] * ext[i + 1] + convw_ref[2:3, :] * ext[i + 2]
         for i in range(n_t)], axis=0)
    ac_ref[...] = to_batch_major(bc * y * silu_gc)
    for j, plane in enumerate(ext[-(CONV_WIDTH - 1):]):
        ncst_ref[j] = plane


def _row_permutations(n_t, n_b):
    r = jnp.arange(n_t * n_b)
    tb = jax.nn.one_hot((r % n_b) * n_t + r // n_b, n_t * n_b, dtype=BF16)
    return tb, tb.T


def _sample_in(xs2d, cst_t, pst_t, g, w_in, conv_w, pool_w, pool_scale, layer, n_t, n_b):
    rows = n_t * n_b
    tb, bt = _row_permutations(n_t, n_b)
    offs = (OFF_HP, OFF_Q, OFF_HC, OFF_CC, OFF_BC, OFF_GC, OFF_GP, OFF_GA) + tuple(
        OFF_MERGE + k * D_MODEL for k in range(N_MERGE))
    weights = [w_in] * len(offs)
    blocks = [off // CW for off in offs]
    assert len(weights) == N_PROJ + N_CAST_ONLY
    w_specs = [pl.BlockSpec((None, D_MODEL, CW), functools.partial(lambda c, blk: (layer, 0, blk + c), blk=blk))
               for blk in blocks]
    chunk2d = pl.BlockSpec((rows, CW), lambda c: (0, c))

    def hist(n_rows):
        return pl.BlockSpec((None, n_rows, n_b, CW), lambda c: (layer, 0, 0, c))

    def new_hist(n_rows):
        return pl.BlockSpec((n_rows, n_b, CW), lambda c: (0, 0, c))

    return pl.pallas_call(
        functools.partial(_sample_in_kernel, n_t=n_t, n_b=n_b),
        grid=(N_CHUNKS,),
        in_specs=[
            _const_spec(xs2d.shape), _const_spec(tb.shape), _const_spec(bt.shape),
            hist(CONV_WIDTH - 1), hist(POOL_STATE), _layer_spec(g, layer),
        ] + w_specs + [
            pl.BlockSpec((None, CONV_WIDTH, CW), lambda c: (layer, 0, c)),
            pl.BlockSpec((None, None, POOL_GROUP, POOL_GROUP), lambda c: (layer, c, 0, 0)),
            pl.BlockSpec((None, 1, CW), lambda c: (layer, 0, c)),
        ],
        out_specs=([chunk2d] * 4 + [new_hist(CONV_WIDTH - 1), new_hist(POOL_STATE)]
                   + [pl.BlockSpec((D_MODEL, CW), lambda c: (0, c))] * len(weights)),
        out_shape=[
            jax.ShapeDtypeStruct((rows, D_MODEL), BF16),
            jax.ShapeDtypeStruct((rows, D_MODEL), BF16),
            jax.ShapeDtypeStruct((rows, D_MODEL), F32),
            jax.ShapeDtypeStruct((rows, D_MODEL), F32),
            jax.ShapeDtypeStruct((CONV_WIDTH - 1, n_b, D_MODEL), F32),
            jax.ShapeDtypeStruct((POOL_STATE, n_b, D_MODEL), F32),
        ] + [jax.ShapeDtypeStruct((D_MODEL, D_MODEL), BF16)] * len(weights),
        scratch_shapes=[pltpu.VMEM((rows, D_MODEL), BF16), pltpu.VMEM((rows, D_MODEL), BF16)],
        compiler_params=_params(1),
        name="sample_in",
    )(xs2d, tb, bt, cst_t, pst_t, g, *weights, conv_w, pool_w, pool_scale)


def _out_side_kernel(x_ref, ac_ref, ap_ref, aa_ref, xs_ref, acs_ref, aps_ref, aas_ref,
                     g_ref, wmc_ref, wmp_ref, wma_ref, wbc_ref, wbp_ref, wba_ref, wout_ref, fg_ref,
                     y_ref, ys_ref, m_s, *, final_norm, n_prompt_tiles):
    def tile(x_ref, ac_ref, ap_ref, aa_ref, y_ref):
        rows = x_ref.shape[0]
        x = x_ref[...]
        h = _rmsnorm(x, g_ref[...]).astype(BF16)
        ac, ap, aa = ac_ref[...], ap_ref[...], aa_ref[...].astype(BF16)
        for c in range(N_CHUNKS):
            sl = _cols(0, c)
            conv_br = _dot(ac, wbc_ref[:, sl])
            pool_br = _dot(ap, wbp_ref[:, sl])
            att_br = _dot(aa, wba_ref[:, sl])
            mc = _dot(h, wmc_ref[:, sl])
            mp = _dot(h, wmp_ref[:, sl])
            ma = _dot(h, wma_ref[:, sl])
            merged = _sigmoid(mc) * conv_br + _sigmoid(mp) * pool_br + _sigmoid(ma) * att_br
            m_s[0:rows, sl] = merged.astype(BF16)
        xn = x + _dot(m_s[0:rows, :], wout_ref[...])
        if final_norm:
            xn = _rmsnorm(xn, fg_ref[...])
        y_ref[...] = xn

    r = pl.program_id(0)

    @pl.when(r < n_prompt_tiles)
    def _():
        tile(x_ref, ac_ref, ap_ref, aa_ref, y_ref)

    @pl.when(r == n_prompt_tiles)
    def _():
        tile(xs_ref, acs_ref, aps_ref, aas_ref, ys_ref)


def _out_side(x2d, ac, ap, aa, xs2d, acs, aps, aas, w_merge, sw, layer, final_norm):
    rows, rows_s = x2d.shape[0], xs2d.shape[0]
    n_tiles = rows // TM_OUT
    assert n_tiles * TM_OUT == rows and rows_s <= TM_OUT
    tile = pl.BlockSpec((TM_OUT, D_MODEL), lambda r: (jnp.minimum(r, n_tiles - 1), 0))
    rest = ("wbc", "wbp", "wba", "wout")
    weights = (sw["g"], *w_merge, *(sw[n] for n in rest), sw["fg"])
    w_specs = ([_layer_spec(sw["g"], layer)] + [_const_spec(w.shape) for w in w_merge]
               + [_layer_spec(sw[n], layer) for n in rest] + [_const_spec(sw["fg"].shape)])
    smp = _const_spec((rows_s, D_MODEL))
    return pl.pallas_call(
        functools.partial(_out_side_kernel, final_norm=final_norm, n_prompt_tiles=n_tiles),
        grid=(n_tiles + 1,),
        in_specs=[tile] * 4 + [smp] * 4 + w_specs,
        out_specs=[tile, smp],
        out_shape=[jax.ShapeDtypeStruct((rows, D_MODEL), F32), jax.ShapeDtypeStruct((rows_s, D_MODEL), F32)],
        scratch_shapes=[pltpu.VMEM((TM_OUT, D_MODEL), BF16)],
        compiler_params=_params(1),
        name="out_side",
    )(x2d, ac, ap, aa, xs2d, acs, aps, aas, *weights)


def kernel(x_prompt, x_sample, mem_prompt, cache_mem_k, cache_mem_v, state_conv, state_pool, norm_g, w_in, conv_w, pool_w, pool_scale, mem_norm_g, w_mem_kv, w_br_conv, w_br_pool, w_br_att, w_out, final_norm_g):
    depth = w_in.shape[0]
    b_p, t_p, _ = x_prompt.shape
    n_b, n_t, _ = x_sample.shape
    rows_p, rows_s = b_p * t_p, n_b * n_t

    k_f, v_f, k_b, v_b = _mem_kv(mem_prompt.reshape(b_p * N_MEM, D_MODEL), mem_norm_g, w_mem_kv)

    sw = dict(
        g=norm_g.reshape(depth, 1, D_MODEL), conv_w=conv_w, pool_w=pool_w.astype(BF16),
        pool_scale=pool_scale.reshape(depth, 1, D_MODEL), fg=final_norm_g.reshape(1, D_MODEL),
        wbc=w_br_conv.astype(BF16), wbp=w_br_pool.astype(BF16), wba=w_br_att.astype(BF16),
        wout=w_out.astype(BF16))

    cst_t = jnp.transpose(state_conv, (0, 2, 1, 3))
    pst_t = jnp.transpose(state_pool, (0, 2, 1, 3))

    xp, xs = x_prompt, x_sample.reshape(rows_s, D_MODEL)
    cv_p, pl_p, cv_s, pl_s = [], [], [], []
    flat = lambda a: a.reshape(rows_p, D_MODEL)
    for l in range(depth):
        ac_s, ap_s, q, sg, c_new_s, p_new_s, *w_bf16 = _sample_in(
            xs, cst_t, pst_t, sw["g"], w_in, conv_w, pool_w, sw["pool_scale"], l, n_t, n_b)
        cv_s.append(c_new_s)
        pl_s.append(p_new_s)

        ac_p, ap_p, aa_p, aa_s, c_new, p_new = _prompt_in(
            xp, k_b, v_b, q, sg, cache_mem_k, cache_mem_v, w_bf16[:N_PROJ], sw, l, n_t)
        cv_p.append(c_new)
        pl_p.append(p_new)

        xp, xs = _out_side(flat(xp), flat(ac_p), flat(ap_p), flat(aa_p), xs, ac_s, ap_s, aa_s,
                           w_bf16[N_PROJ:], sw, l, l == depth - 1)
        xp = xp.reshape(b_p, t_p, D_MODEL)

    batch_first = lambda hist: jnp.transpose(jnp.stack(hist), (0, 2, 1, 3))
    return (xp, xs.reshape(n_b, n_t, D_MODEL), k_f, v_f, jnp.stack(cv_p), jnp.stack(pl_p),
            batch_first(cv_s), batch_first(pl_s))
```

```python
import functools

import jax
import jax.numpy as jnp
from jax import lax
from jax.experimental import pallas as pl
from jax.experimental.pallas import tpu as pltpu

D_MODEL = 1024
N_MEM = 256
HEADS = 4
HEAD_DIM = 256
CONV_WIDTH = 3
POOL_WINDOWS = (2, 4, 8, 16)
POOL_GROUP = 256
POOL_STATE = 15
PAST_LEN = 16384
EPS = 1e-6
OFF_HC, OFF_BC, OFF_CC, OFF_GC = 0, 1024, 2048, 3072
OFF_HP, OFF_GP = 4096, 5120
OFF_Q = 6144
OFF_GA = 7168
OFF_MERGE = 8192
N_MERGE = 3
N_PROJ = 8
N_CAST_ONLY = N_MERGE

CW = 256
N_CHUNKS = D_MODEL // CW
assert CW == POOL_GROUP == HEAD_DIM
assert POOL_WINDOWS == tuple(2 ** (g + 1) for g in range(len(POOL_WINDOWS)))
TM = 512
TM_OUT = 512
POOL_PAD = 32
CONV_PAD = 8
VMEM_BYTES_V7X = 64 * 1024 * 1024
VMEM_LIMIT = VMEM_BYTES_V7X - 2 * 1024 * 1024

F32 = jnp.float32
BF16 = jnp.bfloat16


def _dot(a, b):
    return jnp.dot(a, b, preferred_element_type=F32)


def _dot_nt(a, b):
    return lax.dot_general(a, b, (((1,), (1,)), ((), ())), preferred_element_type=F32)


def _sigmoid(x):
    return 0.5 * jnp.tanh(0.5 * x) + 0.5


def _silu(x):
    return x * _sigmoid(x)


def _rmsnorm(x, g):
    ms = jnp.mean(x * x, axis=-1, keepdims=True)
    return (x * lax.rsqrt(ms + EPS)) * g


def _softmax(s):
    e = jnp.exp(s - jnp.max(s, axis=-1, keepdims=True))
    return e * (1.0 / jnp.sum(e, axis=-1, keepdims=True))


def _cols(off, c, w=CW):
    return slice(off + c * w, off + (c + 1) * w)


def _params(n_axes):
    return pltpu.CompilerParams(
        dimension_semantics=("arbitrary",) * n_axes, vmem_limit_bytes=VMEM_LIMIT)


def _const_spec(shape):
    nd = len(shape)
    return pl.BlockSpec(shape, lambda *_: (0,) * nd, pipeline_mode=pl.Buffered(1))


def _layer_spec(arr, layer, cols=None, col_block=0):
    tail = list(arr.shape[1:])
    idx = [0] * len(tail)
    if cols is not None:
        tail[-1] = cols
        idx[-1] = col_block
    return pl.BlockSpec((None, *tail), lambda *_: (layer, *idx), pipeline_mode=pl.Buffered(1))


def _kv_kernel(mem_ref, g_ref, w_ref, k_ref, v_ref, kb_ref, vb_ref):
    h = _rmsnorm(mem_ref[...], g_ref[0]).astype(BF16)
    k = _dot(h, w_ref[0, :, :D_MODEL].astype(BF16))
    v = _dot(h, w_ref[0, :, D_MODEL:].astype(BF16))
    k_ref[0] = k.reshape(k_ref.shape[1:])
    v_ref[0] = v.reshape(v_ref.shape[1:])
    kb_ref[0] = k.astype(BF16)
    vb_ref[0] = v.astype(BF16)


def _mem_kv(mem2d, mem_norm_g, w_kv):
    depth = w_kv.shape[0]
    rows = mem2d.shape[0]
    bt = 2
    rt = bt * N_MEM
    out_f = jax.ShapeDtypeStruct((depth, rows // N_MEM, N_MEM, HEADS, HEAD_DIM), F32)
    out_b = jax.ShapeDtypeStruct((depth, rows, D_MODEL), BF16)
    blk = pl.BlockSpec((1, rt, D_MODEL), lambda l, r: (l, r, 0))
    blk5 = pl.BlockSpec((1, bt, N_MEM, HEADS, HEAD_DIM), lambda l, r: (l, r, 0, 0, 0))
    return pl.pallas_call(
        _kv_kernel,
        grid=(depth, rows // rt),
        in_specs=[
            pl.BlockSpec((rt, D_MODEL), lambda l, r: (r, 0)),
            pl.BlockSpec((1, 1, D_MODEL), lambda l, r: (l, 0, 0)),
            pl.BlockSpec((1, D_MODEL, 2 * D_MODEL), lambda l, r: (l, 0, 0)),
        ],
        out_specs=[blk5, blk5, blk, blk],
        out_shape=[out_f, out_f, out_b, out_b],
        compiler_params=_params(2),
        name="mem_kv",
    )(mem2d, mem_norm_g.reshape(depth, 1, D_MODEL), w_kv)


KV_TILE = CW // HEADS
N_KV_TILES = N_MEM // KV_TILE


def _sample_attention_tiles(sq_ref, ssg_ref, sk_ref, sv_ref, saa_ref, sb, n_t):
    n_flat = N_MEM * HEADS
    col_head = lax.broadcasted_iota(jnp.int32, (HEADS * n_t, n_flat), 1) % HEADS
    row_head = lax.broadcasted_iota(jnp.int32, (HEADS * n_t, n_flat), 0) // n_t
    own = col_head == row_head
    for i in range(sb):
        rows = slice(i * n_t, (i + 1) * n_t)
        qb = sq_ref[rows, :]
        qe = jnp.concatenate([qb[:, _cols(0, hh, HEAD_DIM)] for hh in range(HEADS)], axis=0).astype(BF16)
        s = []
        for j in range(N_KV_TILES):
            kf = sk_ref[0, i, j * KV_TILE:(j + 1) * KV_TILE].reshape(CW, HEAD_DIM).astype(BF16)
            s.append(_dot_nt(qe, kf))
            yield
        s = jnp.where(own, jnp.concatenate(s, axis=1) * HEAD_DIM ** -0.5, -jnp.inf)
        p = _softmax(s).astype(BF16)
        o = None
        for j in range(N_KV_TILES):
            vf = sv_ref[0, i, j * KV_TILE:(j + 1) * KV_TILE].reshape(CW, HEAD_DIM).astype(BF16)
            part = _dot(p[:, _cols(0, j)], vf)
            o = part if o is None else o + part
            yield
        o = jnp.concatenate([o[hh * n_t:(hh + 1) * n_t] for hh in range(HEADS)], axis=1)
        saa_ref[rows, :] = o * ssg_ref[rows, :]


def _prompt_in_kernel(x_ref, k_ref, v_ref, sq_ref, ssg_ref, sk_ref, sv_ref,
                      g_ref, whp_ref, wq_ref, whc_ref, wcc_ref, wbc_ref, wgc_ref, wgp_ref, wga_ref,
                      convw_ref, poolw_ref, pscale_ref,
                      ac_ref, ap_ref, aa_ref, saa_ref, cst_ref, pst_ref,
                      u_s, p_s, sa_s, sb_s, *, sb, n_t):
    t = pl.program_id(1)

    @pl.when(t == 0)
    def _():
        u_s[0:CONV_PAD, :] = jnp.zeros((CONV_PAD, D_MODEL), F32)
        p_s[0:POOL_PAD, :] = jnp.zeros((POOL_PAD, D_MODEL), F32)
        sa_s[0:16, :] = jnp.zeros((16, POOL_GROUP), F32)
        sb_s[0:16, :] = jnp.zeros((16, POOL_GROUP), F32)

    sample_attn = _sample_attention_tiles(sq_ref, ssg_ref, sk_ref, sv_ref, saa_ref, sb, n_t)
    for _ in range(N_KV_TILES):
        next(sample_attn, None)
    x = x_ref[0]
    h = _rmsnorm(x, g_ref[...]).astype(BF16)

    def hdot(w_ref, c):
        r = _dot(h, w_ref[:, _cols(0, c)])
        next(sample_attn, None)
        return r

    pos1 = (t * TM + 1 + lax.broadcasted_iota(jnp.int32, (TM, 1), 0)).astype(F32)
    n = TM + 16
    for c, w in enumerate(POOL_WINDOWS):
        sl = _cols(0, c)

        hp = hdot(whp_ref, c)
        p_s[POOL_PAD:POOL_PAD + TM, sl] = hp
        cur = p_s[16:16 + n, sl] + p_s[15:15 + n, sl]
        shift, src, dst = 2, sa_s, sb_s
        while shift < w:
            src[16:16 + n, :] = cur
            cur = src[16:16 + n, :] + src[16 - shift:16 - shift + n, :]
            shift *= 2
            src, dst = dst, src
        inv_cnt = 1.0 / jnp.minimum(pos1, float(w))
        mixed = (cur[16:, :] * inv_cnt - hp).astype(BF16)

        q = hdot(wq_ref, c).astype(BF16)

        hc = hdot(whc_ref, c)
        cc = hdot(wcc_ref, c)
        u = cc * hc
        u_s[CONV_PAD:CONV_PAD + TM, sl] = u
        u1 = u_s[CONV_PAD - 1:CONV_PAD - 1 + TM, sl]
        u2 = u_s[CONV_PAD - 2:CONV_PAD - 2 + TM, sl]
        y = convw_ref[0:1, sl] * u2 + convw_ref[1:2, sl] * u1 + convw_ref[2:3, sl] * u

        p = _softmax(_dot_nt(q, k_ref[:, sl]) * HEAD_DIM ** -0.5).astype(BF16)

        bc = hdot(wbc_ref, c)
        gc = hdot(wgc_ref, c)
        ac_ref[0, :, sl] = (bc * y * _silu(gc)).astype(BF16)

        gp = hdot(wgp_ref, c)
        pooled = _dot(mixed, poolw_ref[c]) * pscale_ref[:, sl]
        ap_ref[0, :, sl] = (pooled * _silu(gp)).astype(BF16)

        ga = hdot(wga_ref, c)
        o = _dot(p, v_ref[:, sl])
        aa_ref[0, :, sl] = (o * _silu(ga)).astype(BF16)

    new_conv = u_s[CONV_PAD + TM - 2:CONV_PAD + TM, :]
    cst_ref[0] = new_conv
    u_s[CONV_PAD - 2:CONV_PAD, :] = new_conv
    pst_ref[0] = p_s[POOL_PAD + TM - POOL_STATE:POOL_PAD + TM, :]
    p_s[16:POOL_PAD, :] = p_s[TM + 16:TM + POOL_PAD, :]

    for _ in sample_attn:
        pass


def _prompt_in(x, kb, vb, sq, ssg, cache_k, cache_v, w_a, sw, layer, n_t):
    b, t, _ = x.shape
    nt = t // TM
    n_b = sq.shape[0] // n_t
    sb = n_b // (b * nt)
    assert sb * b * nt == n_b
    tile = pl.BlockSpec((1, TM, D_MODEL), lambda i, j: (i, j, 0))
    smp = pl.BlockSpec((sb * n_t, D_MODEL), lambda i, j: (i * nt + j, 0))
    skv = pl.BlockSpec((1, sb, N_MEM, HEADS, HEAD_DIM), lambda i, j: (layer, i * nt + j, 0, 0, 0))
    act = jax.ShapeDtypeStruct((b, t, D_MODEL), BF16)
    return pl.pallas_call(
        functools.partial(_prompt_in_kernel, sb=sb, n_t=n_t),
        grid=(b, nt),
        in_specs=[
            tile,
            pl.BlockSpec((None, N_MEM, D_MODEL), lambda i, j: (layer, i, 0)),
            pl.BlockSpec((None, N_MEM, D_MODEL), lambda i, j: (layer, i, 0)),
            smp, smp, skv, skv,
            _layer_spec(sw["g"], layer),
        ] + [_const_spec(w.shape) for w in w_a] + [
            _layer_spec(sw["conv_w"], layer), _layer_spec(sw["pool_w"], layer), _layer_spec(sw["pool_scale"], layer),
        ],
        out_specs=[
            tile, tile, tile, smp,
            pl.BlockSpec((1, CONV_WIDTH - 1, D_MODEL), lambda i, j: (i, 0, 0)),
            pl.BlockSpec((1, POOL_STATE, D_MODEL), lambda i, j: (i, 0, 0)),
        ],
        out_shape=[
            act, act, act,
            jax.ShapeDtypeStruct(sq.shape, F32),
            jax.ShapeDtypeStruct((b, CONV_WIDTH - 1, D_MODEL), F32),
            jax.ShapeDtypeStruct((b, POOL_STATE, D_MODEL), F32),
        ],
        scratch_shapes=[
            pltpu.VMEM((CONV_PAD + TM, D_MODEL), F32),
            pltpu.VMEM((POOL_PAD + TM, D_MODEL), F32),
            pltpu.VMEM((POOL_PAD + TM, POOL_GROUP), F32),
            pltpu.VMEM((POOL_PAD + TM, POOL_GROUP), F32),
        ],
        compiler_params=_params(2),
        name="prompt_in",
    )(x, kb, vb, sq, ssg, cache_k, cache_v, sw["g"], *w_a, sw["conv_w"], sw["pool_w"], sw["pool_scale"])


def _sample_in_kernel(*refs, n_t, n_b, cast_weights):
    refs = list(refs)
    n_w = N_PROJ + N_CAST_ONLY if cast_weights else N_PROJ
    x_ref, tb_ref, bt_ref, cst_ref, pst_ref, g_ref = refs[:6]
    w_refs = refs[6:6 + n_w]
    convw_ref, poolw_ref, pscale_ref = refs[6 + n_w:9 + n_w]
    outs = refs[9 + n_w:]
    ac_ref, ap_ref, q_ref, sg_ref, ncst_ref, npst_ref = outs[:6]
    wb_refs = outs[6:-2]
    hb_s, ht_s = outs[-2:]
    c = pl.program_id(0)

    @pl.when(c == 0)
    def _():
        hb = _rmsnorm(x_ref[...], g_ref[...]).astype(BF16)
        hb_s[...] = hb
        ht_s[...] = _dot(tb_ref[...], hb).astype(BF16)

    hb, ht = hb_s[...], ht_s[...]
    w = [w_ref[...].astype(BF16) for w_ref in w_refs]
    for w_cast, wb_ref in zip(w, wb_refs):
        wb_ref[...] = w_cast
    whp, wq, whc, wcc, wbc, wgc, wgp, wga = w[:N_PROJ]

    def to_batch_major(a):
        return _dot(bt_ref[...], a.astype(BF16)).astype(BF16)

    def planes(a):
        return [a[i * n_b:(i + 1) * n_b] for i in range(n_t)]

    hp = _dot(ht, whp)
    u = _dot(ht, wcc) * _dot(ht, whc)
    bc = _dot(ht, wbc)
    silu_gc = _silu(_dot(ht, wgc))
    silu_gp = _silu(_dot(ht, wgp))
    q_ref[...] = _dot(hb, wq)
    sg_ref[...] = _silu(_dot(hb, wga))

    ext = [pst_ref[j] for j in range(POOL_STATE)] + planes(hp)
    for j, plane in enumerate(ext[-POOL_STATE:]):
        npst_ref[j] = plane
    wsum = {}

    def window_sum(k, i):
        if k == 0:
            return ext[i]
        if (k, i) not in wsum:
            wsum[(k, i)] = window_sum(k - 1, i) + window_sum(k - 1, i - 2 ** (k - 1))
        return wsum[(k, i)]

    mixed = []
    for i in range(n_t):
        mean = None
        for g, win in enumerate(POOL_WINDOWS):
            cnt = float(min(PAST_LEN + i + 1, win))
            cand = window_sum(g + 1, POOL_STATE + i) * (1.0 / cnt)
            mean = cand if mean is None else jnp.where(c == g, cand, mean)
        mixed.append(mean - ext[POOL_STATE + i])
    mixed = jnp.concatenate(mixed, axis=0).astype(BF16)
    pooled = _dot(mixed, poolw_ref[...].astype(BF16)) * pscale_ref[...]
    ap_ref[...] = to_batch_major(pooled * silu_gp)

    ext = [cst_ref[0], cst_ref[1]] + planes(u)
    y = jnp.concatenate(
        [convw_ref[0:1, :] * ext[i] + convw_ref[1:2, :] * ext[i + 1] + convw_ref[2:3, :] * ext[i + 2]
         for i in range(n_t)], axis=0)
    ac_ref[...] = to_batch_major(bc * y * silu_gc)
    for j, plane in enumerate(ext[-(CONV_WIDTH - 1):]):
        ncst_ref[j] = plane


def _row_permutations(n_t, n_b):
    r = jnp.arange(n_t * n_b)
    tb = jax.nn.one_hot((r % n_b) * n_t + r // n_b, n_t * n_b, dtype=BF16)
    return tb, tb.T


W_IN_BLOCK_OFFSETS = (OFF_HP, OFF_Q, OFF_HC, OFF_CC, OFF_BC, OFF_GC, OFF_GP, OFF_GA) + tuple(
    OFF_MERGE + k * D_MODEL for k in range(N_MERGE))
assert len(W_IN_BLOCK_OFFSETS) == N_PROJ + N_CAST_ONLY


def _sample_in(xs2d, cst_t, pst_t, g, w_in, w_bf16, conv_w, pool_w, pool_scale, layer, n_t, n_b):
    rows = n_t * n_b
    tb, bt = _row_permutations(n_t, n_b)
    cast_weights = w_bf16 is None
    if cast_weights:
        weights = [w_in] * len(W_IN_BLOCK_OFFSETS)
        w_specs = [pl.BlockSpec((None, D_MODEL, CW), functools.partial(lambda c, blk: (layer, 0, blk + c), blk=off // CW))
                   for off in W_IN_BLOCK_OFFSETS]
    else:
        weights = list(w_bf16[:N_PROJ])
        w_specs = [pl.BlockSpec((D_MODEL, CW), lambda c: (0, c))] * N_PROJ
    n_emit = len(weights) if cast_weights else 0
    chunk2d = pl.BlockSpec((rows, CW), lambda c: (0, c))

    def hist(n_rows):
        return pl.BlockSpec((None, n_rows, n_b, CW), lambda c: (layer, 0, 0, c))

    def new_hist(n_rows):
        return pl.BlockSpec((n_rows, n_b, CW), lambda c: (0, 0, c))

    outs = pl.pallas_call(
        functools.partial(_sample_in_kernel, n_t=n_t, n_b=n_b, cast_weights=cast_weights),
        grid=(N_CHUNKS,),
        in_specs=[
            _const_spec(xs2d.shape), _const_spec(tb.shape), _const_spec(bt.shape),
            hist(CONV_WIDTH - 1), hist(POOL_STATE), _layer_spec(g, layer),
        ] + w_specs + [
            pl.BlockSpec((None, CONV_WIDTH, CW), lambda c: (layer, 0, c)),
            pl.BlockSpec((None, None, POOL_GROUP, POOL_GROUP), lambda c: (layer, c, 0, 0)),
            pl.BlockSpec((None, 1, CW), lambda c: (layer, 0, c)),
        ],
        out_specs=([chunk2d] * 4 + [new_hist(CONV_WIDTH - 1), new_hist(POOL_STATE)]
                   + [pl.BlockSpec((D_MODEL, CW), lambda c: (0, c))] * n_emit),
        out_shape=[
            jax.ShapeDtypeStruct((rows, D_MODEL), BF16),
            jax.ShapeDtypeStruct((rows, D_MODEL), BF16),
            jax.ShapeDtypeStruct((rows, D_MODEL), F32),
            jax.ShapeDtypeStruct((rows, D_MODEL), F32),
            jax.ShapeDtypeStruct((CONV_WIDTH - 1, n_b, D_MODEL), F32),
            jax.ShapeDtypeStruct((POOL_STATE, n_b, D_MODEL), F32),
        ] + [jax.ShapeDtypeStruct((D_MODEL, D_MODEL), BF16)] * n_emit,
        scratch_shapes=[pltpu.VMEM((rows, D_MODEL), BF16), pltpu.VMEM((rows, D_MODEL), BF16)],
        compiler_params=_params(1),
        name="sample_in",
    )(xs2d, tb, bt, cst_t, pst_t, g, *weights, conv_w, pool_w, pool_scale)
    return tuple(outs[:6]) + (list(outs[6:]) if cast_weights else list(w_bf16),)


def _out_side_kernel(*refs, final_norm, n_prompt_tiles, n_cast):
    refs = list(refs)
    (x_ref, ac_ref, ap_ref, aa_ref, xs_ref, acs_ref, aps_ref, aas_ref,
     g_ref, wmc_ref, wmp_ref, wma_ref, wbc_ref, wbp_ref, wba_ref, wout_ref, fg_ref) = refs[:17]
    next_w_refs = refs[17:17 + n_cast]
    y_ref, ys_ref = refs[17 + n_cast:19 + n_cast]
    next_wb_refs = refs[19 + n_cast:19 + 2 * n_cast]
    m_s = refs[-1]

    def tile(x_ref, ac_ref, ap_ref, aa_ref, y_ref):
        rows = x_ref.shape[0]
        x = x_ref[...]
        h = _rmsnorm(x, g_ref[...]).astype(BF16)
        ac, ap, aa = ac_ref[...], ap_ref[...], aa_ref[...].astype(BF16)
        for c in range(N_CHUNKS):
            sl = _cols(0, c)
            conv_br = _dot(ac, wbc_ref[:, sl])
            pool_br = _dot(ap, wbp_ref[:, sl])
            att_br = _dot(aa, wba_ref[:, sl])
            mc = _dot(h, wmc_ref[:, sl])
            mp = _dot(h, wmp_ref[:, sl])
            ma = _dot(h, wma_ref[:, sl])
            merged = _sigmoid(mc) * conv_br + _sigmoid(mp) * pool_br + _sigmoid(ma) * att_br
            m_s[0:rows, sl] = merged.astype(BF16)
        xn = x + _dot(m_s[0:rows, :], wout_ref[...])
        if final_norm:
            xn = _rmsnorm(xn, fg_ref[...])
        y_ref[...] = xn

    r = pl.program_id(0)

    @pl.when(r < n_prompt_tiles)
    def _():
        for w_ref, wb_ref in zip(next_w_refs, next_wb_refs):
            wb_ref[...] = w_ref[...].astype(BF16)
        tile(x_ref, ac_ref, ap_ref, aa_ref, y_ref)

    @pl.when(r == n_prompt_tiles)
    def _():
        tile(xs_ref, acs_ref, aps_ref, aas_ref, ys_ref)


def _out_side(x2d, ac, ap, aa, xs2d, acs, aps, aas, w_merge, sw, layer, final_norm, next_w_in=None):
    rows, rows_s = x2d.shape[0], xs2d.shape[0]
    n_tiles = rows // TM_OUT
    assert n_tiles * TM_OUT == rows and rows_s <= TM_OUT
    n_cast = 0 if next_w_in is None else len(W_IN_BLOCK_OFFSETS)
    cast_rows = D_MODEL // n_tiles
    assert cast_rows * n_tiles == D_MODEL
    cast_in = [pl.BlockSpec((None, cast_rows, D_MODEL),
                            functools.partial(lambda r, blk: (layer + 1, jnp.minimum(r, n_tiles - 1), blk),
                                              blk=off // D_MODEL))
               for off in W_IN_BLOCK_OFFSETS[:n_cast]]
    cast_out = [pl.BlockSpec((cast_rows, D_MODEL), lambda r: (jnp.minimum(r, n_tiles - 1), 0))] * n_cast
    tile = pl.BlockSpec((TM_OUT, D_MODEL), lambda r: (jnp.minimum(r, n_tiles - 1), 0))
    rest = ("wbc", "wbp", "wba", "wout")
    weights = (sw["g"], *w_merge, *(sw[n] for n in rest), sw["fg"])
    w_specs = ([_layer_spec(sw["g"], layer)] + [_const_spec(w.shape) for w in w_merge]
               + [_layer_spec(sw[n], layer) for n in rest] + [_const_spec(sw["fg"].shape)])
    smp = _const_spec((rows_s, D_MODEL))
    y, ys, *next_w_bf16 = pl.pallas_call(
        functools.partial(_out_side_kernel, final_norm=final_norm, n_prompt_tiles=n_tiles, n_cast=n_cast),
        grid=(n_tiles + 1,),
        in_specs=[tile] * 4 + [smp] * 4 + w_specs + cast_in,
        out_specs=[tile, smp] + cast_out,
        out_shape=([jax.ShapeDtypeStruct((rows, D_MODEL), F32), jax.ShapeDtypeStruct((rows_s, D_MODEL), F32)]
                   + [jax.ShapeDtypeStruct((D_MODEL, D_MODEL), BF16)] * n_cast),
        scratch_shapes=[pltpu.VMEM((TM_OUT, D_MODEL), BF16)],
        compiler_params=_params(1),
        name="out_side",
    )(x2d, ac, ap, aa, xs2d, acs, aps, aas, *weights, *([next_w_in] * n_cast))
    return y, ys, (next_w_bf16 or None)


def kernel(x_prompt, x_sample, mem_prompt, cache_mem_k, cache_mem_v, state_conv, state_pool, norm_g, w_in, conv_w, pool_w, pool_scale, mem_norm_g, w_mem_kv, w_br_conv, w_br_pool, w_br_att, w_out, final_norm_g):
    depth = w_in.shape[0]
    b_p, t_p, _ = x_prompt.shape
    n_b, n_t, _ = x_sample.shape
    rows_p, rows_s = b_p * t_p, n_b * n_t

    k_f, v_f, k_b, v_b = _mem_kv(mem_prompt.reshape(b_p * N_MEM, D_MODEL), mem_norm_g, w_mem_kv)

    sw = dict(
        g=norm_g.reshape(depth, 1, D_MODEL), conv_w=conv_w, pool_w=pool_w.astype(BF16),
        pool_scale=pool_scale.reshape(depth, 1, D_MODEL), fg=final_norm_g.reshape(1, D_MODEL),
        wbc=w_br_conv.astype(BF16), wbp=w_br_pool.astype(BF16), wba=w_br_att.astype(BF16),
        wout=w_out.astype(BF16))

    cst_t = jnp.transpose(state_conv, (0, 2, 1, 3))
    pst_t = jnp.transpose(state_pool, (0, 2, 1, 3))

    xp, xs = x_prompt, x_sample.reshape(rows_s, D_MODEL)
    cv_p, pl_p, cv_s, pl_s = [], [], [], []
    flat = lambda a: a.reshape(rows_p, D_MODEL)
    w_bf16 = None
    for l in range(depth):
        ac_s, ap_s, q, sg, c_new_s, p_new_s, w_bf16 = _sample_in(
            xs, cst_t, pst_t, sw["g"], w_in, w_bf16, conv_w, pool_w, sw["pool_scale"], l, n_t, n_b)
        cv_s.append(c_new_s)
        pl_s.append(p_new_s)

        ac_p, ap_p, aa_p, aa_s, c_new, p_new = _prompt_in(
            xp, k_b, v_b, q, sg, cache_mem_k, cache_mem_v, w_bf16[:N_PROJ], sw, l, n_t)
        cv_p.append(c_new)
        pl_p.append(p_new)

        last = l == depth - 1
        xp, xs, w_bf16 = _out_side(flat(xp), flat(ac_p), flat(ap_p), flat(aa_p), xs, ac_s, ap_s, aa_s,
                                   w_bf16[N_PROJ:], sw, l, last, None if last else w_in)
        xp = xp.reshape(b_p, t_p, D_MODEL)

    batch_first = lambda hist: jnp.transpose(jnp.stack(hist), (0, 2, 1, 3))
    return (xp, xs.reshape(n_b, n_t, D_MODEL), k_f, v_f, jnp.stack(cv_p), jnp.stack(pl_p),
            batch_first(cv_s), batch_first(pl_s))
```

```python
import functools

import jax
import jax.numpy as jnp
from jax import lax
from jax.experimental import pallas as pl
from jax.experimental.pallas import tpu as pltpu

D_MODEL = 1024
N_MEM = 256
HEADS = 4
HEAD_DIM = 256
CONV_WIDTH = 3
POOL_WINDOWS = (2, 4, 8, 16)
POOL_GROUP = 256
POOL_STATE = 15
PAST_LEN = 16384
EPS = 1e-6
OFF_HC, OFF_BC, OFF_CC, OFF_GC = 0, 1024, 2048, 3072
OFF_HP, OFF_GP = 4096, 5120
OFF_Q = 6144
OFF_GA = 7168
OFF_MERGE = 8192
N_MERGE = 3
N_PROJ = 8
N_CAST_ONLY = N_MERGE

CW = 256
N_CHUNKS = D_MODEL // CW
assert CW == POOL_GROUP == HEAD_DIM
assert POOL_WINDOWS == tuple(2 ** (g + 1) for g in range(len(POOL_WINDOWS)))
TM = 512
TM_OUT = 512
POOL_PAD = 32
CONV_PAD = 8
VMEM_BYTES_V7X = 64 * 1024 * 1024
VMEM_LIMIT = VMEM_BYTES_V7X - 2 * 1024 * 1024

F32 = jnp.float32
BF16 = jnp.bfloat16


def _dot(a, b):
    return jnp.dot(a, b, preferred_element_type=F32)


def _dot_nt(a, b):
    return lax.dot_general(a, b, (((1,), (1,)), ((), ())), preferred_element_type=F32)


def _sigmoid(x):
    return 0.5 * jnp.tanh(0.5 * x) + 0.5


def _silu(x):
    return x * _sigmoid(x)


def _rmsnorm(x, g):
    ms = jnp.mean(x * x, axis=-1, keepdims=True)
    return (x * lax.rsqrt(ms + EPS)) * g


def _softmax(s):
    e = jnp.exp(s - jnp.max(s, axis=-1, keepdims=True))
    return e * (1.0 / jnp.sum(e, axis=-1, keepdims=True))


def _cols(off, c, w=CW):
    return slice(off + c * w, off + (c + 1) * w)


def _params(n_axes):
    return pltpu.CompilerParams(
        dimension_semantics=("arbitrary",) * n_axes, vmem_limit_bytes=VMEM_LIMIT)


def _const_spec(shape):
    nd = len(shape)
    return pl.BlockSpec(shape, lambda *_: (0,) * nd, pipeline_mode=pl.Buffered(1))


def _layer_spec(arr, layer, cols=None, col_block=0):
    tail = list(arr.shape[1:])
    idx = [0] * len(tail)
    if cols is not None:
        tail[-1] = cols
        idx[-1] = col_block
    return pl.BlockSpec((None, *tail), lambda *_: (layer, *idx), pipeline_mode=pl.Buffered(1))


def _kv_kernel(mem_ref, g_ref, w_ref, k_ref, v_ref, kb_ref, vb_ref):
    h = _rmsnorm(mem_ref[...], g_ref[0]).astype(BF16)
    k = _dot(h, w_ref[0, :, :D_MODEL].astype(BF16))
    v = _dot(h, w_ref[0, :, D_MODEL:].astype(BF16))
    k_ref[0] = k.reshape(k_ref.shape[1:])
    v_ref[0] = v.reshape(v_ref.shape[1:])
    kb_ref[0] = k.astype(BF16)
    vb_ref[0] = v.astype(BF16)


def _mem_kv(mem2d, mem_norm_g, w_kv):
    depth = w_kv.shape[0]
    rows = mem2d.shape[0]
    bt = 2
    rt = bt * N_MEM
    out_f = jax.ShapeDtypeStruct((depth, rows // N_MEM, N_MEM, HEADS, HEAD_DIM), F32)
    out_b = jax.ShapeDtypeStruct((depth, rows, D_MODEL), BF16)
    blk = pl.BlockSpec((1, rt, D_MODEL), lambda l, r: (l, r, 0))
    blk5 = pl.BlockSpec((1, bt, N_MEM, HEADS, HEAD_DIM), lambda l, r: (l, r, 0, 0, 0))
    return pl.pallas_call(
        _kv_kernel,
        grid=(depth, rows // rt),
        in_specs=[
            pl.BlockSpec((rt, D_MODEL), lambda l, r: (r, 0)),
            pl.BlockSpec((1, 1, D_MODEL), lambda l, r: (l, 0, 0)),
            pl.BlockSpec((1, D_MODEL, 2 * D_MODEL), lambda l, r: (l, 0, 0)),
        ],
        out_specs=[blk5, blk5, blk, blk],
        out_shape=[out_f, out_f, out_b, out_b],
        compiler_params=_params(2),
        name="mem_kv",
    )(mem2d, mem_norm_g.reshape(depth, 1, D_MODEL), w_kv)


KV_TILE = CW // HEADS
N_KV_TILES = N_MEM // KV_TILE


def _sample_attention_tiles(sq_ref, ssg_ref, sk_ref, sv_ref, saa_ref, sb, n_t):
    n_flat = N_MEM * HEADS
    col_head = lax.broadcasted_iota(jnp.int32, (HEADS * n_t, n_flat), 1) % HEADS
    row_head = lax.broadcasted_iota(jnp.int32, (HEADS * n_t, n_flat), 0) // n_t
    own = col_head == row_head
    for i in range(sb):
        rows = slice(i * n_t, (i + 1) * n_t)
        qb = sq_ref[rows, :]
        qe = jnp.concatenate([qb[:, _cols(0, hh, HEAD_DIM)] for hh in range(HEADS)], axis=0).astype(BF16)
        s = []
        for j in range(N_KV_TILES):
            kf = sk_ref[0, i, j * KV_TILE:(j + 1) * KV_TILE].reshape(CW, HEAD_DIM).astype(BF16)
            s.append(_dot_nt(qe, kf))
            yield
        s = jnp.where(own, jnp.concatenate(s, axis=1) * HEAD_DIM ** -0.5, -jnp.inf)
        p = _softmax(s).astype(BF16)
        o = None
        for j in range(N_KV_TILES):
            vf = sv_ref[0, i, j * KV_TILE:(j + 1) * KV_TILE].reshape(CW, HEAD_DIM).astype(BF16)
            part = _dot(p[:, _cols(0, j)], vf)
            o = part if o is None else o + part
            yield
        o = jnp.concatenate([o[hh * n_t:(hh + 1) * n_t] for hh in range(HEADS)], axis=1)
        saa_ref[rows, :] = o * ssg_ref[rows, :]


def _prompt_in_kernel(*refs, sb, n_t, n_cast):
    refs = list(refs)
    (x_ref, k_ref, v_ref, sq_ref, ssg_ref, sk_ref, sv_ref,
     g_ref, whp_ref, wq_ref, whc_ref, wcc_ref, wbc_ref, wgc_ref, wgp_ref, wga_ref,
     convw_ref, poolw_ref, pscale_ref) = refs[:19]
    cast_in = refs[19:19 + n_cast]
    ac_ref, ap_ref, aa_ref, saa_ref, cst_ref, pst_ref = refs[19 + n_cast:25 + n_cast]
    cast_out = refs[25 + n_cast:25 + 2 * n_cast]
    u_s, p_s, sa_s, sb_s = refs[25 + 2 * n_cast:]
    t = pl.program_id(1)

    @pl.when(t == 0)
    def _():
        u_s[0:CONV_PAD, :] = jnp.zeros((CONV_PAD, D_MODEL), F32)
        p_s[0:POOL_PAD, :] = jnp.zeros((POOL_PAD, D_MODEL), F32)
        sa_s[0:16, :] = jnp.zeros((16, POOL_GROUP), F32)
        sb_s[0:16, :] = jnp.zeros((16, POOL_GROUP), F32)

    for w_ref, wb_ref in zip(cast_in, cast_out):
        wb_ref[...] = w_ref[...].astype(BF16)

    sample_attn = _sample_attention_tiles(sq_ref, ssg_ref, sk_ref, sv_ref, saa_ref, sb, n_t)
    for _ in range(N_KV_TILES):
        next(sample_attn, None)
    x = x_ref[0]
    h = _rmsnorm(x, g_ref[...]).astype(BF16)

    def hdot(w_ref, c):
        r = _dot(h, w_ref[:, _cols(0, c)])
        next(sample_attn, None)
        return r

    pos1 = (t * TM + 1 + lax.broadcasted_iota(jnp.int32, (TM, 1), 0)).astype(F32)
    n = TM + 16
    for c, w in enumerate(POOL_WINDOWS):
        sl = _cols(0, c)

        hp = hdot(whp_ref, c)
        p_s[POOL_PAD:POOL_PAD + TM, sl] = hp
        cur = p_s[16:16 + n, sl] + p_s[15:15 + n, sl]
        shift, src, dst = 2, sa_s, sb_s
        while shift < w:
            src[16:16 + n, :] = cur
            cur = src[16:16 + n, :] + src[16 - shift:16 - shift + n, :]
            shift *= 2
            src, dst = dst, src
        inv_cnt = 1.0 / jnp.minimum(pos1, float(w))
        mixed = (cur[16:, :] * inv_cnt - hp).astype(BF16)

        q = hdot(wq_ref, c).astype(BF16)

        hc = hdot(whc_ref, c)
        cc = hdot(wcc_ref, c)
        u = cc * hc
        u_s[CONV_PAD:CONV_PAD + TM, sl] = u
        u1 = u_s[CONV_PAD - 1:CONV_PAD - 1 + TM, sl]
        u2 = u_s[CONV_PAD - 2:CONV_PAD - 2 + TM, sl]
        y = convw_ref[0:1, sl] * u2 + convw_ref[1:2, sl] * u1 + convw_ref[2:3, sl] * u

        p = _softmax(_dot_nt(q, k_ref[:, sl]) * HEAD_DIM ** -0.5).astype(BF16)

        bc = hdot(wbc_ref, c)
        gc = hdot(wgc_ref, c)
        ac_ref[0, :, sl] = (bc * y * _silu(gc)).astype(BF16)

        gp = hdot(wgp_ref, c)
        pooled = _dot(mixed, poolw_ref[c]) * pscale_ref[:, sl]
        ap_ref[0, :, sl] = (pooled * _silu(gp)).astype(BF16)

        ga = hdot(wga_ref, c)
        o = _dot(p, v_ref[:, sl])
        aa_ref[0, :, sl] = (o * _silu(ga)).astype(BF16)

    new_conv = u_s[CONV_PAD + TM - 2:CONV_PAD + TM, :]
    cst_ref[0] = new_conv
    u_s[CONV_PAD - 2:CONV_PAD, :] = new_conv
    pst_ref[0] = p_s[POOL_PAD + TM - POOL_STATE:POOL_PAD + TM, :]
    p_s[16:POOL_PAD, :] = p_s[TM + 16:TM + POOL_PAD, :]

    for _ in sample_attn:
        pass


def _prompt_in(x, kb, vb, sq, ssg, cache_k, cache_v, w_a, w_to_cast, sw, layer, n_t):
    b, t, _ = x.shape
    nt = t // TM
    n_cast = len(w_to_cast)
    cast_rows = D_MODEL // (b * nt)
    assert cast_rows * b * nt == D_MODEL
    n_b = sq.shape[0] // n_t
    sb = n_b // (b * nt)
    assert sb * b * nt == n_b
    tile = pl.BlockSpec((1, TM, D_MODEL), lambda i, j: (i, j, 0))
    smp = pl.BlockSpec((sb * n_t, D_MODEL), lambda i, j: (i * nt + j, 0))
    skv = pl.BlockSpec((1, sb, N_MEM, HEADS, HEAD_DIM), lambda i, j: (layer, i * nt + j, 0, 0, 0))
    act = jax.ShapeDtypeStruct((b, t, D_MODEL), BF16)
    return pl.pallas_call(
        functools.partial(_prompt_in_kernel, sb=sb, n_t=n_t, n_cast=n_cast),
        grid=(b, nt),
        in_specs=[
            tile,
            pl.BlockSpec((None, N_MEM, D_MODEL), lambda i, j: (layer, i, 0)),
            pl.BlockSpec((None, N_MEM, D_MODEL), lambda i, j: (layer, i, 0)),
            smp, smp, skv, skv,
            _layer_spec(sw["g"], layer),
        ] + [_const_spec(w.shape) for w in w_a] + [
            _layer_spec(sw["conv_w"], layer), _layer_spec(sw["pool_w"], layer), _layer_spec(sw["pool_scale"], layer),
        ] + [pl.BlockSpec((None, cast_rows, D_MODEL), lambda i, j: (layer, i * nt + j, 0))] * n_cast,
        out_specs=[
            tile, tile, tile, smp,
            pl.BlockSpec((1, CONV_WIDTH - 1, D_MODEL), lambda i, j: (i, 0, 0)),
            pl.BlockSpec((1, POOL_STATE, D_MODEL), lambda i, j: (i, 0, 0)),
        ] + [pl.BlockSpec((cast_rows, D_MODEL), lambda i, j: (i * nt + j, 0))] * n_cast,
        out_shape=[
            act, act, act,
            jax.ShapeDtypeStruct(sq.shape, F32),
            jax.ShapeDtypeStruct((b, CONV_WIDTH - 1, D_MODEL), F32),
            jax.ShapeDtypeStruct((b, POOL_STATE, D_MODEL), F32),
        ] + [jax.ShapeDtypeStruct((D_MODEL, D_MODEL), BF16)] * n_cast,
        scratch_shapes=[
            pltpu.VMEM((CONV_PAD + TM, D_MODEL), F32),
            pltpu.VMEM((POOL_PAD + TM, D_MODEL), F32),
            pltpu.VMEM((POOL_PAD + TM, POOL_GROUP), F32),
            pltpu.VMEM((POOL_PAD + TM, POOL_GROUP), F32),
        ],
        compiler_params=_params(2),
        name="prompt_in",
    )(x, kb, vb, sq, ssg, cache_k, cache_v, sw["g"], *w_a, sw["conv_w"], sw["pool_w"], sw["pool_scale"], *w_to_cast)


def _sample_in_kernel(*refs, n_t, n_b, cast_weights):
    refs = list(refs)
    n_w = N_PROJ + N_CAST_ONLY if cast_weights else N_PROJ
    x_ref, tb_ref, bt_ref, cst_ref, pst_ref, g_ref = refs[:6]
    w_refs = refs[6:6 + n_w]
    convw_ref, poolw_ref, pscale_ref = refs[6 + n_w:9 + n_w]
    outs = refs[9 + n_w:]
    ac_ref, ap_ref, q_ref, sg_ref, ncst_ref, npst_ref = outs[:6]
    wb_refs = outs[6:-2]
    hb_s, ht_s = outs[-2:]
    c = pl.program_id(0)

    @pl.when(c == 0)
    def _():
        hb = _rmsnorm(x_ref[...], g_ref[...]).astype(BF16)
        hb_s[...] = hb
        ht_s[...] = _dot(tb_ref[...], hb).astype(BF16)

    hb, ht = hb_s[...], ht_s[...]
    w = [w_ref[...].astype(BF16) for w_ref in w_refs]
    for w_cast, wb_ref in zip(w, wb_refs):
        wb_ref[...] = w_cast
    whp, wq, whc, wcc, wbc, wgc, wgp, wga = w[:N_PROJ]

    def to_batch_major(a):
        return _dot(bt_ref[...], a.astype(BF16)).astype(BF16)

    def planes(a):
        return [a[i * n_b:(i + 1) * n_b] for i in range(n_t)]

    hp = _dot(ht, whp)
    u = _dot(ht, wcc) * _dot(ht, whc)
    bc = _dot(ht, wbc)
    silu_gc = _silu(_dot(ht, wgc))
    silu_gp = _silu(_dot(ht, wgp))
    q_ref[...] = _dot(hb, wq)
    sg_ref[...] = _silu(_dot(hb, wga))

    ext = [pst_ref[j] for j in range(POOL_STATE)] + planes(hp)
    for j, plane in enumerate(ext[-POOL_STATE:]):
        npst_ref[j] = plane
    wsum = {}

    def window_sum(k, i):
        if k == 0:
            return ext[i]
        if (k, i) not in wsum:
            wsum[(k, i)] = window_sum(k - 1, i) + window_sum(k - 1, i - 2 ** (k - 1))
        return wsum[(k, i)]

    mixed = []
    for i in range(n_t):
        mean = None
        for g, win in enumerate(POOL_WINDOWS):
            cnt = float(min(PAST_LEN + i + 1, win))
            cand = window_sum(g + 1, POOL_STATE + i) * (1.0 / cnt)
            mean = cand if mean is None else jnp.where(c == g, cand, mean)
        mixed.append(mean - ext[POOL_STATE + i])
    mixed = jnp.concatenate(mixed, axis=0).astype(BF16)
    pooled = _dot(mixed, poolw_ref[...].astype(BF16)) * pscale_ref[...]
    ap_ref[...] = to_batch_major(pooled * silu_gp)

    ext = [cst_ref[0], cst_ref[1]] + planes(u)
    y = jnp.concatenate(
        [convw_ref[0:1, :] * ext[i] + convw_ref[1:2, :] * ext[i + 1] + convw_ref[2:3, :] * ext[i + 2]
         for i in range(n_t)], axis=0)
    ac_ref[...] = to_batch_major(bc * y * silu_gc)
    for j, plane in enumerate(ext[-(CONV_WIDTH - 1):]):
        ncst_ref[j] = plane


def _row_permutations(n_t, n_b):
    r = jnp.arange(n_t * n_b)
    tb = jax.nn.one_hot((r % n_b) * n_t + r // n_b, n_t * n_b, dtype=BF16)
    return tb, tb.T


W_IN_BLOCK_OFFSETS = (OFF_HP, OFF_Q, OFF_HC, OFF_CC, OFF_BC, OFF_GC, OFF_GP, OFF_GA) + tuple(
    OFF_MERGE + k * D_MODEL for k in range(N_MERGE))
assert len(W_IN_BLOCK_OFFSETS) == N_PROJ + N_CAST_ONLY


def _sample_in(xs2d, cst_t, pst_t, g, w_in, w_bf16, conv_w, pool_w, pool_scale, layer, n_t, n_b):
    rows = n_t * n_b
    tb, bt = _row_permutations(n_t, n_b)
    cast_weights = w_bf16 is None
    if cast_weights:
        weights = [w_in] * len(W_IN_BLOCK_OFFSETS)
        w_specs = [pl.BlockSpec((None, D_MODEL, CW), functools.partial(lambda c, blk: (layer, 0, blk + c), blk=off // CW))
                   for off in W_IN_BLOCK_OFFSETS]
    else:
        weights = list(w_bf16[:N_PROJ])
        w_specs = [pl.BlockSpec((D_MODEL, CW), lambda c: (0, c))] * N_PROJ
    n_emit = len(weights) if cast_weights else 0
    chunk2d = pl.BlockSpec((rows, CW), lambda c: (0, c))

    def hist(n_rows):
        return pl.BlockSpec((None, n_rows, n_b, CW), lambda c: (layer, 0, 0, c))

    def new_hist(n_rows):
        return pl.BlockSpec((n_rows, n_b, CW), lambda c: (0, 0, c))

    outs = pl.pallas_call(
        functools.partial(_sample_in_kernel, n_t=n_t, n_b=n_b, cast_weights=cast_weights),
        grid=(N_CHUNKS,),
        in_specs=[
            _const_spec(xs2d.shape), _const_spec(tb.shape), _const_spec(bt.shape),
            hist(CONV_WIDTH - 1), hist(POOL_STATE), _layer_spec(g, layer),
        ] + w_specs + [
            pl.BlockSpec((None, CONV_WIDTH, CW), lambda c: (layer, 0, c)),
            pl.BlockSpec((None, None, POOL_GROUP, POOL_GROUP), lambda c: (layer, c, 0, 0)),
            pl.BlockSpec((None, 1, CW), lambda c: (layer, 0, c)),
        ],
        out_specs=([chunk2d] * 4 + [new_hist(CONV_WIDTH - 1), new_hist(POOL_STATE)]
                   + [pl.BlockSpec((D_MODEL, CW), lambda c: (0, c))] * n_emit),
        out_shape=[
            jax.ShapeDtypeStruct((rows, D_MODEL), BF16),
            jax.ShapeDtypeStruct((rows, D_MODEL), BF16),
            jax.ShapeDtypeStruct((rows, D_MODEL), F32),
            jax.ShapeDtypeStruct((rows, D_MODEL), F32),
            jax.ShapeDtypeStruct((CONV_WIDTH - 1, n_b, D_MODEL), F32),
            jax.ShapeDtypeStruct((POOL_STATE, n_b, D_MODEL), F32),
        ] + [jax.ShapeDtypeStruct((D_MODEL, D_MODEL), BF16)] * n_emit,
        scratch_shapes=[pltpu.VMEM((rows, D_MODEL), BF16), pltpu.VMEM((rows, D_MODEL), BF16)],
        compiler_params=_params(1),
        name="sample_in",
    )(xs2d, tb, bt, cst_t, pst_t, g, *weights, conv_w, pool_w, pool_scale)
    return tuple(outs[:6]) + (list(outs[6:]) if cast_weights else list(w_bf16),)


def _out_side_kernel(*refs, final_norm, n_prompt_tiles, n_cast):
    refs = list(refs)
    (x_ref, ac_ref, ap_ref, aa_ref, xs_ref, acs_ref, aps_ref, aas_ref,
     g_ref, wmc_ref, wmp_ref, wma_ref, wbc_ref, wbp_ref, wba_ref, wout_ref, fg_ref) = refs[:17]
    next_w_refs = refs[17:17 + n_cast]
    y_ref, ys_ref = refs[17 + n_cast:19 + n_cast]
    next_wb_refs = refs[19 + n_cast:19 + 2 * n_cast]
    m_s = refs[-1]

    def tile(x_ref, ac_ref, ap_ref, aa_ref, y_ref):
        rows = x_ref.shape[0]
        x = x_ref[...]
        h = _rmsnorm(x, g_ref[...]).astype(BF16)
        ac, ap, aa = ac_ref[...], ap_ref[...], aa_ref[...].astype(BF16)
        for c in range(N_CHUNKS):
            sl = _cols(0, c)
            conv_br = _dot(ac, wbc_ref[:, sl])
            pool_br = _dot(ap, wbp_ref[:, sl])
            att_br = _dot(aa, wba_ref[:, sl])
            mc = _dot(h, wmc_ref[:, sl])
            mp = _dot(h, wmp_ref[:, sl])
            ma = _dot(h, wma_ref[:, sl])
            merged = _sigmoid(mc) * conv_br + _sigmoid(mp) * pool_br + _sigmoid(ma) * att_br
            m_s[0:rows, sl] = merged.astype(BF16)
        xn = x + _dot(m_s[0:rows, :], wout_ref[...])
        if final_norm:
            xn = _rmsnorm(xn, fg_ref[...])
        y_ref[...] = xn

    r = pl.program_id(0)

    @pl.when(r < n_prompt_tiles)
    def _():
        for w_ref, wb_ref in zip(next_w_refs, next_wb_refs):
            wb_ref[...] = w_ref[...].astype(BF16)
        tile(x_ref, ac_ref, ap_ref, aa_ref, y_ref)

    @pl.when(r == n_prompt_tiles)
    def _():
        tile(xs_ref, acs_ref, aps_ref, aas_ref, ys_ref)


def _out_side(x2d, ac, ap, aa, xs2d, acs, aps, aas, w_merge, w_branch, sw, layer, final_norm, next_w_in=None):
    rows, rows_s = x2d.shape[0], xs2d.shape[0]
    n_tiles = rows // TM_OUT
    assert n_tiles * TM_OUT == rows and rows_s <= TM_OUT
    n_cast = 0 if next_w_in is None else len(W_IN_BLOCK_OFFSETS)
    cast_rows = D_MODEL // n_tiles
    assert cast_rows * n_tiles == D_MODEL
    cast_in = [pl.BlockSpec((None, cast_rows, D_MODEL),
                            functools.partial(lambda r, blk: (layer + 1, jnp.minimum(r, n_tiles - 1), blk),
                                              blk=off // D_MODEL))
               for off in W_IN_BLOCK_OFFSETS[:n_cast]]
    cast_out = [pl.BlockSpec((cast_rows, D_MODEL), lambda r: (jnp.minimum(r, n_tiles - 1), 0))] * n_cast
    tile = pl.BlockSpec((TM_OUT, D_MODEL), lambda r: (jnp.minimum(r, n_tiles - 1), 0))
    weights = (sw["g"], *w_merge, *w_branch, sw["fg"])
    w_specs = ([_layer_spec(sw["g"], layer)] + [_const_spec(w.shape) for w in (*w_merge, *w_branch)]
               + [_const_spec(sw["fg"].shape)])
    smp = _const_spec((rows_s, D_MODEL))
    y, ys, *next_w_bf16 = pl.pallas_call(
        functools.partial(_out_side_kernel, final_norm=final_norm, n_prompt_tiles=n_tiles, n_cast=n_cast),
        grid=(n_tiles + 1,),
        in_specs=[tile] * 4 + [smp] * 4 + w_specs + cast_in,
        out_specs=[tile, smp] + cast_out,
        out_shape=([jax.ShapeDtypeStruct((rows, D_MODEL), F32), jax.ShapeDtypeStruct((rows_s, D_MODEL), F32)]
                   + [jax.ShapeDtypeStruct((D_MODEL, D_MODEL), BF16)] * n_cast),
        scratch_shapes=[pltpu.VMEM((TM_OUT, D_MODEL), BF16)],
        compiler_params=_params(1),
        name="out_side",
    )(x2d, ac, ap, aa, xs2d, acs, aps, aas, *weights, *([next_w_in] * n_cast))
    return y, ys, (next_w_bf16 or None)


def kernel(x_prompt, x_sample, mem_prompt, cache_mem_k, cache_mem_v, state_conv, state_pool, norm_g, w_in, conv_w, pool_w, pool_scale, mem_norm_g, w_mem_kv, w_br_conv, w_br_pool, w_br_att, w_out, final_norm_g):
    depth = w_in.shape[0]
    b_p, t_p, _ = x_prompt.shape
    n_b, n_t, _ = x_sample.shape
    rows_p, rows_s = b_p * t_p, n_b * n_t

    k_f, v_f, k_b, v_b = _mem_kv(mem_prompt.reshape(b_p * N_MEM, D_MODEL), mem_norm_g, w_mem_kv)

    sw = dict(
        g=norm_g.reshape(depth, 1, D_MODEL), conv_w=conv_w, pool_w=pool_w.astype(BF16),
        pool_scale=pool_scale.reshape(depth, 1, D_MODEL), fg=final_norm_g.reshape(1, D_MODEL))
    w_branch_f32 = (w_br_conv, w_br_pool, w_br_att, w_out)

    cst_t = jnp.transpose(state_conv, (0, 2, 1, 3))
    pst_t = jnp.transpose(state_pool, (0, 2, 1, 3))

    xp, xs = x_prompt, x_sample.reshape(rows_s, D_MODEL)
    cv_p, pl_p, cv_s, pl_s = [], [], [], []
    flat = lambda a: a.reshape(rows_p, D_MODEL)
    w_bf16 = None
    for l in range(depth):
        ac_s, ap_s, q, sg, c_new_s, p_new_s, w_bf16 = _sample_in(
            xs, cst_t, pst_t, sw["g"], w_in, w_bf16, conv_w, pool_w, sw["pool_scale"], l, n_t, n_b)
        cv_s.append(c_new_s)
        pl_s.append(p_new_s)

        ac_p, ap_p, aa_p, aa_s, c_new, p_new, *w_branch = _prompt_in(
            xp, k_b, v_b, q, sg, cache_mem_k, cache_mem_v, w_bf16[:N_PROJ], w_branch_f32, sw, l, n_t)
        cv_p.append(c_new)
        pl_p.append(p_new)

        last = l == depth - 1
        xp, xs, w_bf16 = _out_side(flat(xp), flat(ac_p), flat(ap_p), flat(aa_p), xs, ac_s, ap_s, aa_s,
                                   w_bf16[N_PROJ:], w_branch, sw, l, last, None if last else w_in)
        xp = xp.reshape(b_p, t_p, D_MODEL)

    batch_first = lambda hist: jnp.transpose(jnp.stack(hist), (0, 2, 1, 3))
    return (xp, xs.reshape(n_b, n_t, D_MODEL), k_f, v_f, jnp.stack(cv_p), jnp.stack(pl_p),
            batch_first(cv_s), batch_first(pl_s))
```

```python
import functools

import jax
import jax.numpy as jnp
from jax import lax
from jax.experimental import pallas as pl
from jax.experimental.pallas import tpu as pltpu

D_MODEL = 1024
N_MEM = 256
HEADS = 4
HEAD_DIM = 256
CONV_WIDTH = 3
POOL_WINDOWS = (2, 4, 8, 16)
POOL_GROUP = 256
POOL_STATE = 15
PAST_LEN = 16384
EPS = 1e-6
OFF_HC, OFF_BC, OFF_CC, OFF_GC = 0, 1024, 2048, 3072
OFF_HP, OFF_GP = 4096, 5120
OFF_Q = 6144
OFF_GA = 7168
OFF_MERGE = 8192
N_MERGE = 3
N_PROJ = 8

CW = 256
N_CHUNKS = D_MODEL // CW
assert CW == POOL_GROUP == HEAD_DIM
assert POOL_WINDOWS == tuple(2 ** (g + 1) for g in range(len(POOL_WINDOWS)))
TM = 512
TM_OUT = 512
POOL_PAD = 32
CONV_PAD = 8
VMEM_BYTES_V7X = 64 * 1024 * 1024
VMEM_LIMIT = VMEM_BYTES_V7X - 2 * 1024 * 1024

F32 = jnp.float32
BF16 = jnp.bfloat16


def _dot(a, b):
    return jnp.dot(a, b, preferred_element_type=F32)


def _dot_nt(a, b):
    return lax.dot_general(a, b, (((1,), (1,)), ((), ())), preferred_element_type=F32)


def _sigmoid(x):
    return 0.5 * jnp.tanh(0.5 * x) + 0.5


def _silu(x):
    return x * _sigmoid(x)


def _rmsnorm(x, g):
    ms = jnp.mean(x * x, axis=-1, keepdims=True)
    return (x * lax.rsqrt(ms + EPS)) * g


def _softmax(s):
    e = jnp.exp(s - jnp.max(s, axis=-1, keepdims=True))
    return e * (1.0 / jnp.sum(e, axis=-1, keepdims=True))


def _cols(off, c, w=CW):
    return slice(off + c * w, off + (c + 1) * w)


def _params(n_axes):
    return pltpu.CompilerParams(
        dimension_semantics=("arbitrary",) * n_axes, vmem_limit_bytes=VMEM_LIMIT)


def _const_spec(shape):
    nd = len(shape)
    return pl.BlockSpec(shape, lambda *_: (0,) * nd, pipeline_mode=pl.Buffered(1))


def _layer_spec(arr, layer, cols=None, col_block=0):
    tail = list(arr.shape[1:])
    idx = [0] * len(tail)
    if cols is not None:
        tail[-1] = cols
        idx[-1] = col_block
    return pl.BlockSpec((None, *tail), lambda *_: (layer, *idx), pipeline_mode=pl.Buffered(1))


def _kv_kernel(mem_ref, g_ref, w_ref, k_ref, v_ref, kb_ref, vb_ref):
    h = _rmsnorm(mem_ref[...], g_ref[0]).astype(BF16)
    k = _dot(h, w_ref[0, :, :D_MODEL].astype(BF16))
    v = _dot(h, w_ref[0, :, D_MODEL:].astype(BF16))
    k_ref[0] = k.reshape(k_ref.shape[1:])
    v_ref[0] = v.reshape(v_ref.shape[1:])
    kb_ref[0] = k.astype(BF16)
    vb_ref[0] = v.astype(BF16)


def _mem_kv(mem2d, mem_norm_g, w_kv):
    depth = w_kv.shape[0]
    rows = mem2d.shape[0]
    bt = 2
    rt = bt * N_MEM
    out_f = jax.ShapeDtypeStruct((depth, rows // N_MEM, N_MEM, HEADS, HEAD_DIM), F32)
    out_b = jax.ShapeDtypeStruct((depth, rows, D_MODEL), BF16)
    blk = pl.BlockSpec((1, rt, D_MODEL), lambda l, r: (l, r, 0))
    blk5 = pl.BlockSpec((1, bt, N_MEM, HEADS, HEAD_DIM), lambda l, r: (l, r, 0, 0, 0))
    return pl.pallas_call(
        _kv_kernel,
        grid=(depth, rows // rt),
        in_specs=[
            pl.BlockSpec((rt, D_MODEL), lambda l, r: (r, 0)),
            pl.BlockSpec((1, 1, D_MODEL), lambda l, r: (l, 0, 0)),
            pl.BlockSpec((1, D_MODEL, 2 * D_MODEL), lambda l, r: (l, 0, 0)),
        ],
        out_specs=[blk5, blk5, blk, blk],
        out_shape=[out_f, out_f, out_b, out_b],
        compiler_params=_params(2),
        name="mem_kv",
    )(mem2d, mem_norm_g.reshape(depth, 1, D_MODEL), w_kv)


KV_TILE = CW // HEADS
N_KV_TILES = N_MEM // KV_TILE


def _sample_attention_tiles(sq_ref, ssg_ref, sk_ref, sv_ref, saa_ref, sb, n_t):
    n_flat = N_MEM * HEADS
    col_head = lax.broadcasted_iota(jnp.int32, (HEADS * n_t, n_flat), 1) % HEADS
    row_head = lax.broadcasted_iota(jnp.int32, (HEADS * n_t, n_flat), 0) // n_t
    own = col_head == row_head
    for i in range(sb):
        rows = slice(i * n_t, (i + 1) * n_t)
        qb = sq_ref[rows, :]
        qe = jnp.concatenate([qb[:, _cols(0, hh, HEAD_DIM)] for hh in range(HEADS)], axis=0).astype(BF16)
        s = []
        for j in range(N_KV_TILES):
            kf = sk_ref[0, i, j * KV_TILE:(j + 1) * KV_TILE].reshape(CW, HEAD_DIM).astype(BF16)
            s.append(_dot_nt(qe, kf))
            yield
        s = jnp.where(own, jnp.concatenate(s, axis=1) * HEAD_DIM ** -0.5, -jnp.inf)
        p = _softmax(s).astype(BF16)
        o = None
        for j in range(N_KV_TILES):
            vf = sv_ref[0, i, j * KV_TILE:(j + 1) * KV_TILE].reshape(CW, HEAD_DIM).astype(BF16)
            part = _dot(p[:, _cols(0, j)], vf)
            o = part if o is None else o + part
            yield
        o = jnp.concatenate([o[hh * n_t:(hh + 1) * n_t] for hh in range(HEADS)], axis=1)
        saa_ref[rows, :] = o * ssg_ref[rows, :]


def _prompt_in_kernel(*refs, sb, n_t, n_cast):
    refs = list(refs)
    (x_ref, k_ref, v_ref, sq_ref, ssg_ref, sk_ref, sv_ref,
     g_ref, whp_ref, wq_ref, whc_ref, wcc_ref, wbc_ref, wgc_ref, wgp_ref, wga_ref,
     convw_ref, poolw_ref, pscale_ref) = refs[:19]
    cast_in = refs[19:19 + n_cast]
    ac_ref, ap_ref, aa_ref, saa_ref, cst_ref, pst_ref = refs[19 + n_cast:25 + n_cast]
    cast_out = refs[25 + n_cast:25 + 2 * n_cast]
    u_s, p_s, sa_s, sb_s = refs[25 + 2 * n_cast:]
    t = pl.program_id(1)

    @pl.when(t == 0)
    def _():
        u_s[0:CONV_PAD, :] = jnp.zeros((CONV_PAD, D_MODEL), F32)
        p_s[0:POOL_PAD, :] = jnp.zeros((POOL_PAD, D_MODEL), F32)
        sa_s[0:16, :] = jnp.zeros((16, POOL_GROUP), F32)
        sb_s[0:16, :] = jnp.zeros((16, POOL_GROUP), F32)

    for w_ref, wb_ref in zip(cast_in, cast_out):
        wb_ref[...] = w_ref[...].astype(BF16)

    sample_attn = _sample_attention_tiles(sq_ref, ssg_ref, sk_ref, sv_ref, saa_ref, sb, n_t)
    for _ in range(N_KV_TILES):
        next(sample_attn, None)
    x = x_ref[0]
    h = _rmsnorm(x, g_ref[...]).astype(BF16)

    def hdot(w_ref, c):
        r = _dot(h, w_ref[:, _cols(0, c)])
        next(sample_attn, None)
        return r

    pos1 = (t * TM + 1 + lax.broadcasted_iota(jnp.int32, (TM, 1), 0)).astype(F32)
    n = TM + 16
    for c, w in enumerate(POOL_WINDOWS):
        sl = _cols(0, c)

        hp = hdot(whp_ref, c)
        p_s[POOL_PAD:POOL_PAD + TM, sl] = hp
        cur = p_s[16:16 + n, sl] + p_s[15:15 + n, sl]
        shift, src, dst = 2, sa_s, sb_s
        while shift < w:
            src[16:16 + n, :] = cur
            cur = src[16:16 + n, :] + src[16 - shift:16 - shift + n, :]
            shift *= 2
            src, dst = dst, src
        inv_cnt = 1.0 / jnp.minimum(pos1, float(w))
        mixed = (cur[16:, :] * inv_cnt - hp).astype(BF16)

        q = hdot(wq_ref, c).astype(BF16)

        hc = hdot(whc_ref, c)
        cc = hdot(wcc_ref, c)
        u = cc * hc
        u_s[CONV_PAD:CONV_PAD + TM, sl] = u
        u1 = u_s[CONV_PAD - 1:CONV_PAD - 1 + TM, sl]
        u2 = u_s[CONV_PAD - 2:CONV_PAD - 2 + TM, sl]
        y = convw_ref[0:1, sl] * u2 + convw_ref[1:2, sl] * u1 + convw_ref[2:3, sl] * u

        p = _softmax(_dot_nt(q, k_ref[:, sl]) * HEAD_DIM ** -0.5).astype(BF16)

        bc = hdot(wbc_ref, c)
        gc = hdot(wgc_ref, c)
        ac_ref[0, :, sl] = (bc * y * _silu(gc)).astype(BF16)

        gp = hdot(wgp_ref, c)
        pooled = _dot(mixed, poolw_ref[c]) * pscale_ref[:, sl]
        ap_ref[0, :, sl] = (pooled * _silu(gp)).astype(BF16)

        ga = hdot(wga_ref, c)
        o = _dot(p, v_ref[:, sl])
        aa_ref[0, :, sl] = (o * _silu(ga)).astype(BF16)

    new_conv = u_s[CONV_PAD + TM - 2:CONV_PAD + TM, :]
    cst_ref[0] = new_conv
    u_s[CONV_PAD - 2:CONV_PAD, :] = new_conv
    pst_ref[0] = p_s[POOL_PAD + TM - POOL_STATE:POOL_PAD + TM, :]
    p_s[16:POOL_PAD, :] = p_s[TM + 16:TM + POOL_PAD, :]

    for _ in sample_attn:
        pass


def _prompt_in(x, kb, vb, sq, ssg, cache_k, cache_v, w_a, w_to_cast, sw, layer, n_t):
    b, t, _ = x.shape
    nt = t // TM
    n_cast = len(w_to_cast)
    cast_rows = D_MODEL // (b * nt)
    assert cast_rows * b * nt == D_MODEL
    n_b = sq.shape[0] // n_t
    sb = n_b // (b * nt)
    assert sb * b * nt == n_b
    tile = pl.BlockSpec((1, TM, D_MODEL), lambda i, j: (i, j, 0))
    smp = pl.BlockSpec((sb * n_t, D_MODEL), lambda i, j: (i * nt + j, 0))
    skv = pl.BlockSpec((1, sb, N_MEM, HEADS, HEAD_DIM), lambda i, j: (layer, i * nt + j, 0, 0, 0))
    act = jax.ShapeDtypeStruct((b, t, D_MODEL), BF16)
    return pl.pallas_call(
        functools.partial(_prompt_in_kernel, sb=sb, n_t=n_t, n_cast=n_cast),
        grid=(b, nt),
        in_specs=[
            tile,
            pl.BlockSpec((None, N_MEM, D_MODEL), lambda i, j: (layer, i, 0)),
            pl.BlockSpec((None, N_MEM, D_MODEL), lambda i, j: (layer, i, 0)),
            smp, smp, skv, skv,
            _layer_spec(sw["g"], layer),
        ] + [_const_spec(w.shape) for w in w_a] + [
            _layer_spec(sw["conv_w"], layer), _layer_spec(sw["pool_w"], layer), _layer_spec(sw["pool_scale"], layer),
        ] + [pl.BlockSpec((None, cast_rows, D_MODEL),
                          functools.partial(lambda i, j, blk: (layer, i * nt + j, blk), blk=blk))
             for _, blk in w_to_cast],
        out_specs=[
            tile, tile, tile, smp,
            pl.BlockSpec((1, CONV_WIDTH - 1, D_MODEL), lambda i, j: (i, 0, 0)),
            pl.BlockSpec((1, POOL_STATE, D_MODEL), lambda i, j: (i, 0, 0)),
        ] + [pl.BlockSpec((cast_rows, D_MODEL), lambda i, j: (i * nt + j, 0))] * n_cast,
        out_shape=[
            act, act, act,
            jax.ShapeDtypeStruct(sq.shape, F32),
            jax.ShapeDtypeStruct((b, CONV_WIDTH - 1, D_MODEL), F32),
            jax.ShapeDtypeStruct((b, POOL_STATE, D_MODEL), F32),
        ] + [jax.ShapeDtypeStruct((D_MODEL, D_MODEL), BF16)] * n_cast,
        scratch_shapes=[
            pltpu.VMEM((CONV_PAD + TM, D_MODEL), F32),
            pltpu.VMEM((POOL_PAD + TM, D_MODEL), F32),
            pltpu.VMEM((POOL_PAD + TM, POOL_GROUP), F32),
            pltpu.VMEM((POOL_PAD + TM, POOL_GROUP), F32),
        ],
        compiler_params=_params(2),
        name="prompt_in",
    )(x, kb, vb, sq, ssg, cache_k, cache_v, sw["g"], *w_a, sw["conv_w"], sw["pool_w"], sw["pool_scale"],
      *(w for w, _ in w_to_cast))


def _sample_in_kernel(*refs, n_t, n_b, cast_weights):
    refs = list(refs)
    n_w = N_PROJ
    x_ref, tb_ref, bt_ref, cst_ref, pst_ref, g_ref = refs[:6]
    w_refs = refs[6:6 + n_w]
    convw_ref, poolw_ref, pscale_ref = refs[6 + n_w:9 + n_w]
    outs = refs[9 + n_w:]
    ac_ref, ap_ref, q_ref, sg_ref, ncst_ref, npst_ref = outs[:6]
    wb_refs = outs[6:-2]
    hb_s, ht_s = outs[-2:]
    c = pl.program_id(0)

    @pl.when(c == 0)
    def _():
        hb = _rmsnorm(x_ref[...], g_ref[...]).astype(BF16)
        hb_s[...] = hb
        ht_s[...] = _dot(tb_ref[...], hb).astype(BF16)

    hb, ht = hb_s[...], ht_s[...]
    w = [w_ref[...].astype(BF16) for w_ref in w_refs]
    for w_cast, wb_ref in zip(w, wb_refs):
        wb_ref[...] = w_cast
    whp, wq, whc, wcc, wbc, wgc, wgp, wga = w

    def to_batch_major(a):
        return _dot(bt_ref[...], a.astype(BF16)).astype(BF16)

    def planes(a):
        return [a[i * n_b:(i + 1) * n_b] for i in range(n_t)]

    hp = _dot(ht, whp)
    u = _dot(ht, wcc) * _dot(ht, whc)
    bc = _dot(ht, wbc)
    silu_gc = _silu(_dot(ht, wgc))
    silu_gp = _silu(_dot(ht, wgp))
    q_ref[...] = _dot(hb, wq)
    sg_ref[...] = _silu(_dot(hb, wga))

    ext = [pst_ref[j] for j in range(POOL_STATE)] + planes(hp)
    for j, plane in enumerate(ext[-POOL_STATE:]):
        npst_ref[j] = plane
    wsum = {}

    def window_sum(k, i):
        if k == 0:
            return ext[i]
        if (k, i) not in wsum:
            wsum[(k, i)] = window_sum(k - 1, i) + window_sum(k - 1, i - 2 ** (k - 1))
        return wsum[(k, i)]

    mixed = []
    for i in range(n_t):
        mean = None
        for g, win in enumerate(POOL_WINDOWS):
            cnt = float(min(PAST_LEN + i + 1, win))
            cand = window_sum(g + 1, POOL_STATE + i) * (1.0 / cnt)
            mean = cand if mean is None else jnp.where(c == g, cand, mean)
        mixed.append(mean - ext[POOL_STATE + i])
    mixed = jnp.concatenate(mixed, axis=0).astype(BF16)
    pooled = _dot(mixed, poolw_ref[...].astype(BF16)) * pscale_ref[...]
    ap_ref[...] = to_batch_major(pooled * silu_gp)

    ext = [cst_ref[0], cst_ref[1]] + planes(u)
    y = jnp.concatenate(
        [convw_ref[0:1, :] * ext[i] + convw_ref[1:2, :] * ext[i + 1] + convw_ref[2:3, :] * ext[i + 2]
         for i in range(n_t)], axis=0)
    ac_ref[...] = to_batch_major(bc * y * silu_gc)
    for j, plane in enumerate(ext[-(CONV_WIDTH - 1):]):
        ncst_ref[j] = plane


def _row_permutations(n_t, n_b):
    r = jnp.arange(n_t * n_b)
    tb = jax.nn.one_hot((r % n_b) * n_t + r // n_b, n_t * n_b, dtype=BF16)
    return tb, tb.T


W_IN_BLOCK_OFFSETS = (OFF_HP, OFF_Q, OFF_HC, OFF_CC, OFF_BC, OFF_GC, OFF_GP, OFF_GA)
assert len(W_IN_BLOCK_OFFSETS) == N_PROJ


def _sample_in(xs2d, cst_t, pst_t, g, w_in, w_bf16, conv_w, pool_w, pool_scale, layer, n_t, n_b):
    rows = n_t * n_b
    tb, bt = _row_permutations(n_t, n_b)
    cast_weights = w_bf16 is None
    if cast_weights:
        weights = [w_in] * len(W_IN_BLOCK_OFFSETS)
        w_specs = [pl.BlockSpec((None, D_MODEL, CW), functools.partial(lambda c, blk: (layer, 0, blk + c), blk=off // CW))
                   for off in W_IN_BLOCK_OFFSETS]
    else:
        weights = list(w_bf16)
        w_specs = [pl.BlockSpec((D_MODEL, CW), lambda c: (0, c))] * N_PROJ
    n_emit = len(weights) if cast_weights else 0
    chunk2d = pl.BlockSpec((rows, CW), lambda c: (0, c))

    def hist(n_rows):
        return pl.BlockSpec((None, n_rows, n_b, CW), lambda c: (layer, 0, 0, c))

    def new_hist(n_rows):
        return pl.BlockSpec((n_rows, n_b, CW), lambda c: (0, 0, c))

    outs = pl.pallas_call(
        functools.partial(_sample_in_kernel, n_t=n_t, n_b=n_b, cast_weights=cast_weights),
        grid=(N_CHUNKS,),
        in_specs=[
            _const_spec(xs2d.shape), _const_spec(tb.shape), _const_spec(bt.shape),
            hist(CONV_WIDTH - 1), hist(POOL_STATE), _layer_spec(g, layer),
        ] + w_specs + [
            pl.BlockSpec((None, CONV_WIDTH, CW), lambda c: (layer, 0, c)),
            pl.BlockSpec((None, None, POOL_GROUP, POOL_GROUP), lambda c: (layer, c, 0, 0)),
            pl.BlockSpec((None, 1, CW), lambda c: (layer, 0, c)),
        ],
        out_specs=([chunk2d] * 4 + [new_hist(CONV_WIDTH - 1), new_hist(POOL_STATE)]
                   + [pl.BlockSpec((D_MODEL, CW), lambda c: (0, c))] * n_emit),
        out_shape=[
            jax.ShapeDtypeStruct((rows, D_MODEL), BF16),
            jax.ShapeDtypeStruct((rows, D_MODEL), BF16),
            jax.ShapeDtypeStruct((rows, D_MODEL), F32),
            jax.ShapeDtypeStruct((rows, D_MODEL), F32),
            jax.ShapeDtypeStruct((CONV_WIDTH - 1, n_b, D_MODEL), F32),
            jax.ShapeDtypeStruct((POOL_STATE, n_b, D_MODEL), F32),
        ] + [jax.ShapeDtypeStruct((D_MODEL, D_MODEL), BF16)] * n_emit,
        scratch_shapes=[pltpu.VMEM((rows, D_MODEL), BF16), pltpu.VMEM((rows, D_MODEL), BF16)],
        compiler_params=_params(1),
        name="sample_in",
    )(xs2d, tb, bt, cst_t, pst_t, g, *weights, conv_w, pool_w, pool_scale)
    return tuple(outs[:6]) + (list(outs[6:]) if cast_weights else list(w_bf16),)


def _out_side_kernel(*refs, final_norm, n_prompt_tiles, n_cast):
    refs = list(refs)
    (x_ref, ac_ref, ap_ref, aa_ref, xs_ref, acs_ref, aps_ref, aas_ref,
     g_ref, wmc_ref, wmp_ref, wma_ref, wbc_ref, wbp_ref, wba_ref, wout_ref, fg_ref) = refs[:17]
    next_w_refs = refs[17:17 + n_cast]
    y_ref, ys_ref = refs[17 + n_cast:19 + n_cast]
    next_wb_refs = refs[19 + n_cast:19 + 2 * n_cast]
    m_s = refs[-1]

    def tile(x_ref, ac_ref, ap_ref, aa_ref, y_ref):
        rows = x_ref.shape[0]
        x = x_ref[...]
        h = _rmsnorm(x, g_ref[...]).astype(BF16)
        ac, ap, aa = ac_ref[...], ap_ref[...], aa_ref[...].astype(BF16)
        for c in range(N_CHUNKS):
            sl = _cols(0, c)
            conv_br = _dot(ac, wbc_ref[:, sl])
            pool_br = _dot(ap, wbp_ref[:, sl])
            att_br = _dot(aa, wba_ref[:, sl])
            mc = _dot(h, wmc_ref[:, sl])
            mp = _dot(h, wmp_ref[:, sl])
            ma = _dot(h, wma_ref[:, sl])
            merged = _sigmoid(mc) * conv_br + _sigmoid(mp) * pool_br + _sigmoid(ma) * att_br
            m_s[0:rows, sl] = merged.astype(BF16)
        xn = x + _dot(m_s[0:rows, :], wout_ref[...])
        if final_norm:
            xn = _rmsnorm(xn, fg_ref[...])
        y_ref[...] = xn

    r = pl.program_id(0)

    @pl.when(r < n_prompt_tiles)
    def _():
        for w_ref, wb_ref in zip(next_w_refs, next_wb_refs):
            wb_ref[...] = w_ref[...].astype(BF16)
        tile(x_ref, ac_ref, ap_ref, aa_ref, y_ref)

    @pl.when(r == n_prompt_tiles)
    def _():
        tile(xs_ref, acs_ref, aps_ref, aas_ref, ys_ref)


def _out_side(x2d, ac, ap, aa, xs2d, acs, aps, aas, w_out_side, sw, layer, final_norm, next_w_in=None):
    rows, rows_s = x2d.shape[0], xs2d.shape[0]
    n_tiles = rows // TM_OUT
    assert n_tiles * TM_OUT == rows and rows_s <= TM_OUT
    n_cast = 0 if next_w_in is None else len(W_IN_BLOCK_OFFSETS)
    cast_rows = D_MODEL // n_tiles
    assert cast_rows * n_tiles == D_MODEL
    cast_in = [pl.BlockSpec((None, cast_rows, D_MODEL),
                            functools.partial(lambda r, blk: (layer + 1, jnp.minimum(r, n_tiles - 1), blk),
                                              blk=off // D_MODEL))
               for off in W_IN_BLOCK_OFFSETS[:n_cast]]
    cast_out = [pl.BlockSpec((cast_rows, D_MODEL), lambda r: (jnp.minimum(r, n_tiles - 1), 0))] * n_cast
    tile = pl.BlockSpec((TM_OUT, D_MODEL), lambda r: (jnp.minimum(r, n_tiles - 1), 0))
    weights = (sw["g"], *w_out_side, sw["fg"])
    w_specs = ([_layer_spec(sw["g"], layer)] + [_const_spec(w.shape) for w in w_out_side]
               + [_const_spec(sw["fg"].shape)])
    smp = _const_spec((rows_s, D_MODEL))
    y, ys, *next_w_bf16 = pl.pallas_call(
        functools.partial(_out_side_kernel, final_norm=final_norm, n_prompt_tiles=n_tiles, n_cast=n_cast),
        grid=(n_tiles + 1,),
        in_specs=[tile] * 4 + [smp] * 4 + w_specs + cast_in,
        out_specs=[tile, smp] + cast_out,
        out_shape=([jax.ShapeDtypeStruct((rows, D_MODEL), F32), jax.ShapeDtypeStruct((rows_s, D_MODEL), F32)]
                   + [jax.ShapeDtypeStruct((D_MODEL, D_MODEL), BF16)] * n_cast),
        scratch_shapes=[pltpu.VMEM((TM_OUT, D_MODEL), BF16)],
        compiler_params=_params(1),
        name="out_side",
    )(x2d, ac, ap, aa, xs2d, acs, aps, aas, *weights, *([next_w_in] * n_cast))
    return y, ys, (next_w_bf16 or None)


def kernel(x_prompt, x_sample, mem_prompt, cache_mem_k, cache_mem_v, state_conv, state_pool, norm_g, w_in, conv_w, pool_w, pool_scale, mem_norm_g, w_mem_kv, w_br_conv, w_br_pool, w_br_att, w_out, final_norm_g):
    depth = w_in.shape[0]
    b_p, t_p, _ = x_prompt.shape
    n_b, n_t, _ = x_sample.shape
    rows_p, rows_s = b_p * t_p, n_b * n_t

    k_f, v_f, k_b, v_b = _mem_kv(mem_prompt.reshape(b_p * N_MEM, D_MODEL), mem_norm_g, w_mem_kv)

    sw = dict(
        g=norm_g.reshape(depth, 1, D_MODEL), conv_w=conv_w, pool_w=pool_w.astype(BF16),
        pool_scale=pool_scale.reshape(depth, 1, D_MODEL), fg=final_norm_g.reshape(1, D_MODEL))
    w_out_side_f32 = tuple((w_in, OFF_MERGE // D_MODEL + k) for k in range(N_MERGE)) + tuple(
        (w, 0) for w in (w_br_conv, w_br_pool, w_br_att, w_out))

    cst_t = jnp.transpose(state_conv, (0, 2, 1, 3))
    pst_t = jnp.transpose(state_pool, (0, 2, 1, 3))

    xp, xs = x_prompt, x_sample.reshape(rows_s, D_MODEL)
    cv_p, pl_p, cv_s, pl_s = [], [], [], []
    flat = lambda a: a.reshape(rows_p, D_MODEL)
    w_bf16 = None
    for l in range(depth):
        ac_s, ap_s, q, sg, c_new_s, p_new_s, w_bf16 = _sample_in(
            xs, cst_t, pst_t, sw["g"], w_in, w_bf16, conv_w, pool_w, sw["pool_scale"], l, n_t, n_b)
        cv_s.append(c_new_s)
        pl_s.append(p_new_s)

        ac_p, ap_p, aa_p, aa_s, c_new, p_new, *w_out_side = _prompt_in(
            xp, k_b, v_b, q, sg, cache_mem_k, cache_mem_v, w_bf16, w_out_side_f32, sw, l, n_t)
        cv_p.append(c_new)
        pl_p.append(p_new)

        last = l == depth - 1
        xp, xs, w_bf16 = _out_side(flat(xp), flat(ac_p), flat(ap_p), flat(aa_p), xs, ac_s, ap_s, aa_s,
                                   w_out_side, sw, l, last, None if last else w_in)
        xp = xp.reshape(b_p, t_p, D_MODEL)

    batch_first = lambda hist: jnp.transpose(jnp.stack(hist), (0, 2, 1, 3))
    return (xp, xs.reshape(n_b, n_t, D_MODEL), k_f, v_f, jnp.stack(cv_p), jnp.stack(pl_p),
            batch_first(cv_s), batch_first(pl_s))
```

```python
import functools

import jax
import jax.numpy as jnp
from jax import lax
from jax.experimental import pallas as pl
from jax.experimental.pallas import tpu as pltpu

D_MODEL = 1024
N_MEM = 256
HEADS = 4
HEAD_DIM = 256
CONV_WIDTH = 3
POOL_WINDOWS = (2, 4, 8, 16)
POOL_GROUP = 256
POOL_STATE = 15
PAST_LEN = 16384
EPS = 1e-6
OFF_HC, OFF_BC, OFF_CC, OFF_GC = 0, 1024, 2048, 3072
OFF_HP, OFF_GP = 4096, 5120
OFF_Q = 6144
OFF_GA = 7168
OFF_MERGE = 8192
N_MERGE = 3
N_PROJ = 8
N_CAST_ONLY = N_MERGE

CW = 256
N_CHUNKS = D_MODEL // CW
assert CW == POOL_GROUP == HEAD_DIM
assert POOL_WINDOWS == tuple(2 ** (g + 1) for g in range(len(POOL_WINDOWS)))
TM = 512
TM_OUT = 512
POOL_PAD = 32
CONV_PAD = 8
VMEM_BYTES_V7X = 64 * 1024 * 1024
VMEM_LIMIT = VMEM_BYTES_V7X - 2 * 1024 * 1024

F32 = jnp.float32
BF16 = jnp.bfloat16


def _dot(a, b):
    return jnp.dot(a, b, preferred_element_type=F32)


def _dot_nt(a, b):
    return lax.dot_general(a, b, (((1,), (1,)), ((), ())), preferred_element_type=F32)


def _sigmoid(x):
    return 0.5 * jnp.tanh(0.5 * x) + 0.5


def _silu(x):
    return x * _sigmoid(x)


def _rmsnorm(x, g):
    ms = jnp.mean(x * x, axis=-1, keepdims=True)
    return (x * lax.rsqrt(ms + EPS)) * g


def _softmax(s):
    e = jnp.exp(s - jnp.max(s, axis=-1, keepdims=True))
    return e * (1.0 / jnp.sum(e, axis=-1, keepdims=True))


def _cols(off, c, w=CW):
    return slice(off + c * w, off + (c + 1) * w)


def _params(n_axes):
    return pltpu.CompilerParams(
        dimension_semantics=("arbitrary",) * n_axes, vmem_limit_bytes=VMEM_LIMIT)


def _const_spec(shape):
    nd = len(shape)
    return pl.BlockSpec(shape, lambda *_: (0,) * nd, pipeline_mode=pl.Buffered(1))


def _layer_spec(arr, layer, cols=None, col_block=0):
    tail = list(arr.shape[1:])
    idx = [0] * len(tail)
    if cols is not None:
        tail[-1] = cols
        idx[-1] = col_block
    return pl.BlockSpec((None, *tail), lambda *_: (layer, *idx), pipeline_mode=pl.Buffered(1))


def _kv_kernel(mem_ref, g_ref, w_ref, k_ref, v_ref, kb_ref, vb_ref):
    h = _rmsnorm(mem_ref[...], g_ref[0]).astype(BF16)
    k = _dot(h, w_ref[0, :, :D_MODEL].astype(BF16))
    v = _dot(h, w_ref[0, :, D_MODEL:].astype(BF16))
    k_ref[0] = k.reshape(k_ref.shape[1:])
    v_ref[0] = v.reshape(v_ref.shape[1:])
    kb_ref[0] = k.astype(BF16)
    vb_ref[0] = v.astype(BF16)


def _mem_kv(mem2d, mem_norm_g, w_kv):
    depth = w_kv.shape[0]
    rows = mem2d.shape[0]
    bt = 2
    rt = bt * N_MEM
    out_f = jax.ShapeDtypeStruct((depth, rows // N_MEM, N_MEM, HEADS, HEAD_DIM), F32)
    out_b = jax.ShapeDtypeStruct((depth, rows, D_MODEL), BF16)
    blk = pl.BlockSpec((1, rt, D_MODEL), lambda l, r: (l, r, 0))
    blk5 = pl.BlockSpec((1, bt, N_MEM, HEADS, HEAD_DIM), lambda l, r: (l, r, 0, 0, 0))
    return pl.pallas_call(
        _kv_kernel,
        grid=(depth, rows // rt),
        in_specs=[
            pl.BlockSpec((rt, D_MODEL), lambda l, r: (r, 0)),
            pl.BlockSpec((1, 1, D_MODEL), lambda l, r: (l, 0, 0)),
            pl.BlockSpec((1, D_MODEL, 2 * D_MODEL), lambda l, r: (l, 0, 0)),
        ],
        out_specs=[blk5, blk5, blk, blk],
        out_shape=[out_f, out_f, out_b, out_b],
        compiler_params=_params(2),
        name="mem_kv",
    )(mem2d, mem_norm_g.reshape(depth, 1, D_MODEL), w_kv)


KV_TILE = CW // HEADS
N_KV_TILES = N_MEM // KV_TILE


def _kv_copies(k_hbm, v_hbm, kbuf, vbuf, sem, layer, batch, slot):
    return (pltpu.make_async_copy(k_hbm.at[layer, batch], kbuf.at[slot], sem.at[0, slot]),
            pltpu.make_async_copy(v_hbm.at[layer, batch], vbuf.at[slot], sem.at[1, slot]))


def _sample_attention_tiles(sq_ref, ssg_ref, k_hbm, v_hbm, kbuf, vbuf, sem, saa_ref, sb, n_t, layer,
                            first_batch, next_first_batch):
    assert sb % 2 == 0
    n_flat = N_MEM * HEADS
    col_head = lax.broadcasted_iota(jnp.int32, (HEADS * n_t, n_flat), 1) % HEADS
    row_head = lax.broadcasted_iota(jnp.int32, (HEADS * n_t, n_flat), 0) // n_t
    own = col_head == row_head
    for i in range(sb):
        slot = i % 2
        for copy in _kv_copies(k_hbm, v_hbm, kbuf, vbuf, sem, layer, first_batch + i, slot):
            copy.wait()
        if i + 1 < sb:
            for copy in _kv_copies(k_hbm, v_hbm, kbuf, vbuf, sem, layer, first_batch + i + 1, 1 - slot):
                copy.start()
        else:
            for copy in _kv_copies(k_hbm, v_hbm, kbuf, vbuf, sem, layer, next_first_batch, 1 - slot):
                copy.start()
        rows = slice(i * n_t, (i + 1) * n_t)
        qb = sq_ref[rows, :]
        qe = jnp.concatenate([qb[:, _cols(0, hh, HEAD_DIM)] for hh in range(HEADS)], axis=0).astype(BF16)
        s = []
        for j in range(N_KV_TILES):
            kf = kbuf[slot, j * KV_TILE:(j + 1) * KV_TILE].reshape(CW, HEAD_DIM).astype(BF16)
            s.append(_dot_nt(qe, kf))
            yield
        s = jnp.where(own, jnp.concatenate(s, axis=1) * HEAD_DIM ** -0.5, -jnp.inf)
        p = _softmax(s).astype(BF16)
        o = None
        for j in range(N_KV_TILES):
            vf = vbuf[slot, j * KV_TILE:(j + 1) * KV_TILE].reshape(CW, HEAD_DIM).astype(BF16)
            part = _dot(p[:, _cols(0, j)], vf)
            o = part if o is None else o + part
            yield
        o = jnp.concatenate([o[hh * n_t:(hh + 1) * n_t] for hh in range(HEADS)], axis=1)
        saa_ref[rows, :] = o * ssg_ref[rows, :]


def _prompt_in_kernel(*refs, sb, n_t, n_cast, layer):
    refs = list(refs)
    (x_ref, k_ref, v_ref, sq_ref, ssg_ref, sk_ref, sv_ref,
     g_ref, whp_ref, wq_ref, whc_ref, wcc_ref, wbc_ref, wgc_ref, wgp_ref, wga_ref,
     convw_ref, poolw_ref, pscale_ref) = refs[:19]
    cast_in = refs[19:19 + n_cast]
    ac_ref, ap_ref, aa_ref, saa_ref, cst_ref, pst_ref = refs[19 + n_cast:25 + n_cast]
    cast_out = refs[25 + n_cast:25 + 2 * n_cast]
    u_s, p_s, sa_s, sb_s, kbuf, vbuf, kv_sem = refs[25 + 2 * n_cast:]
    t = pl.program_id(1)
    step = pl.program_id(0) * pl.num_programs(1) + t
    last_step = pl.num_programs(0) * pl.num_programs(1) - 1
    first_batch = step * sb
    next_first_batch = jnp.minimum(step + 1, last_step) * sb

    @pl.when(step == 0)
    def _():
        for copy in _kv_copies(sk_ref, sv_ref, kbuf, vbuf, kv_sem, layer, 0, 0):
            copy.start()

    @pl.when(t == 0)
    def _():
        u_s[0:CONV_PAD, :] = jnp.zeros((CONV_PAD, D_MODEL), F32)
        p_s[0:POOL_PAD, :] = jnp.zeros((POOL_PAD, D_MODEL), F32)
        sa_s[0:16, :] = jnp.zeros((16, POOL_GROUP), F32)
        sb_s[0:16, :] = jnp.zeros((16, POOL_GROUP), F32)

    for w_ref, wb_ref in zip(cast_in, cast_out):
        wb_ref[...] = w_ref[...].astype(BF16)

    sample_attn = _sample_attention_tiles(sq_ref, ssg_ref, sk_ref, sv_ref, kbuf, vbuf, kv_sem, saa_ref, sb, n_t,
                                          layer, first_batch, next_first_batch)
    for _ in range(N_KV_TILES):
        next(sample_attn, None)
    x = x_ref[0]
    h = _rmsnorm(x, g_ref[...]).astype(BF16)

    def hdot(w_ref, c):
        r = _dot(h, w_ref[:, _cols(0, c)])
        next(sample_attn, None)
        return r

    pos1 = (t * TM + 1 + lax.broadcasted_iota(jnp.int32, (TM, 1), 0)).astype(F32)
    n = TM + 16
    for c, w in enumerate(POOL_WINDOWS):
        sl = _cols(0, c)

        hp = hdot(whp_ref, c)
        p_s[POOL_PAD:POOL_PAD + TM, sl] = hp
        cur = p_s[16:16 + n, sl] + p_s[15:15 + n, sl]
        shift, src, dst = 2, sa_s, sb_s
        while shift < w:
            src[16:16 + n, :] = cur
            cur = src[16:16 + n, :] + src[16 - shift:16 - shift + n, :]
            shift *= 2
            src, dst = dst, src
        inv_cnt = 1.0 / jnp.minimum(pos1, float(w))
        mixed = (cur[16:, :] * inv_cnt - hp).astype(BF16)

        q = hdot(wq_ref, c).astype(BF16)

        hc = hdot(whc_ref, c)
        cc = hdot(wcc_ref, c)
        u = cc * hc
        u_s[CONV_PAD:CONV_PAD + TM, sl] = u
        u1 = u_s[CONV_PAD - 1:CONV_PAD - 1 + TM, sl]
        u2 = u_s[CONV_PAD - 2:CONV_PAD - 2 + TM, sl]
        y = convw_ref[0:1, sl] * u2 + convw_ref[1:2, sl] * u1 + convw_ref[2:3, sl] * u

        p = _softmax(_dot_nt(q, k_ref[:, sl]) * HEAD_DIM ** -0.5).astype(BF16)

        bc = hdot(wbc_ref, c)
        gc = hdot(wgc_ref, c)
        ac_ref[0, :, sl] = (bc * y * _silu(gc)).astype(BF16)

        gp = hdot(wgp_ref, c)
        pooled = _dot(mixed, poolw_ref[c]) * pscale_ref[:, sl]
        ap_ref[0, :, sl] = (pooled * _silu(gp)).astype(BF16)

        ga = hdot(wga_ref, c)
        o = _dot(p, v_ref[:, sl])
        aa_ref[0, :, sl] = (o * _silu(ga)).astype(BF16)

    new_conv = u_s[CONV_PAD + TM - 2:CONV_PAD + TM, :]
    cst_ref[0] = new_conv
    u_s[CONV_PAD - 2:CONV_PAD, :] = new_conv
    pst_ref[0] = p_s[POOL_PAD + TM - POOL_STATE:POOL_PAD + TM, :]
    p_s[16:POOL_PAD, :] = p_s[TM + 16:TM + POOL_PAD, :]

    for _ in sample_attn:
        pass

    @pl.when(step == last_step)
    def _():
        for copy in _kv_copies(sk_ref, sv_ref, kbuf, vbuf, kv_sem, layer, next_first_batch, 0):
            copy.wait()


def _prompt_in(x, kb, vb, sq, ssg, cache_k, cache_v, w_a, w_to_cast, sw, layer, n_t):
    b, t, _ = x.shape
    nt = t // TM
    n_cast = len(w_to_cast)
    cast_rows = D_MODEL // (b * nt)
    assert cast_rows * b * nt == D_MODEL
    n_b = sq.shape[0] // n_t
    sb = n_b // (b * nt)
    assert sb * b * nt == n_b
    tile = pl.BlockSpec((1, TM, D_MODEL), lambda i, j: (i, j, 0))
    smp = pl.BlockSpec((sb * n_t, D_MODEL), lambda i, j: (i * nt + j, 0))
    skv = pl.BlockSpec(memory_space=pl.ANY)
    act = jax.ShapeDtypeStruct((b, t, D_MODEL), BF16)
    return pl.pallas_call(
        functools.partial(_prompt_in_kernel, sb=sb, n_t=n_t, n_cast=n_cast, layer=layer),
        grid=(b, nt),
        in_specs=[
            tile,
            pl.BlockSpec((None, N_MEM, D_MODEL), lambda i, j: (layer, i, 0)),
            pl.BlockSpec((None, N_MEM, D_MODEL), lambda i, j: (layer, i, 0)),
            smp, smp, skv, skv,
            _layer_spec(sw["g"], layer),
        ] + [_const_spec(w.shape) for w in w_a] + [
            _layer_spec(sw["conv_w"], layer), _layer_spec(sw["pool_w"], layer), _layer_spec(sw["pool_scale"], layer),
        ] + [pl.BlockSpec((None, cast_rows, D_MODEL), lambda i, j: (layer, i * nt + j, 0))] * n_cast,
        out_specs=[
            tile, tile, tile, smp,
            pl.BlockSpec((1, CONV_WIDTH - 1, D_MODEL), lambda i, j: (i, 0, 0)),
            pl.BlockSpec((1, POOL_STATE, D_MODEL), lambda i, j: (i, 0, 0)),
        ] + [pl.BlockSpec((cast_rows, D_MODEL), lambda i, j: (i * nt + j, 0))] * n_cast,
        out_shape=[
            act, act, act,
            jax.ShapeDtypeStruct(sq.shape, F32),
            jax.ShapeDtypeStruct((b, CONV_WIDTH - 1, D_MODEL), F32),
            jax.ShapeDtypeStruct((b, POOL_STATE, D_MODEL), F32),
        ] + [jax.ShapeDtypeStruct((D_MODEL, D_MODEL), BF16)] * n_cast,
        scratch_shapes=[
            pltpu.VMEM((CONV_PAD + TM, D_MODEL), F32),
            pltpu.VMEM((POOL_PAD + TM, D_MODEL), F32),
            pltpu.VMEM((POOL_PAD + TM, POOL_GROUP), F32),
            pltpu.VMEM((POOL_PAD + TM, POOL_GROUP), F32),
            pltpu.VMEM((2, N_MEM, HEADS, HEAD_DIM), F32),
            pltpu.VMEM((2, N_MEM, HEADS, HEAD_DIM), F32),
            pltpu.SemaphoreType.DMA((2, 2)),
        ],
        compiler_params=_params(2),
        name="prompt_in",
    )(x, kb, vb, sq, ssg, cache_k, cache_v, sw["g"], *w_a, sw["conv_w"], sw["pool_w"], sw["pool_scale"], *w_to_cast)


def _sample_in_kernel(*refs, n_t, n_b, cast_weights):
    refs = list(refs)
    n_w = N_PROJ + N_CAST_ONLY if cast_weights else N_PROJ
    x_ref, tb_ref, bt_ref, cst_ref, pst_ref, g_ref = refs[:6]
    w_refs = refs[6:6 + n_w]
    convw_ref, poolw_ref, pscale_ref = refs[6 + n_w:9 + n_w]
    outs = refs[9 + n_w:]
    ac_ref, ap_ref, q_ref, sg_ref, ncst_ref, npst_ref = outs[:6]
    wb_refs = outs[6:-2]
    hb_s, ht_s = outs[-2:]
    c = pl.program_id(0)

    @pl.when(c == 0)
    def _():
        hb = _rmsnorm(x_ref[...], g_ref[...]).astype(BF16)
        hb_s[...] = hb
        ht_s[...] = _dot(tb_ref[...], hb).astype(BF16)

    hb, ht = hb_s[...], ht_s[...]
    w = [w_ref[...].astype(BF16) for w_ref in w_refs]
    for w_cast, wb_ref in zip(w, wb_refs):
        wb_ref[...] = w_cast
    whp, wq, whc, wcc, wbc, wgc, wgp, wga = w[:N_PROJ]

    def to_batch_major(a):
        return _dot(bt_ref[...], a.astype(BF16)).astype(BF16)

    def planes(a):
        return [a[i * n_b:(i + 1) * n_b] for i in range(n_t)]

    hp = _dot(ht, whp)
    u = _dot(ht, wcc) * _dot(ht, whc)
    bc = _dot(ht, wbc)
    silu_gc = _silu(_dot(ht, wgc))
    silu_gp = _silu(_dot(ht, wgp))
    q_ref[...] = _dot(hb, wq)
    sg_ref[...] = _silu(_dot(hb, wga))

    ext = [pst_ref[j] for j in range(POOL_STATE)] + planes(hp)
    for j, plane in enumerate(ext[-POOL_STATE:]):
        npst_ref[j] = plane
    wsum = {}

    def window_sum(k, i):
        if k == 0:
            return ext[i]
        if (k, i) not in wsum:
            wsum[(k, i)] = window_sum(k - 1, i) + window_sum(k - 1, i - 2 ** (k - 1))
        return wsum[(k, i)]

    mixed = []
    for i in range(n_t):
        mean = None
        for g, win in enumerate(POOL_WINDOWS):
            cnt = float(min(PAST_LEN + i + 1, win))
            cand = window_sum(g + 1, POOL_STATE + i) * (1.0 / cnt)
            mean = cand if mean is None else jnp.where(c == g, cand, mean)
        mixed.append(mean - ext[POOL_STATE + i])
    mixed = jnp.concatenate(mixed, axis=0).astype(BF16)
    pooled = _dot(mixed, poolw_ref[...].astype(BF16)) * pscale_ref[...]
    ap_ref[...] = to_batch_major(pooled * silu_gp)

    ext = [cst_ref[0], cst_ref[1]] + planes(u)
    y = jnp.concatenate(
        [convw_ref[0:1, :] * ext[i] + convw_ref[1:2, :] * ext[i + 1] + convw_ref[2:3, :] * ext[i + 2]
         for i in range(n_t)], axis=0)
    ac_ref[...] = to_batch_major(bc * y * silu_gc)
    for j, plane in enumerate(ext[-(CONV_WIDTH - 1):]):
        ncst_ref[j] = plane


def _row_permutations(n_t, n_b):
    r = jnp.arange(n_t * n_b)
    tb = jax.nn.one_hot((r % n_b) * n_t + r // n_b, n_t * n_b, dtype=BF16)
    return tb, tb.T


W_IN_BLOCK_OFFSETS = (OFF_HP, OFF_Q, OFF_HC, OFF_CC, OFF_BC, OFF_GC, OFF_GP, OFF_GA) + tuple(
    OFF_MERGE + k * D_MODEL for k in range(N_MERGE))
assert len(W_IN_BLOCK_OFFSETS) == N_PROJ + N_CAST_ONLY


def _sample_in(xs2d, cst_t, pst_t, g, w_in, w_bf16, conv_w, pool_w, pool_scale, layer, n_t, n_b):
    rows = n_t * n_b
    tb, bt = _row_permutations(n_t, n_b)
    cast_weights = w_bf16 is None
    if cast_weights:
        weights = [w_in] * len(W_IN_BLOCK_OFFSETS)
        w_specs = [pl.BlockSpec((None, D_MODEL, CW), functools.partial(lambda c, blk: (layer, 0, blk + c), blk=off // CW))
                   for off in W_IN_BLOCK_OFFSETS]
    else:
        weights = list(w_bf16[:N_PROJ])
        w_specs = [pl.BlockSpec((D_MODEL, CW), lambda c: (0, c))] * N_PROJ
    n_emit = len(weights) if cast_weights else 0
    chunk2d = pl.BlockSpec((rows, CW), lambda c: (0, c))

    def hist(n_rows):
        return pl.BlockSpec((None, n_rows, n_b, CW), lambda c: (layer, 0, 0, c))

    def new_hist(n_rows):
        return pl.BlockSpec((n_rows, n_b, CW), lambda c: (0, 0, c))

    outs = pl.pallas_call(
        functools.partial(_sample_in_kernel, n_t=n_t, n_b=n_b, cast_weights=cast_weights),
        grid=(N_CHUNKS,),
        in_specs=[
            _const_spec(xs2d.shape), _const_spec(tb.shape), _const_spec(bt.shape),
            hist(CONV_WIDTH - 1), hist(POOL_STATE), _layer_spec(g, layer),
        ] + w_specs + [
            pl.BlockSpec((None, CONV_WIDTH, CW), lambda c: (layer, 0, c)),
            pl.BlockSpec((None, None, POOL_GROUP, POOL_GROUP), lambda c: (layer, c, 0, 0)),
            pl.BlockSpec((None, 1, CW), lambda c: (layer, 0, c)),
        ],
        out_specs=([chunk2d] * 4 + [new_hist(CONV_WIDTH - 1), new_hist(POOL_STATE)]
                   + [pl.BlockSpec((D_MODEL, CW), lambda c: (0, c))] * n_emit),
        out_shape=[
            jax.ShapeDtypeStruct((rows, D_MODEL), BF16),
            jax.ShapeDtypeStruct((rows, D_MODEL), BF16),
            jax.ShapeDtypeStruct((rows, D_MODEL), F32),
            jax.ShapeDtypeStruct((rows, D_MODEL), F32),
            jax.ShapeDtypeStruct((CONV_WIDTH - 1, n_b, D_MODEL), F32),
            jax.ShapeDtypeStruct((POOL_STATE, n_b, D_MODEL), F32),
        ] + [jax.ShapeDtypeStruct((D_MODEL, D_MODEL), BF16)] * n_emit,
        scratch_shapes=[pltpu.VMEM((rows, D_MODEL), BF16), pltpu.VMEM((rows, D_MODEL), BF16)],
        compiler_params=_params(1),
        name="sample_in",
    )(xs2d, tb, bt, cst_t, pst_t, g, *weights, conv_w, pool_w, pool_scale)
    return tuple(outs[:6]) + (list(outs[6:]) if cast_weights else list(w_bf16),)


def _out_side_kernel(*refs, final_norm, n_prompt_tiles, n_cast):
    refs = list(refs)
    (x_ref, ac_ref, ap_ref, aa_ref, xs_ref, acs_ref, aps_ref, aas_ref,
     g_ref, wmc_ref, wmp_ref, wma_ref, wbc_ref, wbp_ref, wba_ref, wout_ref, fg_ref) = refs[:17]
    next_w_refs = refs[17:17 + n_cast]
    y_ref, ys_ref = refs[17 + n_cast:19 + n_cast]
    next_wb_refs = refs[19 + n_cast:19 + 2 * n_cast]
    m_s = refs[-1]

    def tile(x_ref, ac_ref, ap_ref, aa_ref, y_ref):
        rows = x_ref.shape[0]
        x = x_ref[...]
        h = _rmsnorm(x, g_ref[...]).astype(BF16)
        ac, ap, aa = ac_ref[...], ap_ref[...], aa_ref[...].astype(BF16)
        for c in range(N_CHUNKS):
            sl = _cols(0, c)
            conv_br = _dot(ac, wbc_ref[:, sl])
            pool_br = _dot(ap, wbp_ref[:, sl])
            att_br = _dot(aa, wba_ref[:, sl])
            mc = _dot(h, wmc_ref[:, sl])
            mp = _dot(h, wmp_ref[:, sl])
            ma = _dot(h, wma_ref[:, sl])
            merged = _sigmoid(mc) * conv_br + _sigmoid(mp) * pool_br + _sigmoid(ma) * att_br
            m_s[0:rows, sl] = merged.astype(BF16)
        xn = x + _dot(m_s[0:rows, :], wout_ref[...])
        if final_norm:
            xn = _rmsnorm(xn, fg_ref[...])
        y_ref[...] = xn

    r = pl.program_id(0)

    @pl.when(r < n_prompt_tiles)
    def _():
        for w_ref, wb_ref in zip(next_w_refs, next_wb_refs):
            wb_ref[...] = w_ref[...].astype(BF16)
        tile(x_ref, ac_ref, ap_ref, aa_ref, y_ref)

    @pl.when(r == n_prompt_tiles)
    def _():
        tile(xs_ref, acs_ref, aps_ref, aas_ref, ys_ref)


def _out_side(x2d, ac, ap, aa, xs2d, acs, aps, aas, w_merge, w_branch, sw, layer, final_norm, next_w_in=None):
    rows, rows_s = x2d.shape[0], xs2d.shape[0]
    n_tiles = rows // TM_OUT
    assert n_tiles * TM_OUT == rows and rows_s <= TM_OUT
    n_cast = 0 if next_w_in is None else len(W_IN_BLOCK_OFFSETS)
    cast_rows = D_MODEL // n_tiles
    assert cast_rows * n_tiles == D_MODEL
    cast_in = [pl.BlockSpec((None, cast_rows, D_MODEL),
                            functools.partial(lambda r, blk: (layer + 1, jnp.minimum(r, n_tiles - 1), blk),
                                              blk=off // D_MODEL))
               for off in W_IN_BLOCK_OFFSETS[:n_cast]]
    cast_out = [pl.BlockSpec((cast_rows, D_MODEL), lambda r: (jnp.minimum(r, n_tiles - 1), 0))] * n_cast
    tile = pl.BlockSpec((TM_OUT, D_MODEL), lambda r: (jnp.minimum(r, n_tiles - 1), 0))
    weights = (sw["g"], *w_merge, *w_branch, sw["fg"])
    w_specs = ([_layer_spec(sw["g"], layer)] + [_const_spec(w.shape) for w in (*w_merge, *w_branch)]
               + [_const_spec(sw["fg"].shape)])
    smp = _const_spec((rows_s, D_MODEL))
    y, ys, *next_w_bf16 = pl.pallas_call(
        functools.partial(_out_side_kernel, final_norm=final_norm, n_prompt_tiles=n_tiles, n_cast=n_cast),
        grid=(n_tiles + 1,),
        in_specs=[tile] * 4 + [smp] * 4 + w_specs + cast_in,
        out_specs=[tile, smp] + cast_out,
        out_shape=([jax.ShapeDtypeStruct((rows, D_MODEL), F32), jax.ShapeDtypeStruct((rows_s, D_MODEL), F32)]
                   + [jax.ShapeDtypeStruct((D_MODEL, D_MODEL), BF16)] * n_cast),
        scratch_shapes=[pltpu.VMEM((TM_OUT, D_MODEL), BF16)],
        compiler_params=_params(1),
        name="out_side",
    )(x2d, ac, ap, aa, xs2d, acs, aps, aas, *weights, *([next_w_in] * n_cast))
    return y, ys, (next_w_bf16 or None)


def kernel(x_prompt, x_sample, mem_prompt, cache_mem_k, cache_mem_v, state_conv, state_pool, norm_g, w_in, conv_w, pool_w, pool_scale, mem_norm_g, w_mem_kv, w_br_conv, w_br_pool, w_br_att, w_out, final_norm_g):
    depth = w_in.shape[0]
    b_p, t_p, _ = x_prompt.shape
    n_b, n_t, _ = x_sample.shape
    rows_p, rows_s = b_p * t_p, n_b * n_t

    k_f, v_f, k_b, v_b = _mem_kv(mem_prompt.reshape(b_p * N_MEM, D_MODEL), mem_norm_g, w_mem_kv)

    sw = dict(
        g=norm_g.reshape(depth, 1, D_MODEL), conv_w=conv_w, pool_w=pool_w.astype(BF16),
        pool_scale=pool_scale.reshape(depth, 1, D_MODEL), fg=final_norm_g.reshape(1, D_MODEL))
    w_branch_f32 = (w_br_conv, w_br_pool, w_br_att, w_out)

    cst_t = jnp.transpose(state_conv, (0, 2, 1, 3))
    pst_t = jnp.transpose(state_pool, (0, 2, 1, 3))

    xp, xs = x_prompt, x_sample.reshape(rows_s, D_MODEL)
    cv_p, pl_p, cv_s, pl_s = [], [], [], []
    flat = lambda a: a.reshape(rows_p, D_MODEL)
    w_bf16 = None
    for l in range(depth):
        ac_s, ap_s, q, sg, c_new_s, p_new_s, w_bf16 = _sample_in(
            xs, cst_t, pst_t, sw["g"], w_in, w_bf16, conv_w, pool_w, sw["pool_scale"], l, n_t, n_b)
        cv_s.append(c_new_s)
        pl_s.append(p_new_s)

        ac_p, ap_p, aa_p, aa_s, c_new, p_new, *w_branch = _prompt_in(
            xp, k_b, v_b, q, sg, cache_mem_k, cache_mem_v, w_bf16[:N_PROJ], w_branch_f32, sw, l, n_t)
        cv_p.append(c_new)
        pl_p.append(p_new)

        last = l == depth - 1
        xp, xs, w_bf16 = _out_side(flat(xp), flat(ac_p), flat(ap_p), flat(aa_p), xs, ac_s, ap_s, aa_s,
                                   w_bf16[N_PROJ:], w_branch, sw, l, last, None if last else w_in)
        xp = xp.reshape(b_p, t_p, D_MODEL)

    batch_first = lambda hist: jnp.transpose(jnp.stack(hist), (0, 2, 1, 3))
    return (xp, xs.reshape(n_b, n_t, D_MODEL), k_f, v_f, jnp.stack(cv_p), jnp.stack(pl_p),
            batch_first(cv_s), batch_first(pl_s))
```

```python
import functools

import jax
import jax.numpy as jnp
from jax import lax
from jax.experimental import pallas as pl
from jax.experimental.pallas import tpu as pltpu

D_MODEL = 1024
N_MEM = 256
HEADS = 4
HEAD_DIM = 256
CONV_WIDTH = 3
POOL_WINDOWS = (2, 4, 8, 16)
POOL_GROUP = 256
POOL_STATE = 15
PAST_LEN = 16384
EPS = 1e-6
OFF_HC, OFF_BC, OFF_CC, OFF_GC = 0, 1024, 2048, 3072
OFF_HP, OFF_GP = 4096, 5120
OFF_Q = 6144
OFF_GA = 7168
OFF_MERGE = 8192
N_MERGE = 3
N_PROJ = 8
N_CAST_ONLY = N_MERGE

CW = 256
N_CHUNKS = D_MODEL // CW
assert CW == POOL_GROUP == HEAD_DIM
assert POOL_WINDOWS == tuple(2 ** (g + 1) for g in range(len(POOL_WINDOWS)))
TM = 512
TM_OUT = 512
POOL_PAD = 32
CONV_PAD = 8
VMEM_BYTES_V7X = 64 * 1024 * 1024
VMEM_LIMIT = VMEM_BYTES_V7X - 2 * 1024 * 1024

F32 = jnp.float32
BF16 = jnp.bfloat16


def _dot(a, b):
    return jnp.dot(a, b, preferred_element_type=F32)


def _dot_nt(a, b):
    return lax.dot_general(a, b, (((1,), (1,)), ((), ())), preferred_element_type=F32)


def _sigmoid(x):
    return 0.5 * jnp.tanh(0.5 * x) + 0.5


def _silu(x):
    return x * _sigmoid(x)


def _rmsnorm(x, g):
    ms = jnp.mean(x * x, axis=-1, keepdims=True)
    return (x * lax.rsqrt(ms + EPS)) * g


def _softmax(s):
    e = jnp.exp(s - jnp.max(s, axis=-1, keepdims=True))
    return e * (1.0 / jnp.sum(e, axis=-1, keepdims=True))


def _cols(off, c, w=CW):
    return slice(off + c * w, off + (c + 1) * w)


def _params(n_axes):
    return pltpu.CompilerParams(
        dimension_semantics=("arbitrary",) * n_axes, vmem_limit_bytes=VMEM_LIMIT)


def _const_spec(shape):
    nd = len(shape)
    return pl.BlockSpec(shape, lambda *_: (0,) * nd, pipeline_mode=pl.Buffered(1))


def _layer_spec(arr, layer, cols=None, col_block=0):
    tail = list(arr.shape[1:])
    idx = [0] * len(tail)
    if cols is not None:
        tail[-1] = cols
        idx[-1] = col_block
    return pl.BlockSpec((None, *tail), lambda *_: (layer, *idx), pipeline_mode=pl.Buffered(1))


def _kv_kernel(mem_ref, g_ref, w_ref, *rest):
    n_cast = (len(rest) - 4) // 2
    cast_in, (k_ref, v_ref, kb_ref, vb_ref), cast_out = rest[:n_cast], rest[n_cast:n_cast + 4], rest[n_cast + 4:]
    for w_in_ref, wb_ref in zip(cast_in, cast_out):
        wb_ref[...] = w_in_ref[...].astype(BF16)
    h = _rmsnorm(mem_ref[...], g_ref[0]).astype(BF16)
    k = _dot(h, w_ref[0, :, :D_MODEL].astype(BF16))
    v = _dot(h, w_ref[0, :, D_MODEL:].astype(BF16))
    k_ref[0] = k.reshape(k_ref.shape[1:])
    v_ref[0] = v.reshape(v_ref.shape[1:])
    kb_ref[0] = k.astype(BF16)
    vb_ref[0] = v.astype(BF16)


def _mem_kv(mem2d, mem_norm_g, w_kv, w_in):
    depth = w_kv.shape[0]
    rows = mem2d.shape[0]
    bt = 2
    rt = bt * N_MEM
    n_r = rows // rt
    n_cast = len(W_IN_BLOCK_OFFSETS)
    cast_rows = D_MODEL // (depth * n_r)
    assert cast_rows * depth * n_r == D_MODEL
    cast_in = [pl.BlockSpec((None, cast_rows, D_MODEL),
                            functools.partial(lambda l, r, blk: (0, l * n_r + r, blk), blk=off // D_MODEL))
               for off in W_IN_BLOCK_OFFSETS]
    cast_out = [pl.BlockSpec((cast_rows, D_MODEL), lambda l, r: (l * n_r + r, 0))] * n_cast
    out_f = jax.ShapeDtypeStruct((depth, rows // N_MEM, N_MEM, HEADS, HEAD_DIM), F32)
    out_b = jax.ShapeDtypeStruct((depth, rows, D_MODEL), BF16)
    blk = pl.BlockSpec((1, rt, D_MODEL), lambda l, r: (l, r, 0))
    blk5 = pl.BlockSpec((1, bt, N_MEM, HEADS, HEAD_DIM), lambda l, r: (l, r, 0, 0, 0))
    k_f, v_f, k_b, v_b, *w_bf16 = pl.pallas_call(
        _kv_kernel,
        grid=(depth, n_r),
        in_specs=[
            pl.BlockSpec((rt, D_MODEL), lambda l, r: (r, 0)),
            pl.BlockSpec((1, 1, D_MODEL), lambda l, r: (l, 0, 0)),
            pl.BlockSpec((1, D_MODEL, 2 * D_MODEL), lambda l, r: (l, 0, 0)),
        ] + cast_in,
        out_specs=[blk5, blk5, blk, blk] + cast_out,
        out_shape=[out_f, out_f, out_b, out_b] + [jax.ShapeDtypeStruct((D_MODEL, D_MODEL), BF16)] * n_cast,
        compiler_params=_params(2),
        name="mem_kv",
    )(mem2d, mem_norm_g.reshape(depth, 1, D_MODEL), w_kv, *([w_in] * n_cast))
    return k_f, v_f, k_b, v_b, w_bf16


KV_TILE = CW // HEADS
N_KV_TILES = N_MEM // KV_TILE


def _sample_attention_tiles(sq_ref, ssg_ref, sk_ref, sv_ref, saa_ref, sb, n_t):
    n_flat = N_MEM * HEADS
    col_head = lax.broadcasted_iota(jnp.int32, (HEADS * n_t, n_flat), 1) % HEADS
    row_head = lax.broadcasted_iota(jnp.int32, (HEADS * n_t, n_flat), 0) // n_t
    own = col_head == row_head
    for i in range(sb):
        rows = slice(i * n_t, (i + 1) * n_t)
        qb = sq_ref[rows, :]
        qe = jnp.concatenate([qb[:, _cols(0, hh, HEAD_DIM)] for hh in range(HEADS)], axis=0).astype(BF16)
        s = []
        for j in range(N_KV_TILES):
            kf = sk_ref[0, i, j * KV_TILE:(j + 1) * KV_TILE].reshape(CW, HEAD_DIM).astype(BF16)
            s.append(_dot_nt(qe, kf))
            yield
        s = jnp.where(own, jnp.concatenate(s, axis=1) * HEAD_DIM ** -0.5, -jnp.inf)
        p = _softmax(s).astype(BF16)
        o = None
        for j in range(N_KV_TILES):
            vf = sv_ref[0, i, j * KV_TILE:(j + 1) * KV_TILE].reshape(CW, HEAD_DIM).astype(BF16)
            part = _dot(p[:, _cols(0, j)], vf)
            o = part if o is None else o + part
            yield
        o = jnp.concatenate([o[hh * n_t:(hh + 1) * n_t] for hh in range(HEADS)], axis=1)
        saa_ref[rows, :] = o * ssg_ref[rows, :]


def _prompt_in_kernel(*refs, sb, n_t, n_cast):
    refs = list(refs)
    (x_ref, k_ref, v_ref, sq_ref, ssg_ref, sk_ref, sv_ref,
     g_ref, whp_ref, wq_ref, whc_ref, wcc_ref, wbc_ref, wgc_ref, wgp_ref, wga_ref,
     convw_ref, poolw_ref, pscale_ref) = refs[:19]
    cast_in = refs[19:19 + n_cast]
    ac_ref, ap_ref, aa_ref, saa_ref, cst_ref, pst_ref = refs[19 + n_cast:25 + n_cast]
    cast_out = refs[25 + n_cast:25 + 2 * n_cast]
    u_s, p_s, sa_s, sb_s = refs[25 + 2 * n_cast:]
    t = pl.program_id(1)

    @pl.when(t == 0)
    def _():
        u_s[0:CONV_PAD, :] = jnp.zeros((CONV_PAD, D_MODEL), F32)
        p_s[0:POOL_PAD, :] = jnp.zeros((POOL_PAD, D_MODEL), F32)
        sa_s[0:16, :] = jnp.zeros((16, POOL_GROUP), F32)
        sb_s[0:16, :] = jnp.zeros((16, POOL_GROUP), F32)

    for w_ref, wb_ref in zip(cast_in, cast_out):
        wb_ref[...] = w_ref[...].astype(BF16)

    sample_attn = _sample_attention_tiles(sq_ref, ssg_ref, sk_ref, sv_ref, saa_ref, sb, n_t)
    for _ in range(N_KV_TILES):
        next(sample_attn, None)
    x = x_ref[0]
    h = _rmsnorm(x, g_ref[...]).astype(BF16)

    def hdot(w_ref, c):
        r = _dot(h, w_ref[:, _cols(0, c)])
        next(sample_attn, None)
        return r

    pos1 = (t * TM + 1 + lax.broadcasted_iota(jnp.int32, (TM, 1), 0)).astype(F32)
    n = TM + 16
    for c, w in enumerate(POOL_WINDOWS):
        sl = _cols(0, c)

        hp = hdot(whp_ref, c)
        p_s[POOL_PAD:POOL_PAD + TM, sl] = hp
        cur = p_s[16:16 + n, sl] + p_s[15:15 + n, sl]
        shift, src, dst = 2, sa_s, sb_s
        while shift < w:
            src[16:16 + n, :] = cur
            cur = src[16:16 + n, :] + src[16 - shift:16 - shift + n, :]
            shift *= 2
            src, dst = dst, src
        inv_cnt = 1.0 / jnp.minimum(pos1, float(w))
        mixed = (cur[16:, :] * inv_cnt - hp).astype(BF16)

        q = hdot(wq_ref, c).astype(BF16)

        hc = hdot(whc_ref, c)
        cc = hdot(wcc_ref, c)
        u = cc * hc
        u_s[CONV_PAD:CONV_PAD + TM, sl] = u
        u1 = u_s[CONV_PAD - 1:CONV_PAD - 1 + TM, sl]
        u2 = u_s[CONV_PAD - 2:CONV_PAD - 2 + TM, sl]
        y = convw_ref[0:1, sl] * u2 + convw_ref[1:2, sl] * u1 + convw_ref[2:3, sl] * u

        p = _softmax(_dot_nt(q, k_ref[:, sl]) * HEAD_DIM ** -0.5).astype(BF16)

        bc = hdot(wbc_ref, c)
        gc = hdot(wgc_ref, c)
        ac_ref[0, :, sl] = (bc * y * _silu(gc)).astype(BF16)

        gp = hdot(wgp_ref, c)
        pooled = _dot(mixed, poolw_ref[c]) * pscale_ref[:, sl]
        ap_ref[0, :, sl] = (pooled * _silu(gp)).astype(BF16)

        ga = hdot(wga_ref, c)
        o = _dot(p, v_ref[:, sl])
        aa_ref[0, :, sl] = (o * _silu(ga)).astype(BF16)

    new_conv = u_s[CONV_PAD + TM - 2:CONV_PAD + TM, :]
    cst_ref[0] = new_conv
    u_s[CONV_PAD - 2:CONV_PAD, :] = new_conv
    pst_ref[0] = p_s[POOL_PAD + TM - POOL_STATE:POOL_PAD + TM, :]
    p_s[16:POOL_PAD, :] = p_s[TM + 16:TM + POOL_PAD, :]

    for _ in sample_attn:
        pass


def _prompt_in(x, kb, vb, sq, ssg, cache_k, cache_v, w_a, w_to_cast, sw, layer, n_t):
    b, t, _ = x.shape
    nt = t // TM
    n_cast = len(w_to_cast)
    cast_rows = D_MODEL // (b * nt)
    assert cast_rows * b * nt == D_MODEL
    n_b = sq.shape[0] // n_t
    sb = n_b // (b * nt)
    assert sb * b * nt == n_b
    tile = pl.BlockSpec((1, TM, D_MODEL), lambda i, j: (i, j, 0))
    smp = pl.BlockSpec((sb * n_t, D_MODEL), lambda i, j: (i * nt + j, 0))
    skv = pl.BlockSpec((1, sb, N_MEM, HEADS, HEAD_DIM), lambda i, j: (layer, i * nt + j, 0, 0, 0))
    act = jax.ShapeDtypeStruct((b, t, D_MODEL), BF16)
    return pl.pallas_call(
        functools.partial(_prompt_in_kernel, sb=sb, n_t=n_t, n_cast=n_cast),
        grid=(b, nt),
        in_specs=[
            tile,
            pl.BlockSpec((None, N_MEM, D_MODEL), lambda i, j: (layer, i, 0)),
            pl.BlockSpec((None, N_MEM, D_MODEL), lambda i, j: (layer, i, 0)),
            smp, smp, skv, skv,
            _layer_spec(sw["g"], layer),
        ] + [_const_spec(w.shape) for w in w_a] + [
            _layer_spec(sw["conv_w"], layer), _layer_spec(sw["pool_w"], layer), _layer_spec(sw["pool_scale"], layer),
        ] + [pl.BlockSpec((None, cast_rows, D_MODEL), lambda i, j: (layer, i * nt + j, 0))] * n_cast,
        out_specs=[
            tile, tile, tile, smp,
            pl.BlockSpec((1, CONV_WIDTH - 1, D_MODEL), lambda i, j: (i, 0, 0)),
            pl.BlockSpec((1, POOL_STATE, D_MODEL), lambda i, j: (i, 0, 0)),
        ] + [pl.BlockSpec((cast_rows, D_MODEL), lambda i, j: (i * nt + j, 0))] * n_cast,
        out_shape=[
            act, act, act,
            jax.ShapeDtypeStruct(sq.shape, F32),
            jax.ShapeDtypeStruct((b, CONV_WIDTH - 1, D_MODEL), F32),
            jax.ShapeDtypeStruct((b, POOL_STATE, D_MODEL), F32),
        ] + [jax.ShapeDtypeStruct((D_MODEL, D_MODEL), BF16)] * n_cast,
        scratch_shapes=[
            pltpu.VMEM((CONV_PAD + TM, D_MODEL), F32),
            pltpu.VMEM((POOL_PAD + TM, D_MODEL), F32),
            pltpu.VMEM((POOL_PAD + TM, POOL_GROUP), F32),
            pltpu.VMEM((POOL_PAD + TM, POOL_GROUP), F32),
        ],
        compiler_params=_params(2),
        name="prompt_in",
    )(x, kb, vb, sq, ssg, cache_k, cache_v, sw["g"], *w_a, sw["conv_w"], sw["pool_w"], sw["pool_scale"], *w_to_cast)


def _sample_in_kernel(*refs, n_t, n_b, cast_weights):
    refs = list(refs)
    n_w = N_PROJ + N_CAST_ONLY if cast_weights else N_PROJ
    x_ref, tb_ref, bt_ref, cst_ref, pst_ref, g_ref = refs[:6]
    w_refs = refs[6:6 + n_w]
    convw_ref, poolw_ref, pscale_ref = refs[6 + n_w:9 + n_w]
    outs = refs[9 + n_w:]
    ac_ref, ap_ref, q_ref, sg_ref, ncst_ref, npst_ref = outs[:6]
    wb_refs = outs[6:-2]
    hb_s, ht_s = outs[-2:]
    c = pl.program_id(0)

    @pl.when(c == 0)
    def _():
        hb = _rmsnorm(x_ref[...], g_ref[...]).astype(BF16)
        hb_s[...] = hb
        ht_s[...] = _dot(tb_ref[...], hb).astype(BF16)

    hb, ht = hb_s[...], ht_s[...]
    w = [w_ref[...].astype(BF16) for w_ref in w_refs]
    for w_cast, wb_ref in zip(w, wb_refs):
        wb_ref[...] = w_cast
    whp, wq, whc, wcc, wbc, wgc, wgp, wga = w[:N_PROJ]

    def to_batch_major(a):
        return _dot(bt_ref[...], a.astype(BF16)).astype(BF16)

    def planes(a):
        return [a[i * n_b:(i + 1) * n_b] for i in range(n_t)]

    hp = _dot(ht, whp)
    u = _dot(ht, wcc) * _dot(ht, whc)
    bc = _dot(ht, wbc)
    silu_gc = _silu(_dot(ht, wgc))
    silu_gp = _silu(_dot(ht, wgp))
    q_ref[...] = _dot(hb, wq)
    sg_ref[...] = _silu(_dot(hb, wga))

    ext = [pst_ref[j] for j in range(POOL_STATE)] + planes(hp)
    for j, plane in enumerate(ext[-POOL_STATE:]):
        npst_ref[j] = plane
    wsum = {}

    def window_sum(k, i):
        if k == 0:
            return ext[i]
        if (k, i) not in wsum:
            wsum[(k, i)] = window_sum(k - 1, i) + window_sum(k - 1, i - 2 ** (k - 1))
        return wsum[(k, i)]

    mixed = []
    for i in range(n_t):
        mean = None
        for g, win in enumerate(POOL_WINDOWS):
            cnt = float(min(PAST_LEN + i + 1, win))
            cand = window_sum(g + 1, POOL_STATE + i) * (1.0 / cnt)
            mean = cand if mean is None else jnp.where(c == g, cand, mean)
        mixed.append(mean - ext[POOL_STATE + i])
    mixed = jnp.concatenate(mixed, axis=0).astype(BF16)
    pooled = _dot(mixed, poolw_ref[...].astype(BF16)) * pscale_ref[...]
    ap_ref[...] = to_batch_major(pooled * silu_gp)

    ext = [cst_ref[0], cst_ref[1]] + planes(u)
    y = jnp.concatenate(
        [convw_ref[0:1, :] * ext[i] + convw_ref[1:2, :] * ext[i + 1] + convw_ref[2:3, :] * ext[i + 2]
         for i in range(n_t)], axis=0)
    ac_ref[...] = to_batch_major(bc * y * silu_gc)
    for j, plane in enumerate(ext[-(CONV_WIDTH - 1):]):
        ncst_ref[j] = plane


def _row_permutations(n_t, n_b):
    r = jnp.arange(n_t * n_b)
    tb = jax.nn.one_hot((r % n_b) * n_t + r // n_b, n_t * n_b, dtype=BF16)
    return tb, tb.T


W_IN_BLOCK_OFFSETS = (OFF_HP, OFF_Q, OFF_HC, OFF_CC, OFF_BC, OFF_GC, OFF_GP, OFF_GA) + tuple(
    OFF_MERGE + k * D_MODEL for k in range(N_MERGE))
assert len(W_IN_BLOCK_OFFSETS) == N_PROJ + N_CAST_ONLY


def _sample_in(xs2d, cst_t, pst_t, g, w_in, w_bf16, conv_w, pool_w, pool_scale, layer, n_t, n_b):
    rows = n_t * n_b
    tb, bt = _row_permutations(n_t, n_b)
    cast_weights = w_bf16 is None
    if cast_weights:
        weights = [w_in] * len(W_IN_BLOCK_OFFSETS)
        w_specs = [pl.BlockSpec((None, D_MODEL, CW), functools.partial(lambda c, blk: (layer, 0, blk + c), blk=off // CW))
                   for off in W_IN_BLOCK_OFFSETS]
    else:
        weights = list(w_bf16[:N_PROJ])
        w_specs = [pl.BlockSpec((D_MODEL, CW), lambda c: (0, c))] * N_PROJ
    n_emit = len(weights) if cast_weights else 0
    chunk2d = pl.BlockSpec((rows, CW), lambda c: (0, c))

    def hist(n_rows):
        return pl.BlockSpec((None, n_rows, n_b, CW), lambda c: (layer, 0, 0, c))

    def new_hist(n_rows):
        return pl.BlockSpec((n_rows, n_b, CW), lambda c: (0, 0, c))

    outs = pl.pallas_call(
        functools.partial(_sample_in_kernel, n_t=n_t, n_b=n_b, cast_weights=cast_weights),
        grid=(N_CHUNKS,),
        in_specs=[
            _const_spec(xs2d.shape), _const_spec(tb.shape), _const_spec(bt.shape),
            hist(CONV_WIDTH - 1), hist(POOL_STATE), _layer_spec(g, layer),
        ] + w_specs + [
            pl.BlockSpec((None, CONV_WIDTH, CW), lambda c: (layer, 0, c)),
            pl.BlockSpec((None, None, POOL_GROUP, POOL_GROUP), lambda c: (layer, c, 0, 0)),
            pl.BlockSpec((None, 1, CW), lambda c: (layer, 0, c)),
        ],
        out_specs=([chunk2d] * 4 + [new_hist(CONV_WIDTH - 1), new_hist(POOL_STATE)]
                   + [pl.BlockSpec((D_MODEL, CW), lambda c: (0, c))] * n_emit),
        out_shape=[
            jax.ShapeDtypeStruct((rows, D_MODEL), BF16),
            jax.ShapeDtypeStruct((rows, D_MODEL), BF16),
            jax.ShapeDtypeStruct((rows, D_MODEL), F32),
            jax.ShapeDtypeStruct((rows, D_MODEL), F32),
            jax.ShapeDtypeStruct((CONV_WIDTH - 1, n_b, D_MODEL), F32),
            jax.ShapeDtypeStruct((POOL_STATE, n_b, D_MODEL), F32),
        ] + [jax.ShapeDtypeStruct((D_MODEL, D_MODEL), BF16)] * n_emit,
        scratch_shapes=[pltpu.VMEM((rows, D_MODEL), BF16), pltpu.VMEM((rows, D_MODEL), BF16)],
        compiler_params=_params(1),
        name="sample_in",
    )(xs2d, tb, bt, cst_t, pst_t, g, *weights, conv_w, pool_w, pool_scale)
    return tuple(outs[:6]) + (list(outs[6:]) if cast_weights else list(w_bf16),)


def _out_side_kernel(*refs, final_norm, n_prompt_tiles, n_cast):
    refs = list(refs)
    (x_ref, ac_ref, ap_ref, aa_ref, xs_ref, acs_ref, aps_ref, aas_ref,
     g_ref, wmc_ref, wmp_ref, wma_ref, wbc_ref, wbp_ref, wba_ref, wout_ref, fg_ref) = refs[:17]
    next_w_refs = refs[17:17 + n_cast]
    y_ref, ys_ref = refs[17 + n_cast:19 + n_cast]
    next_wb_refs = refs[19 + n_cast:19 + 2 * n_cast]
    m_s = refs[-1]

    def tile(x_ref, ac_ref, ap_ref, aa_ref, y_ref):
        rows = x_ref.shape[0]
        x = x_ref[...]
        h = _rmsnorm(x, g_ref[...]).astype(BF16)
        ac, ap, aa = ac_ref[...], ap_ref[...], aa_ref[...].astype(BF16)
        for c in range(N_CHUNKS):
            sl = _cols(0, c)
            conv_br = _dot(ac, wbc_ref[:, sl])
            pool_br = _dot(ap, wbp_ref[:, sl])
            att_br = _dot(aa, wba_ref[:, sl])
            mc = _dot(h, wmc_ref[:, sl])
            mp = _dot(h, wmp_ref[:, sl])
            ma = _dot(h, wma_ref[:, sl])
            merged = _sigmoid(mc) * conv_br + _sigmoid(mp) * pool_br + _sigmoid(ma) * att_br
            m_s[0:rows, sl] = merged.astype(BF16)
        xn = x + _dot(m_s[0:rows, :], wout_ref[...])
        if final_norm:
            xn = _rmsnorm(xn, fg_ref[...])
        y_ref[...] = xn

    r = pl.program_id(0)

    @pl.when(r < n_prompt_tiles)
    def _():
        for w_ref, wb_ref in zip(next_w_refs, next_wb_refs):
            wb_ref[...] = w_ref[...].astype(BF16)
        tile(x_ref, ac_ref, ap_ref, aa_ref, y_ref)

    @pl.when(r == n_prompt_tiles)
    def _():
        tile(xs_ref, acs_ref, aps_ref, aas_ref, ys_ref)


def _out_side(x2d, ac, ap, aa, xs2d, acs, aps, aas, w_merge, w_branch, sw, layer, final_norm, next_w_in=None):
    rows, rows_s = x2d.shape[0], xs2d.shape[0]
    n_tiles = rows // TM_OUT
    assert n_tiles * TM_OUT == rows and rows_s <= TM_OUT
    n_cast = 0 if next_w_in is None else len(W_IN_BLOCK_OFFSETS)
    cast_rows = D_MODEL // n_tiles
    assert cast_rows * n_tiles == D_MODEL
    cast_in = [pl.BlockSpec((None, cast_rows, D_MODEL),
                            functools.partial(lambda r, blk: (layer + 1, jnp.minimum(r, n_tiles - 1), blk),
                                              blk=off // D_MODEL))
               for off in W_IN_BLOCK_OFFSETS[:n_cast]]
    cast_out = [pl.BlockSpec((cast_rows, D_MODEL), lambda r: (jnp.minimum(r, n_tiles - 1), 0))] * n_cast
    tile = pl.BlockSpec((TM_OUT, D_MODEL), lambda r: (jnp.minimum(r, n_tiles - 1), 0))
    weights = (sw["g"], *w_merge, *w_branch, sw["fg"])
    w_specs = ([_layer_spec(sw["g"], layer)] + [_const_spec(w.shape) for w in (*w_merge, *w_branch)]
               + [_const_spec(sw["fg"].shape)])
    smp = _const_spec((rows_s, D_MODEL))
    y, ys, *next_w_bf16 = pl.pallas_call(
        functools.partial(_out_side_kernel, final_norm=final_norm, n_prompt_tiles=n_tiles, n_cast=n_cast),
        grid=(n_tiles + 1,),
        in_specs=[tile] * 4 + [smp] * 4 + w_specs + cast_in,
        out_specs=[tile, smp] + cast_out,
        out_shape=([jax.ShapeDtypeStruct((rows, D_MODEL), F32), jax.ShapeDtypeStruct((rows_s, D_MODEL), F32)]
                   + [jax.ShapeDtypeStruct((D_MODEL, D_MODEL), BF16)] * n_cast),
        scratch_shapes=[pltpu.VMEM((TM_OUT, D_MODEL), BF16)],
        compiler_params=_params(1),
        name="out_side",
    )(x2d, ac, ap, aa, xs2d, acs, aps, aas, *weights, *([next_w_in] * n_cast))
    return y, ys, (next_w_bf16 or None)


def kernel(x_prompt, x_sample, mem_prompt, cache_mem_k, cache_mem_v, state_conv, state_pool, norm_g, w_in, conv_w, pool_w, pool_scale, mem_norm_g, w_mem_kv, w_br_conv, w_br_pool, w_br_att, w_out, final_norm_g):
    depth = w_in.shape[0]
    b_p, t_p, _ = x_prompt.shape
    n_b, n_t, _ = x_sample.shape
    rows_p, rows_s = b_p * t_p, n_b * n_t

    k_f, v_f, k_b, v_b, w_bf16 = _mem_kv(mem_prompt.reshape(b_p * N_MEM, D_MODEL), mem_norm_g, w_mem_kv, w_in)

    sw = dict(
        g=norm_g.reshape(depth, 1, D_MODEL), conv_w=conv_w, pool_w=pool_w.astype(BF16),
        pool_scale=pool_scale.reshape(depth, 1, D_MODEL), fg=final_norm_g.reshape(1, D_MODEL))
    w_branch_f32 = (w_br_conv, w_br_pool, w_br_att, w_out)

    cst_t = jnp.transpose(state_conv, (0, 2, 1, 3))
    pst_t = jnp.transpose(state_pool, (0, 2, 1, 3))

    xp, xs = x_prompt, x_sample.reshape(rows_s, D_MODEL)
    cv_p, pl_p, cv_s, pl_s = [], [], [], []
    flat = lambda a: a.reshape(rows_p, D_MODEL)
    for l in range(depth):
        ac_s, ap_s, q, sg, c_new_s, p_new_s, w_bf16 = _sample_in(
            xs, cst_t, pst_t, sw["g"], w_in, w_bf16, conv_w, pool_w, sw["pool_scale"], l, n_t, n_b)
        cv_s.append(c_new_s)
        pl_s.append(p_new_s)

        ac_p, ap_p, aa_p, aa_s, c_new, p_new, *w_branch = _prompt_in(
            xp, k_b, v_b, q, sg, cache_mem_k, cache_mem_v, w_bf16[:N_PROJ], w_branch_f32, sw, l, n_t)
        cv_p.append(c_new)
        pl_p.append(p_new)

        last = l == depth - 1
        xp, xs, w_bf16 = _out_side(flat(xp), flat(ac_p), flat(ap_p), flat(aa_p), xs, ac_s, ap_s, aa_s,
                                   w_bf16[N_PROJ:], w_branch, sw, l, last, None if last else w_in)
        xp = xp.reshape(b_p, t_p, D_MODEL)

    batch_first = lambda hist: jnp.transpose(jnp.stack(hist), (0, 2, 1, 3))
    return (xp, xs.reshape(n_b, n_t, D_MODEL), k_f, v_f, jnp.stack(cv_p), jnp.stack(pl_p),
            batch_first(cv_s), batch_first(pl_s))
```

```python
import functools

import jax
import jax.numpy as jnp
from jax import lax
from jax.experimental import pallas as pl
from jax.experimental.pallas import tpu as pltpu

D_MODEL = 1024
N_MEM = 256
HEADS = 4
HEAD_DIM = 256
CONV_WIDTH = 3
POOL_WINDOWS = (2, 4, 8, 16)
POOL_GROUP = 256
POOL_STATE = 15
PAST_LEN = 16384
EPS = 1e-6
OFF_HC, OFF_BC, OFF_CC, OFF_GC = 0, 1024, 2048, 3072
OFF_HP, OFF_GP = 4096, 5120
OFF_Q = 6144
OFF_GA = 7168
OFF_MERGE = 8192
N_MERGE = 3
N_PROJ = 8
N_CAST_ONLY = N_MERGE

CW = 256
N_CHUNKS = D_MODEL // CW
assert CW == POOL_GROUP == HEAD_DIM
assert POOL_WINDOWS == tuple(2 ** (g + 1) for g in range(len(POOL_WINDOWS)))
TM = 512
TM_OUT = 512
POOL_PAD = 32
CONV_PAD = 8
VMEM_BYTES_V7X = 64 * 1024 * 1024
VMEM_LIMIT = VMEM_BYTES_V7X - 2 * 1024 * 1024

F32 = jnp.float32
BF16 = jnp.bfloat16


def _dot(a, b):
    return jnp.dot(a, b, preferred_element_type=F32)


def _dot_nt(a, b):
    return lax.dot_general(a, b, (((1,), (1,)), ((), ())), preferred_element_type=F32)


def _sigmoid(x):
    return 0.5 * jnp.tanh(0.5 * x) + 0.5


def _silu(x):
    return x * _sigmoid(x)


def _rmsnorm(x, g):
    ms = jnp.mean(x * x, axis=-1, keepdims=True)
    return (x * lax.rsqrt(ms + EPS)) * g


def _softmax(s):
    e = jnp.exp(s - jnp.max(s, axis=-1, keepdims=True))
    return e * (1.0 / jnp.sum(e, axis=-1, keepdims=True))


def _cols(off, c, w=CW):
    return slice(off + c * w, off + (c + 1) * w)


def _params(n_axes):
    return pltpu.CompilerParams(
        dimension_semantics=("arbitrary",) * n_axes, vmem_limit_bytes=VMEM_LIMIT)


def _const_spec(shape):
    nd = len(shape)
    return pl.BlockSpec(shape, lambda *_: (0,) * nd, pipeline_mode=pl.Buffered(1))


def _layer_spec(arr, layer, cols=None, col_block=0):
    tail = list(arr.shape[1:])
    idx = [0] * len(tail)
    if cols is not None:
        tail[-1] = cols
        idx[-1] = col_block
    return pl.BlockSpec((None, *tail), lambda *_: (layer, *idx), pipeline_mode=pl.Buffered(1))


def _kv_kernel(mem_ref, g_ref, w_ref, k_ref, v_ref, kb_ref, vb_ref):
    h = _rmsnorm(mem_ref[...], g_ref[0]).astype(BF16)
    k = _dot(h, w_ref[0, :, :D_MODEL].astype(BF16))
    v = _dot(h, w_ref[0, :, D_MODEL:].astype(BF16))
    k_ref[0] = k.reshape(k_ref.shape[1:])
    v_ref[0] = v.reshape(v_ref.shape[1:])
    kb_ref[0] = k.astype(BF16)
    vb_ref[0] = v.astype(BF16)


def _mem_kv(mem2d, mem_norm_g, w_kv):
    depth = w_kv.shape[0]
    rows = mem2d.shape[0]
    bt = 4
    rt = bt * N_MEM
    out_f = jax.ShapeDtypeStruct((depth, rows // N_MEM, N_MEM, HEADS, HEAD_DIM), F32)
    out_b = jax.ShapeDtypeStruct((depth, rows, D_MODEL), BF16)
    blk = pl.BlockSpec((1, rt, D_MODEL), lambda l, r: (l, r, 0))
    blk5 = pl.BlockSpec((1, bt, N_MEM, HEADS, HEAD_DIM), lambda l, r: (l, r, 0, 0, 0))
    return pl.pallas_call(
        _kv_kernel,
        grid=(depth, rows // rt),
        in_specs=[
            pl.BlockSpec((rt, D_MODEL), lambda l, r: (r, 0)),
            pl.BlockSpec((1, 1, D_MODEL), lambda l, r: (l, 0, 0)),
            pl.BlockSpec((1, D_MODEL, 2 * D_MODEL), lambda l, r: (l, 0, 0)),
        ],
        out_specs=[blk5, blk5, blk, blk],
        out_shape=[out_f, out_f, out_b, out_b],
        compiler_params=_params(2),
        name="mem_kv",
    )(mem2d, mem_norm_g.reshape(depth, 1, D_MODEL), w_kv)


KV_TILE = CW // HEADS
N_KV_TILES = N_MEM // KV_TILE


def _sample_attention_tiles(sq_ref, ssg_ref, sk_ref, sv_ref, saa_ref, sb, n_t):
    n_flat = N_MEM * HEADS
    col_head = lax.broadcasted_iota(jnp.int32, (HEADS * n_t, n_flat), 1) % HEADS
    row_head = lax.broadcasted_iota(jnp.int32, (HEADS * n_t, n_flat), 0) // n_t
    own = col_head == row_head
    for i in range(sb):
        rows = slice(i * n_t, (i + 1) * n_t)
        qb = sq_ref[rows, :]
        qe = jnp.concatenate([qb[:, _cols(0, hh, HEAD_DIM)] for hh in range(HEADS)], axis=0).astype(BF16)
        s = []
        for j in range(N_KV_TILES):
            kf = sk_ref[0, i, j * KV_TILE:(j + 1) * KV_TILE].reshape(CW, HEAD_DIM).astype(BF16)
            s.append(_dot_nt(qe, kf))
            yield
        s = jnp.where(own, jnp.concatenate(s, axis=1) * HEAD_DIM ** -0.5, -jnp.inf)
        p = _softmax(s).astype(BF16)
        o = None
        for j in range(N_KV_TILES):
            vf = sv_ref[0, i, j * KV_TILE:(j + 1) * KV_TILE].reshape(CW, HEAD_DIM).astype(BF16)
            part = _dot(p[:, _cols(0, j)], vf)
            o = part if o is None else o + part
            yield
        o = jnp.concatenate([o[hh * n_t:(hh + 1) * n_t] for hh in range(HEADS)], axis=1)
        saa_ref[rows, :] = o * ssg_ref[rows, :]


def _prompt_in_kernel(*refs, sb, n_t, n_cast):
    refs = list(refs)
    (x_ref, k_ref, v_ref, sq_ref, ssg_ref, sk_ref, sv_ref,
     g_ref, whp_ref, wq_ref, whc_ref, wcc_ref, wbc_ref, wgc_ref, wgp_ref, wga_ref,
     convw_ref, poolw_ref, pscale_ref) = refs[:19]
    cast_in = refs[19:19 + n_cast]
    ac_ref, ap_ref, aa_ref, saa_ref, cst_ref, pst_ref = refs[19 + n_cast:25 + n_cast]
    cast_out = refs[25 + n_cast:25 + 2 * n_cast]
    u_s, p_s, sa_s, sb_s = refs[25 + 2 * n_cast:]
    t = pl.program_id(1)

    @pl.when(t == 0)
    def _():
        u_s[0:CONV_PAD, :] = jnp.zeros((CONV_PAD, D_MODEL), F32)
        p_s[0:POOL_PAD, :] = jnp.zeros((POOL_PAD, D_MODEL), F32)
        sa_s[0:16, :] = jnp.zeros((16, POOL_GROUP), F32)
        sb_s[0:16, :] = jnp.zeros((16, POOL_GROUP), F32)

    for w_ref, wb_ref in zip(cast_in, cast_out):
        wb_ref[...] = w_ref[...].astype(BF16)

    sample_attn = _sample_attention_tiles(sq_ref, ssg_ref, sk_ref, sv_ref, saa_ref, sb, n_t)
    for _ in range(N_KV_TILES):
        next(sample_attn, None)
    x = x_ref[0]
    h = _rmsnorm(x, g_ref[...]).astype(BF16)

    def hdot(w_ref, c):
        r = _dot(h, w_ref[:, _cols(0, c)])
        next(sample_attn, None)
        return r

    pos1 = (t * TM + 1 + lax.broadcasted_iota(jnp.int32, (TM, 1), 0)).astype(F32)
    n = TM + 16
    for c, w in enumerate(POOL_WINDOWS):
        sl = _cols(0, c)

        hp = hdot(whp_ref, c)
        p_s[POOL_PAD:POOL_PAD + TM, sl] = hp
        cur = p_s[16:16 + n, sl] + p_s[15:15 + n, sl]
        shift, src, dst = 2, sa_s, sb_s
        while shift < w:
            src[16:16 + n, :] = cur
            cur = src[16:16 + n, :] + src[16 - shift:16 - shift + n, :]
            shift *= 2
            src, dst = dst, src
        inv_cnt = 1.0 / jnp.minimum(pos1, float(w))
        mixed = (cur[16:, :] * inv_cnt - hp).astype(BF16)

        q = hdot(wq_ref, c).astype(BF16)

        hc = hdot(whc_ref, c)
        cc = hdot(wcc_ref, c)
        u = cc * hc
        u_s[CONV_PAD:CONV_PAD + TM, sl] = u
        u1 = u_s[CONV_PAD - 1:CONV_PAD - 1 + TM, sl]
        u2 = u_s[CONV_PAD - 2:CONV_PAD - 2 + TM, sl]
        y = convw_ref[0:1, sl] * u2 + convw_ref[1:2, sl] * u1 + convw_ref[2:3, sl] * u

        p = _softmax(_dot_nt(q, k_ref[:, sl]) * HEAD_DIM ** -0.5).astype(BF16)

        bc = hdot(wbc_ref, c)
        gc = hdot(wgc_ref, c)
        ac_ref[0, :, sl] = (bc * y * _silu(gc)).astype(BF16)

        gp = hdot(wgp_ref, c)
        pooled = _dot(mixed, poolw_ref[c]) * pscale_ref[:, sl]
        ap_ref[0, :, sl] = (pooled * _silu(gp)).astype(BF16)

        ga = hdot(wga_ref, c)
        o = _dot(p, v_ref[:, sl])
        aa_ref[0, :, sl] = (o * _silu(ga)).astype(BF16)

    new_conv = u_s[CONV_PAD + TM - 2:CONV_PAD + TM, :]
    cst_ref[0] = new_conv
    u_s[CONV_PAD - 2:CONV_PAD, :] = new_conv
    pst_ref[0] = p_s[POOL_PAD + TM - POOL_STATE:POOL_PAD + TM, :]
    p_s[16:POOL_PAD, :] = p_s[TM + 16:TM + POOL_PAD, :]

    for _ in sample_attn:
        pass


def _prompt_in(x, kb, vb, sq, ssg, cache_k, cache_v, w_a, w_to_cast, sw, layer, n_t):
    b, t, _ = x.shape
    nt = t // TM
    n_cast = len(w_to_cast)
    cast_rows = D_MODEL // (b * nt)
    assert cast_rows * b * nt == D_MODEL
    n_b = sq.shape[0] // n_t
    sb = n_b // (b * nt)
    assert sb * b * nt == n_b
    tile = pl.BlockSpec((1, TM, D_MODEL), lambda i, j: (i, j, 0))
    smp = pl.BlockSpec((sb * n_t, D_MODEL), lambda i, j: (i * nt + j, 0))
    skv = pl.BlockSpec((1, sb, N_MEM, HEADS, HEAD_DIM), lambda i, j: (layer, i * nt + j, 0, 0, 0))
    act = jax.ShapeDtypeStruct((b, t, D_MODEL), BF16)
    return pl.pallas_call(
        functools.partial(_prompt_in_kernel, sb=sb, n_t=n_t, n_cast=n_cast),
        grid=(b, nt),
        in_specs=[
            tile,
            pl.BlockSpec((None, N_MEM, D_MODEL), lambda i, j: (layer, i, 0)),
            pl.BlockSpec((None, N_MEM, D_MODEL), lambda i, j: (layer, i, 0)),
            smp, smp, skv, skv,
            _layer_spec(sw["g"], layer),
        ] + [_const_spec(w.shape) for w in w_a] + [
            _layer_spec(sw["conv_w"], layer), _layer_spec(sw["pool_w"], layer), _layer_spec(sw["pool_scale"], layer),
        ] + [pl.BlockSpec((None, cast_rows, D_MODEL), lambda i, j: (layer, i * nt + j, 0))] * n_cast,
        out_specs=[
            tile, tile, tile, smp,
            pl.BlockSpec((1, CONV_WIDTH - 1, D_MODEL), lambda i, j: (i, 0, 0)),
            pl.BlockSpec((1, POOL_STATE, D_MODEL), lambda i, j: (i, 0, 0)),
        ] + [pl.BlockSpec((cast_rows, D_MODEL), lambda i, j: (i * nt + j, 0))] * n_cast,
        out_shape=[
            act, act, act,
            jax.ShapeDtypeStruct(sq.shape, F32),
            jax.ShapeDtypeStruct((b, CONV_WIDTH - 1, D_MODEL), F32),
            jax.ShapeDtypeStruct((b, POOL_STATE, D_MODEL), F32),
        ] + [jax.ShapeDtypeStruct((D_MODEL, D_MODEL), BF16)] * n_cast,
        scratch_shapes=[
            pltpu.VMEM((CONV_PAD + TM, D_MODEL), F32),
            pltpu.VMEM((POOL_PAD + TM, D_MODEL), F32),
            pltpu.VMEM((POOL_PAD + TM, POOL_GROUP), F32),
            pltpu.VMEM((POOL_PAD + TM, POOL_GROUP), F32),
        ],
        compiler_params=_params(2),
        name="prompt_in",
    )(x, kb, vb, sq, ssg, cache_k, cache_v, sw["g"], *w_a, sw["conv_w"], sw["pool_w"], sw["pool_scale"], *w_to_cast)


def _sample_in_kernel(*refs, n_t, n_b, cast_weights):
    refs = list(refs)
    n_w = N_PROJ + N_CAST_ONLY if cast_weights else N_PROJ
    x_ref, tb_ref, bt_ref, cst_ref, pst_ref, g_ref = refs[:6]
    w_refs = refs[6:6 + n_w]
    convw_ref, poolw_ref, pscale_ref = refs[6 + n_w:9 + n_w]
    outs = refs[9 + n_w:]
    ac_ref, ap_ref, q_ref, sg_ref, ncst_ref, npst_ref = outs[:6]
    wb_refs = outs[6:-2]
    hb_s, ht_s = outs[-2:]
    c = pl.program_id(0)

    @pl.when(c == 0)
    def _():
        hb = _rmsnorm(x_ref[...], g_ref[...]).astype(BF16)
        hb_s[...] = hb
        ht_s[...] = _dot(tb_ref[...], hb).astype(BF16)

    hb, ht = hb_s[...], ht_s[...]
    w = [w_ref[...].astype(BF16) for w_ref in w_refs]
    for w_cast, wb_ref in zip(w, wb_refs):
        wb_ref[...] = w_cast
    whp, wq, whc, wcc, wbc, wgc, wgp, wga = w[:N_PROJ]

    def to_batch_major(a):
        return _dot(bt_ref[...], a.astype(BF16)).astype(BF16)

    def planes(a):
        return [a[i * n_b:(i + 1) * n_b] for i in range(n_t)]

    hp = _dot(ht, whp)
    u = _dot(ht, wcc) * _dot(ht, whc)
    bc = _dot(ht, wbc)
    silu_gc = _silu(_dot(ht, wgc))
    silu_gp = _silu(_dot(ht, wgp))
    q_ref[...] = _dot(hb, wq)
    sg_ref[...] = _silu(_dot(hb, wga))

    ext = [pst_ref[j] for j in range(POOL_STATE)] + planes(hp)
    for j, plane in enumerate(ext[-POOL_STATE:]):
        npst_ref[j] = plane
    wsum = {}

    def window_sum(k, i):
        if k == 0:
            return ext[i]
        if (k, i) not in wsum:
            wsum[(k, i)] = window_sum(k - 1, i) + window_sum(k - 1, i - 2 ** (k - 1))
        return wsum[(k, i)]

    mixed = []
    for i in range(n_t):
        mean = None
        for g, win in enumerate(POOL_WINDOWS):
            cnt = float(min(PAST_LEN + i + 1, win))
            cand = window_sum(g + 1, POOL_STATE + i) * (1.0 / cnt)
            mean = cand if mean is None else jnp.where(c == g, cand, mean)
        mixed.append(mean - ext[POOL_STATE + i])
    mixed = jnp.concatenate(mixed, axis=0).astype(BF16)
    pooled = _dot(mixed, poolw_ref[...].astype(BF16)) * pscale_ref[...]
    ap_ref[...] = to_batch_major(pooled * silu_gp)

    ext = [cst_ref[0], cst_ref[1]] + planes(u)
    y = jnp.concatenate(
        [convw_ref[0:1, :] * ext[i] + convw_ref[1:2, :] * ext[i + 1] + convw_ref[2:3, :] * ext[i + 2]
         for i in range(n_t)], axis=0)
    ac_ref[...] = to_batch_major(bc * y * silu_gc)
    for j, plane in enumerate(ext[-(CONV_WIDTH - 1):]):
        ncst_ref[j] = plane


def _row_permutations(n_t, n_b):
    r = jnp.arange(n_t * n_b)
    tb = jax.nn.one_hot((r % n_b) * n_t + r // n_b, n_t * n_b, dtype=BF16)
    return tb, tb.T


W_IN_BLOCK_OFFSETS = (OFF_HP, OFF_Q, OFF_HC, OFF_CC, OFF_BC, OFF_GC, OFF_GP, OFF_GA) + tuple(
    OFF_MERGE + k * D_MODEL for k in range(N_MERGE))
assert len(W_IN_BLOCK_OFFSETS) == N_PROJ + N_CAST_ONLY


def _sample_in(xs2d, cst_t, pst_t, g, w_in, w_bf16, conv_w, pool_w, pool_scale, layer, n_t, n_b):
    rows = n_t * n_b
    tb, bt = _row_permutations(n_t, n_b)
    cast_weights = w_bf16 is None
    if cast_weights:
        weights = [w_in] * len(W_IN_BLOCK_OFFSETS)
        w_specs = [pl.BlockSpec((None, D_MODEL, CW), functools.partial(lambda c, blk: (layer, 0, blk + c), blk=off // CW))
                   for off in W_IN_BLOCK_OFFSETS]
    else:
        weights = list(w_bf16[:N_PROJ])
        w_specs = [pl.BlockSpec((D_MODEL, CW), lambda c: (0, c))] * N_PROJ
    n_emit = len(weights) if cast_weights else 0
    chunk2d = pl.BlockSpec((rows, CW), lambda c: (0, c))

    def hist(n_rows):
        return pl.BlockSpec((None, n_rows, n_b, CW), lambda c: (layer, 0, 0, c))

    def new_hist(n_rows):
        return pl.BlockSpec((n_rows, n_b, CW), lambda c: (0, 0, c))

    outs = pl.pallas_call(
        functools.partial(_sample_in_kernel, n_t=n_t, n_b=n_b, cast_weights=cast_weights),
        grid=(N_CHUNKS,),
        in_specs=[
            _const_spec(xs2d.shape), _const_spec(tb.shape), _const_spec(bt.shape),
            hist(CONV_WIDTH - 1), hist(POOL_STATE), _layer_spec(g, layer),
        ] + w_specs + [
            pl.BlockSpec((None, CONV_WIDTH, CW), lambda c: (layer, 0, c)),
            pl.BlockSpec((None, None, POOL_GROUP, POOL_GROUP), lambda c: (layer, c, 0, 0)),
            pl.BlockSpec((None, 1, CW), lambda c: (layer, 0, c)),
        ],
        out_specs=([chunk2d] * 4 + [new_hist(CONV_WIDTH - 1), new_hist(POOL_STATE)]
                   + [pl.BlockSpec((D_MODEL, CW), lambda c: (0, c))] * n_emit),
        out_shape=[
            jax.ShapeDtypeStruct((rows, D_MODEL), BF16),
            jax.ShapeDtypeStruct((rows, D_MODEL), BF16),
            jax.ShapeDtypeStruct((rows, D_MODEL), F32),
            jax.ShapeDtypeStruct((rows, D_MODEL), F32),
            jax.ShapeDtypeStruct((CONV_WIDTH - 1, n_b, D_MODEL), F32),
            jax.ShapeDtypeStruct((POOL_STATE, n_b, D_MODEL), F32),
        ] + [jax.ShapeDtypeStruct((D_MODEL, D_MODEL), BF16)] * n_emit,
        scratch_shapes=[pltpu.VMEM((rows, D_MODEL), BF16), pltpu.VMEM((rows, D_MODEL), BF16)],
        compiler_params=_params(1),
        name="sample_in",
    )(xs2d, tb, bt, cst_t, pst_t, g, *weights, conv_w, pool_w, pool_scale)
    return tuple(outs[:6]) + (list(outs[6:]) if cast_weights else list(w_bf16),)


def _out_side_kernel(*refs, final_norm, n_prompt_tiles, n_cast):
    refs = list(refs)
    (x_ref, ac_ref, ap_ref, aa_ref, xs_ref, acs_ref, aps_ref, aas_ref,
     g_ref, wmc_ref, wmp_ref, wma_ref, wbc_ref, wbp_ref, wba_ref, wout_ref, fg_ref) = refs[:17]
    next_w_refs = refs[17:17 + n_cast]
    y_ref, ys_ref = refs[17 + n_cast:19 + n_cast]
    next_wb_refs = refs[19 + n_cast:19 + 2 * n_cast]
    m_s = refs[-1]

    def tile(x_ref, ac_ref, ap_ref, aa_ref, y_ref):
        rows = x_ref.shape[0]
        x = x_ref[...]
        h = _rmsnorm(x, g_ref[...]).astype(BF16)
        ac, ap, aa = ac_ref[...], ap_ref[...], aa_ref[...].astype(BF16)
        for c in range(N_CHUNKS):
            sl = _cols(0, c)
            conv_br = _dot(ac, wbc_ref[:, sl])
            pool_br = _dot(ap, wbp_ref[:, sl])
            att_br = _dot(aa, wba_ref[:, sl])
            mc = _dot(h, wmc_ref[:, sl])
            mp = _dot(h, wmp_ref[:, sl])
            ma = _dot(h, wma_ref[:, sl])
            merged = _sigmoid(mc) * conv_br + _sigmoid(mp) * pool_br + _sigmoid(ma) * att_br
            m_s[0:rows, sl] = merged.astype(BF16)
        xn = x + _dot(m_s[0:rows, :], wout_ref[...])
        if final_norm:
            xn = _rmsnorm(xn, fg_ref[...])
        y_ref[...] = xn

    r = pl.program_id(0)

    @pl.when(r < n_prompt_tiles)
    def _():
        for w_ref, wb_ref in zip(next_w_refs, next_wb_refs):
            wb_ref[...] = w_ref[...].astype(BF16)
        tile(x_ref, ac_ref, ap_ref, aa_ref, y_ref)

    @pl.when(r == n_prompt_tiles)
    def _():
        tile(xs_ref, acs_ref, aps_ref, aas_ref, ys_ref)


def _out_side(x2d, ac, ap, aa, xs2d, acs, aps, aas, w_merge, w_branch, sw, layer, final_norm, next_w_in=None):
    rows, rows_s = x2d.shape[0], xs2d.shape[0]
    n_tiles = rows // TM_OUT
    assert n_tiles * TM_OUT == rows and rows_s <= TM_OUT
    n_cast = 0 if next_w_in is None else len(W_IN_BLOCK_OFFSETS)
    cast_rows = D_MODEL // n_tiles
    assert cast_rows * n_tiles == D_MODEL
    cast_in = [pl.BlockSpec((None, cast_rows, D_MODEL),
                            functools.partial(lambda r, blk: (layer + 1, jnp.minimum(r, n_tiles - 1), blk),
                                              blk=off // D_MODEL))
               for off in W_IN_BLOCK_OFFSETS[:n_cast]]
    cast_out = [pl.BlockSpec((cast_rows, D_MODEL), lambda r: (jnp.minimum(r, n_tiles - 1), 0))] * n_cast
    tile = pl.BlockSpec((TM_OUT, D_MODEL), lambda r: (jnp.minimum(r, n_tiles - 1), 0))
    weights = (sw["g"], *w_merge, *w_branch, sw["fg"])
    w_specs = ([_layer_spec(sw["g"], layer)] + [_const_spec(w.shape) for w in (*w_merge, *w_branch)]
               + [_const_spec(sw["fg"].shape)])
    smp = _const_spec((rows_s, D_MODEL))
    y, ys, *next_w_bf16 = pl.pallas_call(
        functools.partial(_out_side_kernel, final_norm=final_norm, n_prompt_tiles=n_tiles, n_cast=n_cast),
        grid=(n_tiles + 1,),
        in_specs=[tile] * 4 + [smp] * 4 + w_specs + cast_in,
        out_specs=[tile, smp] + cast_out,
        out_shape=([jax.ShapeDtypeStruct((rows, D_MODEL), F32), jax.ShapeDtypeStruct((rows_s, D_MODEL), F32)]
                   + [jax.ShapeDtypeStruct((D_MODEL, D_MODEL), BF16)] * n_cast),
        scratch_shapes=[pltpu.VMEM((TM_OUT, D_MODEL), BF16)],
        compiler_params=_params(1),
        name="out_side",
    )(x2d, ac, ap, aa, xs2d, acs, aps, aas, *weights, *([next_w_in] * n_cast))
    return y, ys, (next_w_bf16 or None)


def kernel(x_prompt, x_sample, mem_prompt, cache_mem_k, cache_mem_v, state_conv, state_pool, norm_g, w_in, conv_w, pool_w, pool_scale, mem_norm_g, w_mem_kv, w_br_conv, w_br_pool, w_br_att, w_out, final_norm_g):
    depth = w_in.shape[0]
    b_p, t_p, _ = x_prompt.shape
    n_b, n_t, _ = x_sample.shape
    rows_p, rows_s = b_p * t_p, n_b * n_t

    k_f, v_f, k_b, v_b = _mem_kv(mem_prompt.reshape(b_p * N_MEM, D_MODEL), mem_norm_g, w_mem_kv)

    sw = dict(
        g=norm_g.reshape(depth, 1, D_MODEL), conv_w=conv_w, pool_w=pool_w.astype(BF16),
        pool_scale=pool_scale.reshape(depth, 1, D_MODEL), fg=final_norm_g.reshape(1, D_MODEL))
    w_branch_f32 = (w_br_conv, w_br_pool, w_br_att, w_out)

    cst_t = jnp.transpose(state_conv, (0, 2, 1, 3))
    pst_t = jnp.transpose(state_pool, (0, 2, 1, 3))

    xp, xs = x_prompt, x_sample.reshape(rows_s, D_MODEL)
    cv_p, pl_p, cv_s, pl_s = [], [], [], []
    flat = lambda a: a.reshape(rows_p, D_MODEL)
    w_bf16 = None
    for l in range(depth):
        ac_s, ap_s, q, sg, c_new_s, p_new_s, w_bf16 = _sample_in(
            xs, cst_t, pst_t, sw["g"], w_in, w_bf16, conv_w, pool_w, sw["pool_scale"], l, n_t, n_b)
        cv_s.append(c_new_s)
        pl_s.append(p_new_s)

        ac_p, ap_p, aa_p, aa_s, c_new, p_new, *w_branch = _prompt_in(
            xp, k_b, v_b, q, sg, cache_mem_k, cache_mem_v, w_bf16[:N_PROJ], w_branch_f32, sw, l, n_t)
        cv_p.append(c_new)
        pl_p.append(p_new)

        last = l == depth - 1
        xp, xs, w_bf16 = _out_side(flat(xp), flat(ac_p), flat(ap_p), flat(aa_p), xs, ac_s, ap_s, aa_s,
                                   w_bf16[N_PROJ:], w_branch, sw, l, last, None if last else w_in)
        xp = xp.reshape(b_p, t_p, D_MODEL)

    batch_first = lambda hist: jnp.transpose(jnp.stack(hist), (0, 2, 1, 3))
    return (xp, xs.reshape(n_b, n_t, D_MODEL), k_f, v_f, jnp.stack(cv_p), jnp.stack(pl_p),
            batch_first(cv_s), batch_first(pl_s))
```

```python
import functools

import jax
import jax.numpy as jnp
from jax import lax
from jax.experimental import pallas as pl
from jax.experimental.pallas import tpu as pltpu

D_MODEL = 1024
N_MEM = 256
HEADS = 4
HEAD_DIM = 256
CONV_WIDTH = 3
POOL_WINDOWS = (2, 4, 8, 16)
POOL_GROUP = 256
POOL_STATE = 15
PAST_LEN = 16384
EPS = 1e-6
OFF_HC, OFF_BC, OFF_CC, OFF_GC = 0, 1024, 2048, 3072
OFF_HP, OFF_GP = 4096, 5120
OFF_Q = 6144
OFF_GA = 7168
OFF_MERGE = 8192
N_MERGE = 3
N_PROJ = 8
N_CAST_ONLY = N_MERGE

CW = 256
N_CHUNKS = D_MODEL // CW
assert CW == POOL_GROUP == HEAD_DIM
assert POOL_WINDOWS == tuple(2 ** (g + 1) for g in range(len(POOL_WINDOWS)))
TM = 512
TM_OUT = 512
POOL_PAD = 32
CONV_PAD = 8
VMEM_BYTES_V7X = 64 * 1024 * 1024
VMEM_LIMIT = VMEM_BYTES_V7X - 2 * 1024 * 1024

F32 = jnp.float32
BF16 = jnp.bfloat16


def _dot(a, b):
    return jnp.dot(a, b, preferred_element_type=F32)


def _dot_nt(a, b):
    return lax.dot_general(a, b, (((1,), (1,)), ((), ())), preferred_element_type=F32)


def _sigmoid(x):
    return 0.5 * jnp.tanh(0.5 * x) + 0.5


def _silu(x):
    half = 0.5 * x
    return half * (1.0 + jnp.tanh(half))


def _rmsnorm(x, g):
    ms = jnp.mean(x * x, axis=-1, keepdims=True)
    return (x * lax.rsqrt(ms + EPS)) * g


def _softmax(s):
    e = jnp.exp(s - jnp.max(s, axis=-1, keepdims=True))
    return e * (1.0 / jnp.sum(e, axis=-1, keepdims=True))


def _cols(off, c, w=CW):
    return slice(off + c * w, off + (c + 1) * w)


def _params(n_axes):
    return pltpu.CompilerParams(
        dimension_semantics=("arbitrary",) * n_axes, vmem_limit_bytes=VMEM_LIMIT)


def _const_spec(shape):
    nd = len(shape)
    return pl.BlockSpec(shape, lambda *_: (0,) * nd, pipeline_mode=pl.Buffered(1))


def _layer_spec(arr, layer, cols=None, col_block=0):
    tail = list(arr.shape[1:])
    idx = [0] * len(tail)
    if cols is not None:
        tail[-1] = cols
        idx[-1] = col_block
    return pl.BlockSpec((None, *tail), lambda *_: (layer, *idx), pipeline_mode=pl.Buffered(1))


def _kv_kernel(mem_ref, g_ref, w_ref, k_ref, v_ref, kb_ref, vb_ref):
    h = _rmsnorm(mem_ref[...], g_ref[0]).astype(BF16)
    k = _dot(h, w_ref[0, :, :D_MODEL].astype(BF16))
    v = _dot(h, w_ref[0, :, D_MODEL:].astype(BF16))
    k_ref[0] = k.reshape(k_ref.shape[1:])
    v_ref[0] = v.reshape(v_ref.shape[1:])
    kb_ref[0] = k.astype(BF16)
    vb_ref[0] = v.astype(BF16)


def _mem_kv(mem2d, mem_norm_g, w_kv):
    depth = w_kv.shape[0]
    rows = mem2d.shape[0]
    bt = 4
    rt = bt * N_MEM
    out_f = jax.ShapeDtypeStruct((depth, rows // N_MEM, N_MEM, HEADS, HEAD_DIM), F32)
    out_b = jax.ShapeDtypeStruct((depth, rows, D_MODEL), BF16)
    blk = pl.BlockSpec((1, rt, D_MODEL), lambda l, r: (l, r, 0))
    blk5 = pl.BlockSpec((1, bt, N_MEM, HEADS, HEAD_DIM), lambda l, r: (l, r, 0, 0, 0))
    return pl.pallas_call(
        _kv_kernel,
        grid=(depth, rows // rt),
        in_specs=[
            pl.BlockSpec((rt, D_MODEL), lambda l, r: (r, 0)),
            pl.BlockSpec((1, 1, D_MODEL), lambda l, r: (l, 0, 0)),
            pl.BlockSpec((1, D_MODEL, 2 * D_MODEL), lambda l, r: (l, 0, 0)),
        ],
        out_specs=[blk5, blk5, blk, blk],
        out_shape=[out_f, out_f, out_b, out_b],
        compiler_params=_params(2),
        name="mem_kv",
    )(mem2d, mem_norm_g.reshape(depth, 1, D_MODEL), w_kv)


KV_TILE = CW // HEADS
N_KV_TILES = N_MEM // KV_TILE


def _sample_attention_tiles(sq_ref, ssg_ref, sk_ref, sv_ref, saa_ref, sb, n_t):
    n_flat = N_MEM * HEADS
    col_head = lax.broadcasted_iota(jnp.int32, (HEADS * n_t, n_flat), 1) % HEADS
    row_head = lax.broadcasted_iota(jnp.int32, (HEADS * n_t, n_flat), 0) // n_t
    own = col_head == row_head
    for i in range(sb):
        rows = slice(i * n_t, (i + 1) * n_t)
        qb = sq_ref[rows, :]
        qe = jnp.concatenate([qb[:, _cols(0, hh, HEAD_DIM)] for hh in range(HEADS)], axis=0).astype(BF16)
        s = []
        for j in range(N_KV_TILES):
            kf = sk_ref[0, i, j * KV_TILE:(j + 1) * KV_TILE].reshape(CW, HEAD_DIM).astype(BF16)
            s.append(_dot_nt(qe, kf))
            yield
        s = jnp.where(own, jnp.concatenate(s, axis=1) * HEAD_DIM ** -0.5, -jnp.inf)
        p = _softmax(s).astype(BF16)
        o = None
        for j in range(N_KV_TILES):
            vf = sv_ref[0, i, j * KV_TILE:(j + 1) * KV_TILE].reshape(CW, HEAD_DIM).astype(BF16)
            part = _dot(p[:, _cols(0, j)], vf)
            o = part if o is None else o + part
            yield
        o = jnp.concatenate([o[hh * n_t:(hh + 1) * n_t] for hh in range(HEADS)], axis=1)
        saa_ref[rows, :] = o * ssg_ref[rows, :]


def _prompt_in_kernel(*refs, sb, n_t, n_cast):
    refs = list(refs)
    (x_ref, k_ref, v_ref, sq_ref, ssg_ref, sk_ref, sv_ref,
     g_ref, whp_ref, wq_ref, whc_ref, wcc_ref, wbc_ref, wgc_ref, wgp_ref, wga_ref,
     convw_ref, poolw_ref, pscale_ref) = refs[:19]
    cast_in = refs[19:19 + n_cast]
    ac_ref, ap_ref, aa_ref, saa_ref, cst_ref, pst_ref = refs[19 + n_cast:25 + n_cast]
    cast_out = refs[25 + n_cast:25 + 2 * n_cast]
    u_s, p_s, sa_s, sb_s = refs[25 + 2 * n_cast:]
    t = pl.program_id(1)

    @pl.when(t == 0)
    def _():
        u_s[0:CONV_PAD, :] = jnp.zeros((CONV_PAD, D_MODEL), F32)
        p_s[0:POOL_PAD, :] = jnp.zeros((POOL_PAD, D_MODEL), F32)
        sa_s[0:16, :] = jnp.zeros((16, POOL_GROUP), F32)
        sb_s[0:16, :] = jnp.zeros((16, POOL_GROUP), F32)

    for w_ref, wb_ref in zip(cast_in, cast_out):
        wb_ref[...] = w_ref[...].astype(BF16)

    sample_attn = _sample_attention_tiles(sq_ref, ssg_ref, sk_ref, sv_ref, saa_ref, sb, n_t)
    for _ in range(N_KV_TILES):
        next(sample_attn, None)
    x = x_ref[0]
    h = _rmsnorm(x, g_ref[...]).astype(BF16)

    def hdot(w_ref, c):
        r = _dot(h, w_ref[:, _cols(0, c)])
        next(sample_attn, None)
        return r

    pos1 = (t * TM + 1 + lax.broadcasted_iota(jnp.int32, (TM, 1), 0)).astype(F32)
    n = TM + 16
    for c, w in enumerate(POOL_WINDOWS):
        sl = _cols(0, c)

        hp = hdot(whp_ref, c)
        p_s[POOL_PAD:POOL_PAD + TM, sl] = hp
        cur = p_s[16:16 + n, sl] + p_s[15:15 + n, sl]
        shift, src, dst = 2, sa_s, sb_s
        while shift < w:
            src[16:16 + n, :] = cur
            cur = src[16:16 + n, :] + src[16 - shift:16 - shift + n, :]
            shift *= 2
            src, dst = dst, src
        inv_cnt = 1.0 / jnp.minimum(pos1, float(w))
        mixed = (cur[16:, :] * inv_cnt - hp).astype(BF16)

        q = hdot(wq_ref, c).astype(BF16)

        hc = hdot(whc_ref, c)
        cc = hdot(wcc_ref, c)
        u = cc * hc
        u_s[CONV_PAD:CONV_PAD + TM, sl] = u
        u1 = u_s[CONV_PAD - 1:CONV_PAD - 1 + TM, sl]
        u2 = u_s[CONV_PAD - 2:CONV_PAD - 2 + TM, sl]
        y = convw_ref[0:1, sl] * u2 + convw_ref[1:2, sl] * u1 + convw_ref[2:3, sl] * u

        p = _softmax(_dot_nt(q, k_ref[:, sl]) * HEAD_DIM ** -0.5).astype(BF16)

        bc = hdot(wbc_ref, c)
        gc = hdot(wgc_ref, c)
        ac_ref[0, :, sl] = (bc * y * _silu(gc)).astype(BF16)

        gp = hdot(wgp_ref, c)
        pooled = _dot(mixed, poolw_ref[c]) * pscale_ref[:, sl]
        ap_ref[0, :, sl] = (pooled * _silu(gp)).astype(BF16)

        ga = hdot(wga_ref, c)
        o = _dot(p, v_ref[:, sl])
        aa_ref[0, :, sl] = (o * _silu(ga)).astype(BF16)

    new_conv = u_s[CONV_PAD + TM - 2:CONV_PAD + TM, :]
    cst_ref[0] = new_conv
    u_s[CONV_PAD - 2:CONV_PAD, :] = new_conv
    pst_ref[0] = p_s[POOL_PAD + TM - POOL_STATE:POOL_PAD + TM, :]
    p_s[16:POOL_PAD, :] = p_s[TM + 16:TM + POOL_PAD, :]

    for _ in sample_attn:
        pass


def _prompt_in(x, kb, vb, sq, ssg, cache_k, cache_v, w_a, w_to_cast, sw, layer, n_t):
    b, t, _ = x.shape
    nt = t // TM
    n_cast = len(w_to_cast)
    cast_rows = D_MODEL // (b * nt)
    assert cast_rows * b * nt == D_MODEL
    n_b = sq.shape[0] // n_t
    sb = n_b // (b * nt)
    assert sb * b * nt == n_b
    tile = pl.BlockSpec((1, TM, D_MODEL), lambda i, j: (i, j, 0))
    smp = pl.BlockSpec((sb * n_t, D_MODEL), lambda i, j: (i * nt + j, 0))
    skv = pl.BlockSpec((1, sb, N_MEM, HEADS, HEAD_DIM), lambda i, j: (layer, i * nt + j, 0, 0, 0))
    act = jax.ShapeDtypeStruct((b, t, D_MODEL), BF16)
    return pl.pallas_call(
        functools.partial(_prompt_in_kernel, sb=sb, n_t=n_t, n_cast=n_cast),
        grid=(b, nt),
        in_specs=[
            tile,
            pl.BlockSpec((None, N_MEM, D_MODEL), lambda i, j: (layer, i, 0)),
            pl.BlockSpec((None, N_MEM, D_MODEL), lambda i, j: (layer, i, 0)),
            smp, smp, skv, skv,
            _layer_spec(sw["g"], layer),
        ] + [_const_spec(w.shape) for w in w_a] + [
            _layer_spec(sw["conv_w"], layer), _layer_spec(sw["pool_w"], layer), _layer_spec(sw["pool_scale"], layer),
        ] + [pl.BlockSpec((None, cast_rows, D_MODEL), lambda i, j: (layer, i * nt + j, 0))] * n_cast,
        out_specs=[
            tile, tile, tile, smp,
            pl.BlockSpec((1, CONV_WIDTH - 1, D_MODEL), lambda i, j: (i, 0, 0)),
            pl.BlockSpec((1, POOL_STATE, D_MODEL), lambda i, j: (i, 0, 0)),
        ] + [pl.BlockSpec((cast_rows, D_MODEL), lambda i, j: (i * nt + j, 0))] * n_cast,
        out_shape=[
            act, act, act,
            jax.ShapeDtypeStruct(sq.shape, F32),
            jax.ShapeDtypeStruct((b, CONV_WIDTH - 1, D_MODEL), F32),
            jax.ShapeDtypeStruct((b, POOL_STATE, D_MODEL), F32),
        ] + [jax.ShapeDtypeStruct((D_MODEL, D_MODEL), BF16)] * n_cast,
        scratch_shapes=[
            pltpu.VMEM((CONV_PAD + TM, D_MODEL), F32),
            pltpu.VMEM((POOL_PAD + TM, D_MODEL), F32),
            pltpu.VMEM((POOL_PAD + TM, POOL_GROUP), F32),
            pltpu.VMEM((POOL_PAD + TM, POOL_GROUP), F32),
        ],
        compiler_params=_params(2),
        name="prompt_in",
    )(x, kb, vb, sq, ssg, cache_k, cache_v, sw["g"], *w_a, sw["conv_w"], sw["pool_w"], sw["pool_scale"], *w_to_cast)


def _sample_in_kernel(*refs, n_t, n_b, cast_weights):
    refs = list(refs)
    n_w = N_PROJ + N_CAST_ONLY if cast_weights else N_PROJ
    x_ref, tb_ref, bt_ref, cst_ref, pst_ref, g_ref = refs[:6]
    w_refs = refs[6:6 + n_w]
    convw_ref, poolw_ref, pscale_ref = refs[6 + n_w:9 + n_w]
    outs = refs[9 + n_w:]
    ac_ref, ap_ref, q_ref, sg_ref, ncst_ref, npst_ref = outs[:6]
    wb_refs = outs[6:-2]
    hb_s, ht_s = outs[-2:]
    c = pl.program_id(0)

    @pl.when(c == 0)
    def _():
        hb = _rmsnorm(x_ref[...], g_ref[...]).astype(BF16)
        hb_s[...] = hb
        ht_s[...] = _dot(tb_ref[...], hb).astype(BF16)

    hb, ht = hb_s[...], ht_s[...]
    w = [w_ref[...].astype(BF16) for w_ref in w_refs]
    for w_cast, wb_ref in zip(w, wb_refs):
        wb_ref[...] = w_cast
    whp, wq, whc, wcc, wbc, wgc, wgp, wga = w[:N_PROJ]

    def to_batch_major(a):
        return _dot(bt_ref[...], a.astype(BF16)).astype(BF16)

    def planes(a):
        return [a[i * n_b:(i + 1) * n_b] for i in range(n_t)]

    hp = _dot(ht, whp)
    u = _dot(ht, wcc) * _dot(ht, whc)
    bc = _dot(ht, wbc)
    silu_gc = _silu(_dot(ht, wgc))
    silu_gp = _silu(_dot(ht, wgp))
    q_ref[...] = _dot(hb, wq)
    sg_ref[...] = _silu(_dot(hb, wga))

    ext = [pst_ref[j] for j in range(POOL_STATE)] + planes(hp)
    for j, plane in enumerate(ext[-POOL_STATE:]):
        npst_ref[j] = plane
    wsum = {}

    def window_sum(k, i):
        if k == 0:
            return ext[i]
        if (k, i) not in wsum:
            wsum[(k, i)] = window_sum(k - 1, i) + window_sum(k - 1, i - 2 ** (k - 1))
        return wsum[(k, i)]

    mixed = []
    for i in range(n_t):
        mean = None
        for g, win in enumerate(POOL_WINDOWS):
            cnt = float(min(PAST_LEN + i + 1, win))
            cand = window_sum(g + 1, POOL_STATE + i) * (1.0 / cnt)
            mean = cand if mean is None else jnp.where(c == g, cand, mean)
        mixed.append(mean - ext[POOL_STATE + i])
    mixed = jnp.concatenate(mixed, axis=0).astype(BF16)
    pooled = _dot(mixed, poolw_ref[...].astype(BF16)) * pscale_ref[...]
    ap_ref[...] = to_batch_major(pooled * silu_gp)

    ext = [cst_ref[0], cst_ref[1]] + planes(u)
    y = jnp.concatenate(
        [convw_ref[0:1, :] * ext[i] + convw_ref[1:2, :] * ext[i + 1] + convw_ref[2:3, :] * ext[i + 2]
         for i in range(n_t)], axis=0)
    ac_ref[...] = to_batch_major(bc * y * silu_gc)
    for j, plane in enumerate(ext[-(CONV_WIDTH - 1):]):
        ncst_ref[j] = plane


def _row_permutations(n_t, n_b):
    r = jnp.arange(n_t * n_b)
    tb = jax.nn.one_hot((r % n_b) * n_t + r // n_b, n_t * n_b, dtype=BF16)
    return tb, tb.T


W_IN_BLOCK_OFFSETS = (OFF_HP, OFF_Q, OFF_HC, OFF_CC, OFF_BC, OFF_GC, OFF_GP, OFF_GA) + tuple(
    OFF_MERGE + k * D_MODEL for k in range(N_MERGE))
assert len(W_IN_BLOCK_OFFSETS) == N_PROJ + N_CAST_ONLY


def _sample_in(xs2d, cst_t, pst_t, g, w_in, w_bf16, conv_w, pool_w, pool_scale, layer, n_t, n_b):
    rows = n_t * n_b
    tb, bt = _row_permutations(n_t, n_b)
    cast_weights = w_bf16 is None
    if cast_weights:
        weights = [w_in] * len(W_IN_BLOCK_OFFSETS)
        w_specs = [pl.BlockSpec((None, D_MODEL, CW), functools.partial(lambda c, blk: (layer, 0, blk + c), blk=off // CW))
                   for off in W_IN_BLOCK_OFFSETS]
    else:
        weights = list(w_bf16[:N_PROJ])
        w_specs = [pl.BlockSpec((D_MODEL, CW), lambda c: (0, c))] * N_PROJ
    n_emit = len(weights) if cast_weights else 0
    chunk2d = pl.BlockSpec((rows, CW), lambda c: (0, c))

    def hist(n_rows):
        return pl.BlockSpec((None, n_rows, n_b, CW), lambda c: (layer, 0, 0, c))

    def new_hist(n_rows):
        return pl.BlockSpec((n_rows, n_b, CW), lambda c: (0, 0, c))

    outs = pl.pallas_call(
        functools.partial(_sample_in_kernel, n_t=n_t, n_b=n_b, cast_weights=cast_weights),
        grid=(N_CHUNKS,),
        in_specs=[
            _const_spec(xs2d.shape), _const_spec(tb.shape), _const_spec(bt.shape),
            hist(CONV_WIDTH - 1), hist(POOL_STATE), _layer_spec(g, layer),
        ] + w_specs + [
            pl.BlockSpec((None, CONV_WIDTH, CW), lambda c: (layer, 0, c)),
            pl.BlockSpec((None, None, POOL_GROUP, POOL_GROUP), lambda c: (layer, c, 0, 0)),
            pl.BlockSpec((None, 1, CW), lambda c: (layer, 0, c)),
        ],
        out_specs=([chunk2d] * 4 + [new_hist(CONV_WIDTH - 1), new_hist(POOL_STATE)]
                   + [pl.BlockSpec((D_MODEL, CW), lambda c: (0, c))] * n_emit),
        out_shape=[
            jax.ShapeDtypeStruct((rows, D_MODEL), BF16),
            jax.ShapeDtypeStruct((rows, D_MODEL), BF16),
            jax.ShapeDtypeStruct((rows, D_MODEL), F32),
            jax.ShapeDtypeStruct((rows, D_MODEL), F32),
            jax.ShapeDtypeStruct((CONV_WIDTH - 1, n_b, D_MODEL), F32),
            jax.ShapeDtypeStruct((POOL_STATE, n_b, D_MODEL), F32),
        ] + [jax.ShapeDtypeStruct((D_MODEL, D_MODEL), BF16)] * n_emit,
        scratch_shapes=[pltpu.VMEM((rows, D_MODEL), BF16), pltpu.VMEM((rows, D_MODEL), BF16)],
        compiler_params=_params(1),
        name="sample_in",
    )(xs2d, tb, bt, cst_t, pst_t, g, *weights, conv_w, pool_w, pool_scale)
    return tuple(outs[:6]) + (list(outs[6:]) if cast_weights else list(w_bf16),)


def _out_side_kernel(*refs, final_norm, n_prompt_tiles, n_cast):
    refs = list(refs)
    (x_ref, ac_ref, ap_ref, aa_ref, xs_ref, acs_ref, aps_ref, aas_ref,
     g_ref, wmc_ref, wmp_ref, wma_ref, wbc_ref, wbp_ref, wba_ref, wout_ref, fg_ref) = refs[:17]
    next_w_refs = refs[17:17 + n_cast]
    y_ref, ys_ref = refs[17 + n_cast:19 + n_cast]
    next_wb_refs = refs[19 + n_cast:19 + 2 * n_cast]
    m_s = refs[-1]

    def tile(x_ref, ac_ref, ap_ref, aa_ref, y_ref):
        rows = x_ref.shape[0]
        x = x_ref[...]
        h = _rmsnorm(x, g_ref[...]).astype(BF16)
        ac, ap, aa = ac_ref[...], ap_ref[...], aa_ref[...].astype(BF16)
        for c in range(N_CHUNKS):
            sl = _cols(0, c)
            conv_br = _dot(ac, wbc_ref[:, sl])
            pool_br = _dot(ap, wbp_ref[:, sl])
            att_br = _dot(aa, wba_ref[:, sl])
            mc = _dot(h, wmc_ref[:, sl])
            mp = _dot(h, wmp_ref[:, sl])
            ma = _dot(h, wma_ref[:, sl])
            merged = _sigmoid(mc) * conv_br + _sigmoid(mp) * pool_br + _sigmoid(ma) * att_br
            m_s[0:rows, sl] = merged.astype(BF16)
        xn = x + _dot(m_s[0:rows, :], wout_ref[...])
        if final_norm:
            xn = _rmsnorm(xn, fg_ref[...])
        y_ref[...] = xn

    r = pl.program_id(0)

    @pl.when(r < n_prompt_tiles)
    def _():
        for w_ref, wb_ref in zip(next_w_refs, next_wb_refs):
            wb_ref[...] = w_ref[...].astype(BF16)
        tile(x_ref, ac_ref, ap_ref, aa_ref, y_ref)

    @pl.when(r == n_prompt_tiles)
    def _():
        tile(xs_ref, acs_ref, aps_ref, aas_ref, ys_ref)


def _out_side(x2d, ac, ap, aa, xs2d, acs, aps, aas, w_merge, w_branch, sw, layer, final_norm, next_w_in=None):
    rows, rows_s = x2d.shape[0], xs2d.shape[0]
    n_tiles = rows // TM_OUT
    assert n_tiles * TM_OUT == rows and rows_s <= TM_OUT
    n_cast = 0 if next_w_in is None else len(W_IN_BLOCK_OFFSETS)
    cast_rows = D_MODEL // n_tiles
    assert cast_rows * n_tiles == D_MODEL
    cast_in = [pl.BlockSpec((None, cast_rows, D_MODEL),
                            functools.partial(lambda r, blk: (layer + 1, jnp.minimum(r, n_tiles - 1), blk),
                                              blk=off // D_MODEL))
               for off in W_IN_BLOCK_OFFSETS[:n_cast]]
    cast_out = [pl.BlockSpec((cast_rows, D_MODEL), lambda r: (jnp.minimum(r, n_tiles - 1), 0))] * n_cast
    tile = pl.BlockSpec((TM_OUT, D_MODEL), lambda r: (jnp.minimum(r, n_tiles - 1), 0))
    weights = (sw["g"], *w_merge, *w_branch, sw["fg"])
    w_specs = ([_layer_spec(sw["g"], layer)] + [_const_spec(w.shape) for w in (*w_merge, *w_branch)]
               + [_const_spec(sw["fg"].shape)])
    smp = _const_spec((rows_s, D_MODEL))
    y, ys, *next_w_bf16 = pl.pallas_call(
        functools.partial(_out_side_kernel, final_norm=final_norm, n_prompt_tiles=n_tiles, n_cast=n_cast),
        grid=(n_tiles + 1,),
        in_specs=[tile] * 4 + [smp] * 4 + w_specs + cast_in,
        out_specs=[tile, smp] + cast_out,
        out_shape=([jax.ShapeDtypeStruct((rows, D_MODEL), F32), jax.ShapeDtypeStruct((rows_s, D_MODEL), F32)]
                   + [jax.ShapeDtypeStruct((D_MODEL, D_MODEL), BF16)] * n_cast),
        scratch_shapes=[pltpu.VMEM((TM_OUT, D_MODEL), BF16)],
        compiler_params=_params(1),
        name="out_side",
    )(x2d, ac, ap, aa, xs2d, acs, aps, aas, *weights, *([next_w_in] * n_cast))
    return y, ys, (next_w_bf16 or None)


def kernel(x_prompt, x_sample, mem_prompt, cache_mem_k, cache_mem_v, state_conv, state_pool, norm_g, w_in, conv_w, pool_w, pool_scale, mem_norm_g, w_mem_kv, w_br_conv, w_br_pool, w_br_att, w_out, final_norm_g):
    depth = w_in.shape[0]
    b_p, t_p, _ = x_prompt.shape
    n_b, n_t, _ = x_sample.shape
    rows_p, rows_s = b_p * t_p, n_b * n_t

    k_f, v_f, k_b, v_b = _mem_kv(mem_prompt.reshape(b_p * N_MEM, D_MODEL), mem_norm_g, w_mem_kv)

    sw = dict(
        g=norm_g.reshape(depth, 1, D_MODEL), conv_w=conv_w, pool_w=pool_w.astype(BF16),
        pool_scale=pool_scale.reshape(depth, 1, D_MODEL), fg=final_norm_g.reshape(1, D_MODEL))
    w_branch_f32 = (w_br_conv, w_br_pool, w_br_att, w_out)

    cst_t = jnp.transpose(state_conv, (0, 2, 1, 3))
    pst_t = jnp.transpose(state_pool, (0, 2, 1, 3))

    xp, xs = x_prompt, x_sample.reshape(rows_s, D_MODEL)
    cv_p, pl_p, cv_s, pl_s = [], [], [], []
    flat = lambda a: a.reshape(rows_p, D_MODEL)
    w_bf16 = None
    for l in range(depth):
        ac_s, ap_s, q, sg, c_new_s, p_new_s, w_bf16 = _sample_in(
            xs, cst_t, pst_t, sw["g"], w_in, w_bf16, conv_w, pool_w, sw["pool_scale"], l, n_t, n_b)
        cv_s.append(c_new_s)
        pl_s.append(p_new_s)

        ac_p, ap_p, aa_p, aa_s, c_new, p_new, *w_branch = _prompt_in(
            xp, k_b, v_b, q, sg, cache_mem_k, cache_mem_v, w_bf16[:N_PROJ], w_branch_f32, sw, l, n_t)
        cv_p.append(c_new)
        pl_p.append(p_new)

        last = l == depth - 1
        xp, xs, w_bf16 = _out_side(flat(xp), flat(ac_p), flat(ap_p), flat(aa_p), xs, ac_s, ap_s, aa_s,
                                   w_bf16[N_PROJ:], w_branch, sw, l, last, None if last else w_in)
        xp = xp.reshape(b_p, t_p, D_MODEL)

    batch_first = lambda hist: jnp.transpose(jnp.stack(hist), (0, 2, 1, 3))
    return (xp, xs.reshape(n_b, n_t, D_MODEL), k_f, v_f, jnp.stack(cv_p), jnp.stack(pl_p),
            batch_first(cv_s), batch_first(pl_s))
```
